```python
import math
import jax, jax.numpy as jnp
from jax import lax
import numpy as np

D_MODEL = 1024
BATCH = 1
SEQ = 16384
DEPTH = 4

N_MIXERS = 3
N_MEM = 256
GRID_W = 64
HEAD_DIM = 64
ROPE_THETA = 10000.0
LN_EPS = 1e-5
RMS_EPS = 1e-6
MASK_VALUE = -1e30

FNET_GROUPS = 4
GQA_Q_HEADS = D_MODEL // HEAD_DIM
GQA_KV_HEADS = GQA_Q_HEADS // 4
Q_BLOCK = 128
DIL_PAIRS = ((128, 1), (512, 4), (2048, 16))
DIL_GROUPS = len(DIL_PAIRS)
DIL_HEADS = D_MODEL // HEAD_DIM
DIL_WIDTH = DIL_GROUPS * DIL_HEADS * HEAD_DIM
MEM_HEADS = 4
MEM_WIDTH = MEM_HEADS * HEAD_DIM
BRANCH_WIDTH = D_MODEL
INNER = BRANCH_WIDTH + MEM_WIDTH
TAIL = INNER + MEM_WIDTH
COLS_A = TAIL
COLS_B = GQA_Q_HEADS * HEAD_DIM + 2 * GQA_KV_HEADS * HEAD_DIM + TAIL
COLS_C = 3 * DIL_WIDTH + TAIL
N_LAYERS_A = (DEPTH + 2) // 3
N_LAYERS_B = (DEPTH + 1) // 3
N_LAYERS_C = DEPTH // 3
DEEPNORM_ALPHA = (2.0 * DEPTH) ** 0.25
DEEPNORM_BETA = (8.0 * DEPTH) ** -0.25

kernel_name = "hybrid_fnet_gqa_dilated_encoder"


def layer_norm(x, g, b):
    xf = x.astype(jnp.float32)
    mu = jnp.mean(xf, axis=-1, keepdims=True)
    var = jnp.mean(jnp.square(xf - mu), axis=-1, keepdims=True)
    y = (xf - mu) * lax.rsqrt(var + LN_EPS) * g.astype(jnp.float32) + b.astype(jnp.float32)
    return y.astype(x.dtype)


def rms_norm(x, g):
    xf = x.astype(jnp.float32)
    return xf * lax.rsqrt(jnp.mean(jnp.square(xf), axis=-1, keepdims=True) + RMS_EPS) * g.astype(jnp.float32)


def rope_angles(pos, dim):
    inv_freq = ROPE_THETA ** (-(jnp.arange(0, dim, 2, dtype=jnp.float32) / dim))
    return pos.astype(jnp.float32)[:, None] * inv_freq[None, :]


def apply_rope(x, ang):
    x1, x2 = jnp.split(x, 2, axis=-1)
    c = jnp.cos(ang)[:, None, :]
    s = jnp.sin(ang)[:, None, :]
    return jnp.concatenate([x1 * c - x2 * s, x2 * c + x1 * s], axis=-1)


def apply_axial_rope(x, ang_row, ang_col):
    half = x.shape[-1] // 2
    return jnp.concatenate([apply_rope(x[..., :half], ang_row), apply_rope(x[..., half:], ang_col)], axis=-1)


def fourier_mix(h):
    B, S, D = h.shape
    hg = h.astype(jnp.float32).reshape(B, S, FNET_GROUPS, D // FNET_GROUPS)
    y = jnp.fft.fftn(hg, axes=(1, 3), norm="ortho").real
    return y.astype(jnp.float32).reshape(B, S, D)


def gqa_block_attention(q, k, v):
    B, S, Hq, Dh = q.shape
    Hkv = k.shape[2]
    G = Hq // Hkv
    nb = S // Q_BLOCK
    qb = q.reshape(B, nb, Q_BLOCK, Hkv, G, Dh).transpose(1, 0, 3, 4, 2, 5)
    kt = k.transpose(0, 2, 1, 3)
    vt = v.transpose(0, 2, 1, 3)
    scale = Dh ** -0.5

    def one_block(qblk):
        s = jnp.einsum("bhgqd,bhkd->bhgqk", qblk, kt) * scale
        p = jax.nn.softmax(s, axis=-1)
        return jnp.einsum("bhgqk,bhkd->bhgqd", p, vt)

    o = lax.map(one_block, qb)
    return o.transpose(1, 0, 4, 2, 3, 5).reshape(B, S, Hq * Dh)


def dilated_group_attention(q, k, v, dil, side):
    B, S, H, Dh = q.shape
    L = S // dil
    nb = -(-L // side)
    Lp = nb * side

    def to_classes(t):
        t = t.reshape(B, L, dil, H, Dh).transpose(0, 2, 1, 3, 4)
        return jnp.pad(t, ((0, 0), (0, 0), (0, Lp - L), (0, 0), (0, 0)))

    qc, kc, vc = to_classes(q), to_classes(k), to_classes(v)
    qb = qc.reshape(B, dil, nb, side, H, Dh)

    def band(t):
        tp = jnp.pad(t, ((0, 0), (0, 0), (side, side), (0, 0), (0, 0)))
        return jnp.concatenate(
            [tp[:, :, i * side: i * side + Lp].reshape(B, dil, nb, side, H, Dh) for i in range(3)], axis=3)

    kb, vb = band(kc), band(vc)
    qi = jnp.arange(side)
    kj = jnp.arange(3 * side) - side
    in_window = jnp.abs(kj[None, :] - qi[:, None]) <= side
    kabs = jnp.arange(nb)[:, None] * side + kj[None, :]
    valid = (kabs >= 0) & (kabs < L)
    mask = in_window[None, :, :] & valid[:, None, :]

    s = jnp.einsum("bcnqhd,bcnkhd->bcnhqk", qb, kb) * (Dh ** -0.5)
    s = jnp.where(mask[None, None, :, None, :, :], s, MASK_VALUE)
    lse = jax.nn.logsumexp(s, axis=-1)
    p = jnp.exp(s - lse[..., None])
    o = jnp.einsum("bcnhqk,bcnkhd->bcnqhd", p, vb)
    o = o.reshape(B, dil, Lp, H, Dh)[:, :, :L].transpose(0, 2, 1, 3, 4).reshape(B, S, H, Dh)
    lse = lse.transpose(0, 1, 2, 4, 3).reshape(B, dil, Lp, H)[:, :, :L]
    lse = lse.transpose(0, 2, 1, 3).reshape(B, S, H)
    return o, lse


def memory_attention(qm, mk, mv):
    s = jnp.einsum("bshd,bmhd->bhsm", qm, mk) * (qm.shape[-1] ** -0.5)
    p = jax.nn.softmax(s, axis=-1)
    o = jnp.einsum("bhsm,bmhd->bshd", p, mv)
    return o.reshape(qm.shape[0], qm.shape[1], -1)


def setup_inputs(seed: int = 0) -> dict:
    key = jax.random.key(seed)
    ks = jax.random.split(key, 16)
    f32 = jnp.float32
    w_in_scale = D_MODEL ** -0.5
    return {
        "x": jax.random.normal(ks[0], (BATCH, SEQ, D_MODEL), f32),
        "mem": jax.random.normal(ks[1], (BATCH, N_MEM, D_MODEL), f32),
        "ln_in_g": 1.0 + 0.02 * jax.random.normal(ks[2], (D_MODEL,), f32),
        "ln_in_b": 0.02 * jax.random.normal(ks[3], (D_MODEL,), f32),
        "w_mem_kv": jax.random.normal(ks[4], (D_MODEL, 2 * MEM_WIDTH), f32) * w_in_scale,
        "w_in_a": jax.random.normal(ks[5], (N_LAYERS_A, D_MODEL, COLS_A), f32) * w_in_scale,
        "w_in_b": jax.random.normal(ks[6], (N_LAYERS_B, D_MODEL, COLS_B), f32) * w_in_scale,
        "q_norm_g": 1.0 + 0.02 * jax.random.normal(ks[7], (N_LAYERS_B, HEAD_DIM), f32),
        "k_norm_g": 1.0 + 0.02 * jax.random.normal(ks[8], (N_LAYERS_B, HEAD_DIM), f32),
        "w_in_c": jax.random.normal(ks[9], (N_LAYERS_C, D_MODEL, COLS_C), f32) * w_in_scale,
        "w_out": jax.random.normal(ks[10], (DEPTH, INNER, D_MODEL), f32) * (INNER ** -0.5) * DEEPNORM_BETA,
        "ln_g": 1.0 + 0.02 * jax.random.normal(ks[11], (DEPTH, D_MODEL), f32),
        "ln_b": 0.02 * jax.random.normal(ks[12], (DEPTH, D_MODEL), f32),
    }


def reference(x, mem, ln_in_g, ln_in_b, w_mem_kv, w_in_a, w_in_b, q_norm_g, k_norm_g, w_in_c, w_out, ln_g, ln_b):
    B, S, D = x.shape
    ROWS = S // GRID_W
    t = jnp.arange(S)
    row = jnp.repeat(jnp.arange(ROWS), GRID_W, total_repeat_length=S)
    col = jnp.tile(jnp.arange(GRID_W), ROWS)
    ang_1d = rope_angles(t, HEAD_DIM)
    ang_row = rope_angles(row, HEAD_DIM // 2)
    ang_col = rope_angles(col, HEAD_DIM // 2)

    mkv = jnp.matmul(mem, w_mem_kv).astype(jnp.float32).reshape(B, N_MEM, 2, MEM_HEADS, HEAD_DIM)
    mk, mv = mkv[:, :, 0], mkv[:, :, 1]

    h = layer_norm(x, ln_in_g, ln_in_b)
    qw = GQA_Q_HEADS * HEAD_DIM
    kvw = GQA_KV_HEADS * HEAD_DIM
    for i in range(DEPTH):
        kind = i % N_MIXERS
        j = i // N_MIXERS
        if kind == 0:
            proj = jnp.matmul(h, w_in_a[j]).astype(jnp.float32)
            branch = fourier_mix(h)
        elif kind == 1:
            proj = jnp.matmul(h, w_in_b[j]).astype(jnp.float32)
            q = proj[..., :qw].reshape(B, S, GQA_Q_HEADS, HEAD_DIM)
            k = proj[..., qw:qw + kvw].reshape(B, S, GQA_KV_HEADS, HEAD_DIM)
            v = proj[..., qw + kvw:qw + 2 * kvw].reshape(B, S, GQA_KV_HEADS, HEAD_DIM)
            q = apply_axial_rope(rms_norm(q, q_norm_g[j]), ang_row, ang_col)
            k = apply_axial_rope(rms_norm(k, k_norm_g[j]), ang_row, ang_col)
            branch = gqa_block_attention(q, k, v)
        else:
            proj = jnp.matmul(h, w_in_c[j]).astype(jnp.float32)
            qkv = proj[..., :3 * DIL_WIDTH].reshape(B, S, 3, DIL_GROUPS, DIL_HEADS, HEAD_DIM)
            outs = []
            lses = []
            for g, (window, dil) in enumerate(DIL_PAIRS):
                qg = apply_rope(qkv[:, :, 0, g], ang_1d)
                kg = apply_rope(qkv[:, :, 1, g], ang_1d)
                og, lg = dilated_group_attention(qg, kg, qkv[:, :, 2, g], dil, window // (2 * dil))
                outs.append(og)
                lses.append(lg)
            wts = jax.nn.softmax(jnp.stack(lses, axis=0), axis=0)
            branch = jnp.sum(wts[..., None] * jnp.stack(outs, axis=0), axis=0).reshape(B, S, BRANCH_WIDTH)
        tail = proj[..., proj.shape[-1] - TAIL:]
        gate = tail[..., :INNER]
        qm = tail[..., INNER:].reshape(B, S, MEM_HEADS, HEAD_DIM)
        mem_out = memory_attention(qm, mk, mv)
        y = jnp.concatenate([branch, mem_out], axis=-1) * jax.nn.silu(gate)
        y = jnp.matmul(y.astype(h.dtype), w_out[i])
        h = layer_norm(DEEPNORM_ALPHA * h + y.astype(h.dtype), ln_g[i], ln_b[i])
    return h
```

```python
import functools

import jax
import jax.numpy as jnp
import numpy as np
from jax import lax
from jax.experimental import pallas as pl
from jax.experimental.pallas import tpu as pltpu

F32 = jnp.float32
BF16 = jnp.bfloat16

D_MODEL = 1024
DEPTH = 4
N_MEM = 256
GRID_W = 64
HEAD_DIM = 64
ROPE_THETA = 10000.0
LN_EPS = 1e-5
RMS_EPS = 1e-6
MASK_VALUE = -1e30
FNET_GROUPS = 4
FNET_GROUP_W = D_MODEL // FNET_GROUPS
GQA_Q_HEADS = 16
GQA_KV_HEADS = 4
DIL_DILATIONS = (1, 4, 16)
DIL_SIDE = 64
MEM_WIDTH = 256
INNER = D_MODEL + MEM_WIDTH
TAIL = INNER + MEM_WIDTH
DEEPNORM_ALPHA = (2.0 * DEPTH) ** 0.25
ATTN_SCALE = HEAD_DIM ** -0.5

LANES = 128
SUBLANES = 8
FFT_N2 = 128
DIL_CHUNK = 1024
VMEM_LIMIT = 48 << 20


def _cparams(*sem):
    return pltpu.CompilerParams(dimension_semantics=sem, vmem_limit_bytes=VMEM_LIMIT)


def _dot(a, b):
    return jnp.dot(a, b, preferred_element_type=F32)


def _dot_nt(a, b):
    return lax.dot_general(a, b, (((1,), (1,)), ((), ())), preferred_element_type=F32)


def _lane_lo(shape=(1, LANES)):
    return lax.broadcasted_iota(jnp.int32, shape, len(shape) - 1) % LANES < HEAD_DIM


def _layer_norm_rows(z, g, b):
    mu = jnp.mean(z, axis=-1, keepdims=True)
    zc = z - mu
    var = jnp.mean(zc * zc, axis=-1, keepdims=True)
    return zc * lax.rsqrt(var + LN_EPS) * g + b


def _ln_kernel(x_ref, g_ref, b_ref, o_ref):
    o_ref[...] = _layer_norm_rows(x_ref[...], g_ref[...], b_ref[...])


def _ln_in(x, g, b, tr=512):
    S = x.shape[0]
    return pl.pallas_call(
        _ln_kernel,
        grid=(S // tr,),
        in_specs=[pl.BlockSpec((tr, D_MODEL), lambda i: (i, 0)),
                  pl.BlockSpec((1, D_MODEL), lambda i: (0, 0)),
                  pl.BlockSpec((1, D_MODEL), lambda i: (0, 0))],
        out_specs=pl.BlockSpec((tr, D_MODEL), lambda i: (i, 0)),
        out_shape=jax.ShapeDtypeStruct((S, D_MODEL), F32),
        compiler_params=_cparams("parallel"),
        name="ln_in",
    )(x, g.reshape(1, -1), b.reshape(1, -1))


def _proj_kernel(x_ref, w_ref, o_ref):
    o_ref[...] = _dot(x_ref[...].astype(BF16), w_ref[...])


def _proj(x, w, tr=512, name="proj"):
    S, K = x.shape
    N = w.shape[1]
    tr = min(tr, S)
    return pl.pallas_call(
        _proj_kernel,
        grid=(S // tr,),
        in_specs=[pl.BlockSpec((tr, K), lambda i: (i, 0)),
                  pl.BlockSpec((K, N), lambda i: (0, 0))],
        out_specs=pl.BlockSpec((tr, N), lambda i: (i, 0)),
        out_shape=jax.ShapeDtypeStruct((S, N), F32),
        compiler_params=_cparams("parallel"),
        name=name,
    )(x, w)


def _proj_a_kernel(x_ref, w_ref, fc_ref, tail_ref, z_ref):
    xb = x_ref[...].astype(BF16)
    tail_ref[...] = _dot(xb, w_ref[...])
    for g in range(FNET_GROUPS):
        cols = slice(g * FNET_GROUP_W, (g + 1) * FNET_GROUP_W)
        zg = _dot(xb[:, cols], fc_ref[...])
        z_ref[0, :, cols] = zg[:, :FNET_GROUP_W]
        z_ref[1, :, cols] = zg[:, FNET_GROUP_W:]


def _proj_a(h, w_tail, fc, tr=512):
    S = h.shape[0]
    return pl.pallas_call(
        _proj_a_kernel,
        grid=(S // tr,),
        in_specs=[pl.BlockSpec((tr, D_MODEL), lambda i: (i, 0)),
                  pl.BlockSpec((D_MODEL, TAIL), lambda i: (0, 0)),
                  pl.BlockSpec((FNET_GROUP_W, 2 * FNET_GROUP_W), lambda i: (0, 0))],
        out_specs=[pl.BlockSpec((tr, TAIL), lambda i: (i, 0)),
                   pl.BlockSpec((2, tr, D_MODEL), lambda i: (0, i, 0))],
        out_shape=[jax.ShapeDtypeStruct((S, TAIL), F32),
                   jax.ShapeDtypeStruct((2, S, D_MODEL), F32)],
        compiler_params=_cparams("parallel"),
        name="proj_a",
    )(h, w_tail, fc)


def _fft1_kernel(z_ref, g_ref, t_ref):
    two, n1, _, sub, tc = z_ref.shape
    x = z_ref[...].reshape(two * n1 * sub, tc).astype(BF16)
    t_ref[...] = _dot(g_ref[...], x).reshape(t_ref.shape)


def _fft_stage1(z, gk, tc=512):
    _, S, D = z.shape
    n1 = S // FFT_N2
    nu = FFT_N2 // SUBLANES
    z5 = z.reshape(2, n1, nu, SUBLANES, D)
    rows = 2 * n1 * SUBLANES
    t5 = pl.pallas_call(
        _fft1_kernel,
        grid=(nu, D // tc),
        in_specs=[pl.BlockSpec((2, n1, 1, SUBLANES, tc), lambda u, c: (0, 0, u, 0, c)),
                  pl.BlockSpec((rows, rows), lambda u, c: (0, 0))],
        out_specs=pl.BlockSpec((n1, 2, 1, SUBLANES, tc), lambda u, c: (0, 0, u, 0, c)),
        out_shape=jax.ShapeDtypeStruct((n1, 2, nu, SUBLANES, D), F32),
        compiler_params=_cparams("parallel", "parallel"),
        name="fft_stage1",
    )(z5, gk)
    return t5.reshape(n1, 2, FFT_N2, D)


def _fft2_kernel(t_ref, h_ref, o_ref, slab_ref):
    nj, _, n2, d = t_ref.shape
    nslab = d // LANES
    for j in range(nj):
        tj = t_ref[j].reshape(2 * n2, d).astype(BF16)
        r = _dot(h_ref[j], tj)
        for s in range(nslab):
            slab_ref[s, pl.ds(j, n2, stride=nj), :] = r[:, s * LANES:(s + 1) * LANES]
    for s in range(nslab):
        o_ref[:, :, s * LANES:(s + 1) * LANES] = slab_ref[s].reshape(n2, nj, LANES)


def _fft_stage2(t, hmat):
    n1, _, n2, D = t.shape
    nj = SUBLANES
    y3 = pl.pallas_call(
        _fft2_kernel,
        grid=(n1 // nj,),
        in_specs=[pl.BlockSpec((nj, 2, n2, D), lambda a: (a, 0, 0, 0)),
                  pl.BlockSpec((nj, n2, 2 * n2), lambda a: (a, 0, 0))],
        out_specs=pl.BlockSpec((n2, nj, D), lambda a: (0, a, 0)),
        out_shape=jax.ShapeDtypeStruct((n2, n1, D), F32),
        scratch_shapes=[pltpu.VMEM((D // LANES, n2 * nj, LANES), F32)],
        compiler_params=_cparams("parallel"),
        name="fft_stage2",
    )(t, hmat)
    return y3.reshape(n2 * n1, D)


def _fft_tables(S):
    n1, n2 = S // FFT_N2, FFT_N2
    c = jnp.arange(FNET_GROUP_W, dtype=jnp.int32)
    ang = (2.0 * np.pi / FNET_GROUP_W) * ((c[:, None] * c[None, :]) % FNET_GROUP_W).astype(F32)
    scale = 1.0 / np.sqrt(float(S) * FNET_GROUP_W)
    fc = jnp.concatenate([jnp.cos(ang), -jnp.sin(ang)], axis=1) * scale
    k1 = jnp.arange(n1, dtype=jnp.int32)
    th = (2.0 * np.pi / n1) * ((k1[:, None] * k1[None, :]) % n1).astype(F32)
    cs, sn = jnp.cos(th), jnp.sin(th)
    g = jnp.stack([jnp.stack([cs, sn], axis=1), jnp.stack([-sn, cs], axis=1)], axis=1)
    g = g.reshape(2 * n1, 2 * n1)
    gk = jnp.kron(g, jnp.eye(SUBLANES, dtype=F32))
    k2 = jnp.arange(n2, dtype=jnp.int32)
    kk = k1[:, None, None] + n1 * k2[None, :, None]
    ph = (2.0 * np.pi / S) * ((k2[None, None, :] * kk) % S).astype(F32)
    hm = jnp.concatenate([jnp.cos(ph), jnp.sin(ph)], axis=2)
    return fc.astype(BF16), gk.astype(BF16), hm.astype(BF16)


def _rope_angles(pos, dim):
    inv_freq = ROPE_THETA ** (-(jnp.arange(0, dim, 2, dtype=F32) / dim))
    return pos.astype(F32)[:, None] * inv_freq[None, :]


def _rope_tables_axial(S):
    t = jnp.arange(S)
    ar = _rope_angles(t // GRID_W, HEAD_DIM // 2)
    ac = _rope_angles(t % GRID_W, HEAD_DIM // 2)
    cos = jnp.concatenate([jnp.cos(ar), jnp.cos(ar), jnp.cos(ac), jnp.cos(ac)], axis=1)
    sin = jnp.concatenate([-jnp.sin(ar), jnp.sin(ar), -jnp.sin(ac), jnp.sin(ac)], axis=1)
    return jnp.tile(cos, (1, 2)), jnp.tile(sin, (1, 2))


def _rope_tables_1d(S):
    a = _rope_angles(jnp.arange(S), HEAD_DIM)
    cos = jnp.concatenate([jnp.cos(a), jnp.cos(a)], axis=1)
    sin = jnp.concatenate([-jnp.sin(a), jnp.sin(a)], axis=1)
    return jnp.tile(cos, (1, 2)), jnp.tile(sin, (1, 2))


def _rotate_partner(x, half):
    lane = lax.broadcasted_iota(jnp.int32, (1, LANES), 1)
    first = lane % (2 * half) < half
    return jnp.where(first, pltpu.roll(x, LANES - half, 1), pltpu.roll(x, half, 1))


def _proj_b_kernel(x_ref, w_ref, cos_ref, sin_ref, gq_ref, gk_ref, o_ref):
    xb = x_ref[...].astype(BF16)
    cos_t, sin_t = cos_ref[...], sin_ref[...]
    lo = _lane_lo()
    nb_w = 4 * LANES
    for nb in range(4):
        r = _dot(xb, w_ref[:, nb * nb_w:(nb + 1) * nb_w])
        if nb == 3:
            o_ref[:, nb * nb_w:(nb + 1) * nb_w] = r.astype(BF16)
            continue
        gain = gq_ref[...] if nb < 2 else gk_ref[...]
        scale = ATTN_SCALE if nb < 2 else 1.0
        for t in range(4):
            rt = r[:, t * LANES:(t + 1) * LANES]
            r2 = rt * rt
            tot = jnp.sum(r2, axis=1, keepdims=True)
            low = jnp.sum(jnp.where(lo, r2, 0.0), axis=1, keepdims=True)
            ss = jnp.where(lo, low, tot - low)
            xn = rt * lax.rsqrt(ss * (1.0 / HEAD_DIM) + RMS_EPS) * gain
            out = (xn * cos_t + _rotate_partner(xn, HEAD_DIM // 4) * sin_t) * scale
            c0 = nb * nb_w + t * LANES
            o_ref[:, c0:c0 + LANES] = out.astype(BF16)


def _proj_b(h, w_qkv, cos_t, sin_t, gq, gk, tr=512):
    S = h.shape[0]
    n_out = w_qkv.shape[1]
    return pl.pallas_call(
        _proj_b_kernel,
        grid=(S // tr,),
        in_specs=[pl.BlockSpec((tr, D_MODEL), lambda i: (i, 0)),
                  pl.BlockSpec((D_MODEL, n_out), lambda i: (0, 0)),
                  pl.BlockSpec((tr, LANES), lambda i: (i, 0)),
                  pl.BlockSpec((tr, LANES), lambda i: (i, 0)),
                  pl.BlockSpec((1, LANES), lambda i: (0, 0)),
                  pl.BlockSpec((1, LANES), lambda i: (0, 0))],
        out_specs=pl.BlockSpec((tr, n_out), lambda i: (i, 0)),
        out_shape=jax.ShapeDtypeStruct((S, n_out), BF16),
        compiler_params=_cparams("parallel"),
        name="proj_b",
    )(h, w_qkv, cos_t, sin_t, gq, gk)


def _gqa_kernel(q_ref, k_ref, v_ref, o_ref, qs_ref, m_ref, l_ref, acc_ref, *, tk):
    tq = q_ref.shape[0]
    S = k_ref.shape[0]
    lo = _lane_lo()
    for t in range(2):
        qt = q_ref[:, t * LANES:(t + 1) * LANES].astype(F32)
        qs_ref[(2 * t) * tq:(2 * t + 1) * tq, :] = jnp.where(lo, qt, 0.0).astype(BF16)
        qs_ref[(2 * t + 1) * tq:(2 * t + 2) * tq, :] = jnp.where(lo, 0.0, qt).astype(BF16)
    m_ref[...] = jnp.full(m_ref.shape, -jnp.inf, F32)
    l_ref[...] = jnp.zeros(l_ref.shape, F32)
    acc_ref[...] = jnp.zeros(acc_ref.shape, F32)

    def body(j, carry):
        k = k_ref[pl.ds(pl.multiple_of(j * tk, tk), tk), :]
        v = v_ref[pl.ds(pl.multiple_of(j * tk, tk), tk), :]
        s = _dot_nt(qs_ref[...], k)
        m_prev = m_ref[...]
        m_new = jnp.maximum(m_prev, jnp.max(s, axis=1, keepdims=True))
        p = jnp.exp(s - jnp.tile(m_new, (1, tk // LANES)))
        alpha = jnp.exp(m_prev - m_new)
        l_ref[...] = alpha * l_ref[...] + jnp.sum(p, axis=1, keepdims=True)
        acc_ref[...] = alpha * acc_ref[...] + _dot(p.astype(BF16), v)
        m_ref[...] = m_new
        return carry

    lax.fori_loop(0, S // tk, body, 0)
    o = acc_ref[...] / l_ref[...]
    for t in range(2):
        o_ref[:, t * LANES:(t + 1) * LANES] = jnp.where(
            lo, o[(2 * t) * tq:(2 * t + 1) * tq], o[(2 * t + 1) * tq:(2 * t + 2) * tq])


def _gqa_attention(qkv, tq=128, tk=512):
    S = qkv.shape[0]
    qw = 4 * HEAD_DIM
    k_blk0 = GQA_Q_HEADS * HEAD_DIM // LANES
    v_blk0 = k_blk0 + GQA_KV_HEADS
    tk = min(tk, S)
    return pl.pallas_call(
        functools.partial(_gqa_kernel, tk=tk),
        grid=(GQA_KV_HEADS, S // tq),
        in_specs=[pl.BlockSpec((tq, qw), lambda h, i: (i, h)),
                  pl.BlockSpec((S, LANES), lambda h, i: (0, k_blk0 + h)),
                  pl.BlockSpec((S, LANES), lambda h, i: (0, v_blk0 + h))],
        out_specs=pl.BlockSpec((tq, qw), lambda h, i: (i, h)),
        out_shape=jax.ShapeDtypeStruct((S, D_MODEL), F32),
        scratch_shapes=[pltpu.VMEM((4 * tq, LANES), BF16),
                        pltpu.VMEM((4 * tq, LANES), F32),
                        pltpu.VMEM((4 * tq, LANES), F32),
                        pltpu.VMEM((4 * tq, LANES), F32)],
        compiler_params=_cparams("parallel", "parallel"),
        name="gqa_attention",
    )(qkv, qkv, qkv)


def _proj_c_kernel(x_ref, w_ref, cos_ref, sin_ref, o_ref, slab_ref, *, dil):
    kind = pl.program_id(1)
    tm, d = x_ref.shape[0], o_ref.shape[-1]
    n = tm // dil
    nslab = d // LANES
    r = _dot(x_ref[...].astype(BF16), w_ref[...])

    def emit(vals):
        if dil == 1:
            o_ref[0, 0, 0] = vals.astype(BF16)
            return
        for s in range(nslab):
            slab_ref[s] = vals[:, s * LANES:(s + 1) * LANES]
        for c in range(dil):
            for s in range(nslab):
                o_ref[0, 0, c, :, s * LANES:(s + 1) * LANES] = (
                    slab_ref[s, pl.ds(c, n, stride=dil), :].astype(BF16))

    @pl.when(kind < 2)
    def _():
        scale = jnp.where(kind == 0, ATTN_SCALE, 1.0).astype(F32)
        cos_t = cos_ref[...] * scale
        sin_t = sin_ref[...] * scale
        parts = []
        for s in range(nslab):
            rt = r[:, s * LANES:(s + 1) * LANES]
            parts.append(rt * cos_t + _rotate_partner(rt, HEAD_DIM // 2) * sin_t)
        emit(jnp.concatenate(parts, axis=1))

    @pl.when(kind == 2)
    def _():
        emit(r)


def _proj_c(h, w, cos_t, sin_t, group, dil):
    S = h.shape[0]
    tm = DIL_CHUNK
    n = tm // dil
    ngroups = len(DIL_DILATIONS)
    return pl.pallas_call(
        functools.partial(_proj_c_kernel, dil=dil),
        grid=(S // tm, 3),
        in_specs=[pl.BlockSpec((tm, D_MODEL), lambda i, t: (i, 0)),
                  pl.BlockSpec((D_MODEL, D_MODEL), lambda i, t: (0, ngroups * t + group)),
                  pl.BlockSpec((tm, LANES), lambda i, t: (i, 0)),
                  pl.BlockSpec((tm, LANES), lambda i, t: (i, 0))],
        out_specs=pl.BlockSpec((1, 1, dil, n, D_MODEL), lambda i, t: (t, i, 0, 0, 0)),
        out_shape=jax.ShapeDtypeStruct((3, S // tm, dil, n, D_MODEL), BF16),
        scratch_shapes=[pltpu.VMEM((D_MODEL // LANES, tm, LANES), F32)],
        compiler_params=_cparams("parallel", "arbitrary"),
        name=f"proj_c_dil{dil}",
    )(h, w, cos_t, sin_t)


def _dil_kernel(*refs, nchunks):
    ng = len(DIL_DILATIONS)
    o_ref, osc_ref, lsc_ref = refs[7 * ng:]
    i = pl.program_id(0)
    lo = _lane_lo()
    for g, dil in enumerate(DIL_DILATIONS):
        q_ref, kc_ref, kp_ref, kn_ref, vc_ref, vp_ref, vn_ref = refs[7 * g:7 * g + 7]
        n = DIL_CHUNK // dil
        bq = min(n, 2 * DIL_SIDE)
        nsub = n // bq
        nk = bq + 2 * DIL_SIDE
        qi = lax.broadcasted_iota(jnp.int32, (bq, nk), 0)
        kj = lax.broadcasted_iota(jnp.int32, (bq, nk), 1)
        band = jnp.abs(kj - DIL_SIDE - qi) <= DIL_SIDE

        def do_class(c, g=g, dil=dil, n=n, bq=bq, nsub=nsub, nk=nk, kj=kj, band=band,
                     q_ref=q_ref, kc_ref=kc_ref, kp_ref=kp_ref, kn_ref=kn_ref,
                     vc_ref=vc_ref, vp_ref=vp_ref, vn_ref=vn_ref):
            kcat = jnp.concatenate([kp_ref[0, 0, c], kc_ref[0, 0, c], kn_ref[0, 0, c]], axis=0)
            vcat = jnp.concatenate([vp_ref[0, 0, c], vc_ref[0, 0, c], vn_ref[0, 0, c]], axis=0)
            for b in range(nsub):
                p0 = b * bq
                kw = kcat[p0:p0 + nk]
                vw = vcat[p0:p0 + nk]
                mask = band
                if b == 0:
                    mask = mask & (kj >= jnp.where(i == 0, DIL_SIDE, 0))
                if b == nsub - 1:
                    mask = mask & (kj < jnp.where(i == nchunks - 1, bq + DIL_SIDE, nk))
                qf = q_ref[0, 0, c, p0:p0 + bq, :].astype(F32)
                outs, lses = [], []
                for qh in (jnp.where(lo, qf, 0.0), jnp.where(lo, 0.0, qf)):
                    s = _dot_nt(qh.astype(BF16), kw)
                    s = jnp.where(mask, s, MASK_VALUE)
                    m = jnp.max(s, axis=1, keepdims=True)
                    p = jnp.exp(s - m)
                    l = jnp.sum(p, axis=1, keepdims=True)
                    outs.append(_dot(p.astype(BF16), vw) / l)
                    lses.append(m + jnp.log(l))
                o_pair = jnp.where(lo, outs[0], outs[1])
                l_pair = jnp.where(lo, lses[0], lses[1])
                if dil == 1:
                    osc_ref[g, p0:p0 + bq, :] = o_pair
                    lsc_ref[g, p0:p0 + bq, :] = l_pair
                else:
                    rows = pl.ds(p0 * dil + c, bq, stride=dil)
                    osc_ref[g, rows, :] = o_pair
                    lsc_ref[g, rows, :] = l_pair

        if dil >= 16:
            def loop_body(c, carry, do_class=do_class):
                do_class(c)
                return carry
            lax.fori_loop(0, dil, loop_body, 0)
        else:
            for c in range(dil):
                do_class(c)

    ls = [lsc_ref[g] for g in range(ng)]
    mx = functools.reduce(jnp.maximum, ls)
    ws = [jnp.exp(l - mx) for l in ls]
    num = sum(w * osc_ref[g] for g, w in enumerate(ws))
    o_ref[...] = num / sum(ws)


def _dilated_attention(qkvs):
    nchunks = qkvs[0].shape[1]
    S = nchunks * DIL_CHUNK
    in_specs, args = [], []
    for dil, a in zip(DIL_DILATIONS, qkvs):
        n = DIL_CHUNK // dil
        nblk = n // DIL_SIDE
        full = (1, 1, dil, n, LANES)
        halo = (1, 1, dil, DIL_SIDE, LANES)

        def cur(t):
            return lambda i, m: (t, i, 0, 0, m)

        def prev(t, nblk=nblk):
            return lambda i, m: (t, jnp.maximum(i - 1, 0), 0, nblk - 1, m)

        def nxt(t):
            return lambda i, m: (t, jnp.minimum(i + 1, nchunks - 1), 0, 0, m)

        in_specs += [pl.BlockSpec(full, cur(0)),
                     pl.BlockSpec(full, cur(1)), pl.BlockSpec(halo, prev(1)), pl.BlockSpec(halo, nxt(1)),
                     pl.BlockSpec(full, cur(2)), pl.BlockSpec(halo, prev(2)), pl.BlockSpec(halo, nxt(2))]
        args += [a] * 7
    ng = len(DIL_DILATIONS)
    return pl.pallas_call(
        functools.partial(_dil_kernel, nchunks=nchunks),
        grid=(nchunks, D_MODEL // LANES),
        in_specs=in_specs,
        out_specs=pl.BlockSpec((DIL_CHUNK, LANES), lambda i, m: (i, m)),
        out_shape=jax.ShapeDtypeStruct((S, D_MODEL), F32),
        scratch_shapes=[pltpu.VMEM((ng, DIL_CHUNK, LANES), F32),
                        pltpu.VMEM((ng, DIL_CHUNK, LANES), F32)],
        compiler_params=_cparams("parallel", "parallel"),
        name="dilated_attention",
    )(*args)


def _epilogue_kernel(br_ref, tail_ref, h_ref, mk_ref, mv_ref, wo_ref, g_ref, b_ref, o_ref):
    lo = _lane_lo()
    gate_b = tail_ref[:, :D_MODEL]
    yb = (br_ref[...] * (gate_b * jax.nn.sigmoid(gate_b))).astype(BF16)
    y = _dot(yb, wo_ref[:D_MODEL, :])
    for t in range(MEM_WIDTH // LANES):
        cols = slice(t * LANES, (t + 1) * LANES)
        qf = tail_ref[:, INNER + t * LANES:INNER + (t + 1) * LANES] * ATTN_SCALE
        kt, vt = mk_ref[:, cols], mv_ref[:, cols]
        outs = []
        for qh in (jnp.where(lo, qf, 0.0), jnp.where(lo, 0.0, qf)):
            s = _dot_nt(qh.astype(BF16), kt)
            m = jnp.max(s, axis=1, keepdims=True)
            p = jnp.exp(s - m)
            l = jnp.sum(p, axis=1, keepdims=True)
            outs.append(_dot(p.astype(BF16), vt) / l)
        mem_out = jnp.where(lo, outs[0], outs[1])
        gm = tail_ref[:, D_MODEL + t * LANES:D_MODEL + (t + 1) * LANES]
        ym = (mem_out * (gm * jax.nn.sigmoid(gm))).astype(BF16)
        y = y + _dot(ym, wo_ref[D_MODEL + t * LANES:D_MODEL + (t + 1) * LANES, :])
    z = DEEPNORM_ALPHA * h_ref[...] + y
    o_ref[...] = _layer_norm_rows(z, g_ref[...], b_ref[...])


def _epilogue(branch, tail, h, mk, mv, w_out, g, b, tr=512):
    S = h.shape[0]
    row = lambda i: (i, 0)
    fixed = lambda i: (0, 0)
    return pl.pallas_call(
        _epilogue_kernel,
        grid=(S // tr,),
        in_specs=[pl.BlockSpec((tr, D_MODEL), row),
                  pl.BlockSpec((tr, TAIL), row),
                  pl.BlockSpec((tr, D_MODEL), row),
                  pl.BlockSpec((N_MEM, MEM_WIDTH), fixed),
                  pl.BlockSpec((N_MEM, MEM_WIDTH), fixed),
                  pl.BlockSpec((INNER, D_MODEL), fixed),
                  pl.BlockSpec((1, D_MODEL), fixed),
                  pl.BlockSpec((1, D_MODEL), fixed)],
        out_specs=pl.BlockSpec((tr, D_MODEL), row),
        out_shape=jax.ShapeDtypeStruct((S, D_MODEL), F32),
        compiler_params=_cparams("parallel"),
        name="epilogue",
    )(branch, tail, h, mk, mv, w_out, g.reshape(1, -1), b.reshape(1, -1))


def _dup_kv_columns(w):
    d = w.shape[0]
    w4 = w.reshape(d, GQA_KV_HEADS, 1, HEAD_DIM)
    return jnp.broadcast_to(w4, (d, GQA_KV_HEADS, 2, HEAD_DIM)).reshape(d, 2 * GQA_KV_HEADS * HEAD_DIM)


def kernel(x, mem, ln_in_g, ln_in_b, w_mem_kv, w_in_a, w_in_b, q_norm_g, k_norm_g, w_in_c, w_out, ln_g, ln_b):
    B, S, D = x.shape
    assert B == 1 and D == D_MODEL and S % DIL_CHUNK == 0 and S % (FFT_N2 * SUBLANES) == 0

    mkv = _proj(mem[0], w_mem_kv.astype(BF16), name="proj_mem").astype(BF16)
    mk, mv = mkv[:, :MEM_WIDTH], mkv[:, MEM_WIDTH:]
    h = _ln_in(x[0], ln_in_g, ln_in_b)
    fft_tabs = None
    qw = GQA_Q_HEADS * HEAD_DIM
    kvw = GQA_KV_HEADS * HEAD_DIM

    for i in range(DEPTH):
        kind, j = i % 3, i // 3
        if kind == 0:
            if fft_tabs is None:
                fft_tabs = _fft_tables(S)
            fc, gk, hm = fft_tabs
            tail, z = _proj_a(h, w_in_a[j].astype(BF16), fc)
            branch = _fft_stage2(_fft_stage1(z, gk), hm)
        elif kind == 1:
            w = w_in_b[j]
            w_qkv = jnp.concatenate([w[:, :qw], _dup_kv_columns(w[:, qw:qw + kvw]),
                                     _dup_kv_columns(w[:, qw + kvw:qw + 2 * kvw])], axis=1).astype(BF16)
            cos_t, sin_t = _rope_tables_axial(S)
            gq = jnp.tile(q_norm_g[j], 2).reshape(1, LANES)
            gkn = jnp.tile(k_norm_g[j], 2).reshape(1, LANES)
            qkv = _proj_b(h, w_qkv, cos_t, sin_t, gq, gkn)
            tail = _proj(h, w[:, qw + 2 * kvw:].astype(BF16), name="proj_tail")
            branch = _gqa_attention(qkv)
        else:
            w = w_in_c[j].astype(BF16)
            cos_t, sin_t = _rope_tables_1d(S)
            qkvs = [_proj_c(h, w, cos_t, sin_t, g, dil) for g, dil in enumerate(DIL_DILATIONS)]
            tail = _proj(h, w[:, w.shape[1] - TAIL:], name="proj_tail")
            branch = _dilated_attention(qkvs)
        h = _epilogue(branch, tail, h, mk, mv, w_out[i].astype(BF16), ln_g[i], ln_b[i])
    return h[None]
```

```python
import functools

import jax
import jax.numpy as jnp
import numpy as np
from jax import lax
from jax.experimental import pallas as pl
from jax.experimental.pallas import tpu as pltpu

F32 = jnp.float32
BF16 = jnp.bfloat16

D_MODEL = 1024
DEPTH = 4
N_MEM = 256
GRID_W = 64
HEAD_DIM = 64
ROPE_THETA = 10000.0
LN_EPS = 1e-5
RMS_EPS = 1e-6
MASK_VALUE = -1e30
FNET_GROUPS = 4
FNET_GROUP_W = D_MODEL // FNET_GROUPS
GQA_Q_HEADS = 16
GQA_KV_HEADS = 4
DIL_DILATIONS = (1, 4, 16)
DIL_SIDE = 64
MEM_WIDTH = 256
INNER = D_MODEL + MEM_WIDTH
TAIL = INNER + MEM_WIDTH
DEEPNORM_ALPHA = (2.0 * DEPTH) ** 0.25
ATTN_SCALE = HEAD_DIM ** -0.5
LOG2_E = float(np.log2(np.e))

LANES = 128
SUBLANES = 8
FFT_N2 = 128
DIL_CHUNK = 1024
VMEM_LIMIT = 48 << 20


def _cparams(*sem):
    return pltpu.CompilerParams(dimension_semantics=sem, vmem_limit_bytes=VMEM_LIMIT)


def _dot(a, b):
    return jnp.dot(a, b, preferred_element_type=F32)


def _dot_nt(a, b):
    return lax.dot_general(a, b, (((1,), (1,)), ((), ())), preferred_element_type=F32)


def _lane_lo(shape=(1, LANES)):
    return lax.broadcasted_iota(jnp.int32, shape, len(shape) - 1) % LANES < HEAD_DIM


def _layer_norm_rows(z, g, b):
    mu = jnp.mean(z, axis=-1, keepdims=True)
    zc = z - mu
    var = jnp.mean(zc * zc, axis=-1, keepdims=True)
    return zc * lax.rsqrt(var + LN_EPS) * g + b


def _ln_kernel(x_ref, g_ref, b_ref, o_ref):
    o_ref[...] = _layer_norm_rows(x_ref[...], g_ref[...], b_ref[...])


def _ln_in(x, g, b, tr=512):
    S = x.shape[0]
    return pl.pallas_call(
        _ln_kernel,
        grid=(S // tr,),
        in_specs=[pl.BlockSpec((tr, D_MODEL), lambda i: (i, 0)),
                  pl.BlockSpec((1, D_MODEL), lambda i: (0, 0)),
                  pl.BlockSpec((1, D_MODEL), lambda i: (0, 0))],
        out_specs=pl.BlockSpec((tr, D_MODEL), lambda i: (i, 0)),
        out_shape=jax.ShapeDtypeStruct((S, D_MODEL), F32),
        compiler_params=_cparams("parallel"),
        name="ln_in",
    )(x, g.reshape(1, -1), b.reshape(1, -1))


def _proj_kernel(x_ref, w_ref, o_ref):
    o_ref[...] = _dot(x_ref[...].astype(BF16), w_ref[...])


def _proj(x, w, tr=512, name="proj"):
    S, K = x.shape
    N = w.shape[1]
    tr = min(tr, S)
    return pl.pallas_call(
        _proj_kernel,
        grid=(S // tr,),
        in_specs=[pl.BlockSpec((tr, K), lambda i: (i, 0)),
                  pl.BlockSpec((K, N), lambda i: (0, 0))],
        out_specs=pl.BlockSpec((tr, N), lambda i: (i, 0)),
        out_shape=jax.ShapeDtypeStruct((S, N), F32),
        compiler_params=_cparams("parallel"),
        name=name,
    )(x, w)


def _proj_a_kernel(x_ref, w_ref, fc_ref, tail_ref, z_ref):
    xb = x_ref[...].astype(BF16)
    tail_ref[...] = _dot(xb, w_ref[...])
    for g in range(FNET_GROUPS):
        cols = slice(g * FNET_GROUP_W, (g + 1) * FNET_GROUP_W)
        zg = _dot(xb[:, cols], fc_ref[...])
        z_ref[0, :, cols] = zg[:, :FNET_GROUP_W]
        z_ref[1, :, cols] = zg[:, FNET_GROUP_W:]


def _proj_a(h, w_tail, fc, tr=512):
    S = h.shape[0]
    return pl.pallas_call(
        _proj_a_kernel,
        grid=(S // tr,),
        in_specs=[pl.BlockSpec((tr, D_MODEL), lambda i: (i, 0)),
                  pl.BlockSpec((D_MODEL, TAIL), lambda i: (0, 0)),
                  pl.BlockSpec((FNET_GROUP_W, 2 * FNET_GROUP_W), lambda i: (0, 0))],
        out_specs=[pl.BlockSpec((tr, TAIL), lambda i: (i, 0)),
                   pl.BlockSpec((2, tr, D_MODEL), lambda i: (0, i, 0))],
        out_shape=[jax.ShapeDtypeStruct((S, TAIL), F32),
                   jax.ShapeDtypeStruct((2, S, D_MODEL), F32)],
        compiler_params=_cparams("parallel"),
        name="proj_a",
    )(h, w_tail, fc)


def _fft1_kernel(z_ref, g_ref, t_ref):
    two, n1, _, sub, tc = z_ref.shape
    x = z_ref[...].reshape(two * n1 * sub, tc).astype(BF16)
    t_ref[...] = _dot(g_ref[...], x).reshape(t_ref.shape)


def _fft_stage1(z, gk, tc=512):
    _, S, D = z.shape
    n1 = S // FFT_N2
    nu = FFT_N2 // SUBLANES
    z5 = z.reshape(2, n1, nu, SUBLANES, D)
    rows = 2 * n1 * SUBLANES
    t5 = pl.pallas_call(
        _fft1_kernel,
        grid=(nu, D // tc),
        in_specs=[pl.BlockSpec((2, n1, 1, SUBLANES, tc), lambda u, c: (0, 0, u, 0, c)),
                  pl.BlockSpec((rows, rows), lambda u, c: (0, 0))],
        out_specs=pl.BlockSpec((n1, 2, 1, SUBLANES, tc), lambda u, c: (0, 0, u, 0, c)),
        out_shape=jax.ShapeDtypeStruct((n1, 2, nu, SUBLANES, D), F32),
        compiler_params=_cparams("parallel", "parallel"),
        name="fft_stage1",
    )(z5, gk)
    return t5.reshape(n1, 2, FFT_N2, D)


def _fft2_kernel(t_ref, h_ref, o_ref, slab_ref):
    nj, _, n2, d = t_ref.shape
    nslab = d // LANES
    for j in range(nj):
        tj = t_ref[j].reshape(2 * n2, d).astype(BF16)
        r = _dot(h_ref[j], tj)
        for s in range(nslab):
            slab_ref[s, pl.ds(j, n2, stride=nj), :] = r[:, s * LANES:(s + 1) * LANES]
    for s in range(nslab):
        o_ref[:, :, s * LANES:(s + 1) * LANES] = slab_ref[s].reshape(n2, nj, LANES)


def _fft_stage2(t, hmat):
    n1, _, n2, D = t.shape
    nj = SUBLANES
    y3 = pl.pallas_call(
        _fft2_kernel,
        grid=(n1 // nj,),
        in_specs=[pl.BlockSpec((nj, 2, n2, D), lambda a: (a, 0, 0, 0)),
                  pl.BlockSpec((nj, n2, 2 * n2), lambda a: (a, 0, 0))],
        out_specs=pl.BlockSpec((n2, nj, D), lambda a: (0, a, 0)),
        out_shape=jax.ShapeDtypeStruct((n2, n1, D), F32),
        scratch_shapes=[pltpu.VMEM((D // LANES, n2 * nj, LANES), F32)],
        compiler_params=_cparams("parallel"),
        name="fft_stage2",
    )(t, hmat)
    return y3.reshape(n2 * n1, D)


def _fft_tables(S):
    n1, n2 = S // FFT_N2, FFT_N2
    c = jnp.arange(FNET_GROUP_W, dtype=jnp.int32)
    ang = (2.0 * np.pi / FNET_GROUP_W) * ((c[:, None] * c[None, :]) % FNET_GROUP_W).astype(F32)
    scale = 1.0 / np.sqrt(float(S) * FNET_GROUP_W)
    fc = jnp.concatenate([jnp.cos(ang), -jnp.sin(ang)], axis=1) * scale
    k1 = jnp.arange(n1, dtype=jnp.int32)
    th = (2.0 * np.pi / n1) * ((k1[:, None] * k1[None, :]) % n1).astype(F32)
    cs, sn = jnp.cos(th), jnp.sin(th)
    g = jnp.stack([jnp.stack([cs, sn], axis=1), jnp.stack([-sn, cs], axis=1)], axis=1)
    g = g.reshape(2 * n1, 2 * n1)
    gk = jnp.kron(g, jnp.eye(SUBLANES, dtype=F32))
    k2 = jnp.arange(n2, dtype=jnp.int32)
    kk = k1[:, None, None] + n1 * k2[None, :, None]
    ph = (2.0 * np.pi / S) * ((k2[None, None, :] * kk) % S).astype(F32)
    hm = jnp.concatenate([jnp.cos(ph), jnp.sin(ph)], axis=2)
    return fc.astype(BF16), gk.astype(BF16), hm.astype(BF16)


def _rope_angles(pos, dim):
    inv_freq = ROPE_THETA ** (-(jnp.arange(0, dim, 2, dtype=F32) / dim))
    return pos.astype(F32)[:, None] * inv_freq[None, :]


def _rope_tables_axial(S):
    t = jnp.arange(S)
    ar = _rope_angles(t // GRID_W, HEAD_DIM // 2)
    ac = _rope_angles(t % GRID_W, HEAD_DIM // 2)
    cos = jnp.concatenate([jnp.cos(ar), jnp.cos(ar), jnp.cos(ac), jnp.cos(ac)], axis=1)
    sin = jnp.concatenate([-jnp.sin(ar), jnp.sin(ar), -jnp.sin(ac), jnp.sin(ac)], axis=1)
    return jnp.tile(cos, (1, 2)), jnp.tile(sin, (1, 2))


def _rope_tables_1d(S):
    a = _rope_angles(jnp.arange(S), HEAD_DIM)
    cos = jnp.concatenate([jnp.cos(a), jnp.cos(a)], axis=1)
    sin = jnp.concatenate([-jnp.sin(a), jnp.sin(a)], axis=1)
    return jnp.tile(cos, (1, 2)), jnp.tile(sin, (1, 2))


def _rotate_partner(x, half):
    lane = lax.broadcasted_iota(jnp.int32, (1, LANES), 1)
    first = lane % (2 * half) < half
    return jnp.where(first, pltpu.roll(x, LANES - half, 1), pltpu.roll(x, half, 1))


def _proj_b_kernel(x_ref, w_ref, cos_ref, sin_ref, gq_ref, gk_ref, o_ref, vt_ref):
    xb = x_ref[...].astype(BF16)
    cos_t, sin_t = cos_ref[...], sin_ref[...]
    lo = _lane_lo()
    nb_w = 4 * LANES
    vt_ref[...] = _dot(xb, w_ref[:, 3 * nb_w:]).T.astype(BF16)
    for nb in range(3):
        r = _dot(xb, w_ref[:, nb * nb_w:(nb + 1) * nb_w])
        gain = gq_ref[...] if nb < 2 else gk_ref[...]
        scale = ATTN_SCALE * LOG2_E if nb < 2 else 1.0
        for t in range(4):
            rt = r[:, t * LANES:(t + 1) * LANES]
            r2 = rt * rt
            tot = jnp.sum(r2, axis=1, keepdims=True)
            low = jnp.sum(jnp.where(lo, r2, 0.0), axis=1, keepdims=True)
            ss = jnp.where(lo, low, tot - low)
            xn = rt * lax.rsqrt(ss * (1.0 / HEAD_DIM) + RMS_EPS) * gain
            out = (xn * cos_t + _rotate_partner(xn, HEAD_DIM // 4) * sin_t) * scale
            c0 = nb * nb_w + t * LANES
            o_ref[:, c0:c0 + LANES] = out.astype(BF16)


def _proj_b(h, w_qkv, cos_t, sin_t, gq, gk, tr=512):
    S = h.shape[0]
    n_in = w_qkv.shape[1]
    vw = GQA_KV_HEADS * HEAD_DIM
    n_out = n_in - vw
    return pl.pallas_call(
        _proj_b_kernel,
        grid=(S // tr,),
        in_specs=[pl.BlockSpec((tr, D_MODEL), lambda i: (i, 0)),
                  pl.BlockSpec((D_MODEL, n_in), lambda i: (0, 0)),
                  pl.BlockSpec((tr, LANES), lambda i: (i, 0)),
                  pl.BlockSpec((tr, LANES), lambda i: (i, 0)),
                  pl.BlockSpec((1, LANES), lambda i: (0, 0)),
                  pl.BlockSpec((1, LANES), lambda i: (0, 0))],
        out_specs=[pl.BlockSpec((tr, n_out), lambda i: (i, 0)),
                   pl.BlockSpec((vw, tr), lambda i: (0, i))],
        out_shape=[jax.ShapeDtypeStruct((S, n_out), BF16),
                   jax.ShapeDtypeStruct((vw, S), BF16)],
        compiler_params=_cparams("parallel"),
        name="proj_b",
    )(h, w_qkv, cos_t, sin_t, gq, gk)


def _gqa_kernel(q_ref, k_ref, vt_ref, o_ref, qs_ref, m_ref, l_ref, acc_ref, s0_ref, s1_ref, *, tk):
    tq = q_ref.shape[0]
    S = k_ref.shape[0]
    nq = 4 * tq
    lo = _lane_lo()
    for t in range(2):
        qt = q_ref[:, t * LANES:(t + 1) * LANES].astype(F32)
        qs_ref[(2 * t) * tq:(2 * t + 1) * tq, :] = jnp.where(lo, qt, 0.0).astype(BF16)
        qs_ref[(2 * t + 1) * tq:(2 * t + 2) * tq, :] = jnp.where(lo, 0.0, qt).astype(BF16)
    m_ref[...] = jnp.full(m_ref.shape, -jnp.inf, F32)
    l_ref[...] = jnp.zeros(l_ref.shape, F32)
    acc_ref[...] = jnp.zeros(acc_ref.shape, F32)

    nkv = S // tk

    def scores(j, s_ref):
        k0 = pl.multiple_of(j * tk, tk)
        s = _dot_nt(k_ref[pl.ds(k0, tk), :], qs_ref[...])
        s_ref[...] = s
        return jnp.max(s.reshape(tk // SUBLANES, SUBLANES, nq), axis=0)

    def accumulate(j, s_ref, cmax):
        k0 = pl.multiple_of(j * tk, tk)
        s3 = s_ref[...].reshape(tk // SUBLANES, SUBLANES, nq)
        m_prev = m_ref[...]
        m_new = jnp.maximum(m_prev, jnp.max(cmax, axis=0, keepdims=True))
        p3 = jnp.exp2(s3 - m_new[None])
        alpha = jnp.exp2(m_prev - m_new)
        l_ref[...] = alpha * l_ref[...] + jnp.sum(p3, axis=0)
        pv = _dot(vt_ref[:, pl.ds(k0, tk)], p3.reshape(tk, nq).astype(BF16))
        acc_ref[...] = jnp.tile(alpha, (HEAD_DIM // SUBLANES, 1)) * acc_ref[...] + pv
        m_ref[...] = m_new

    def body(jj, cmax0):
        j = 2 * jj
        cmax1 = scores(j + 1, s1_ref)
        accumulate(j, s0_ref, cmax0)
        cmax0 = scores(jnp.minimum(j + 2, nkv - 1), s0_ref)
        accumulate(j + 1, s1_ref, cmax1)
        return cmax0

    lax.fori_loop(0, nkv // 2, body, scores(0, s0_ref))
    ot = acc_ref[...] / jnp.sum(l_ref[...], axis=0, keepdims=True)
    for t in range(2):
        pair = jnp.concatenate([ot[:, (2 * t) * tq:(2 * t + 1) * tq],
                                ot[:, (2 * t + 1) * tq:(2 * t + 2) * tq]], axis=0)
        o_ref[:, t * LANES:(t + 1) * LANES] = pair.T


def _gqa_attention(qk, vt, tq=128, tk=512):
    S = qk.shape[0]
    qw = 4 * HEAD_DIM
    k_blk0 = GQA_Q_HEADS * HEAD_DIM // LANES
    tk = min(tk, S)
    return pl.pallas_call(
        functools.partial(_gqa_kernel, tk=tk),
        grid=(GQA_KV_HEADS, S // tq),
        in_specs=[pl.BlockSpec((tq, qw), lambda h, i: (i, h)),
                  pl.BlockSpec((S, LANES), lambda h, i: (0, k_blk0 + h)),
                  pl.BlockSpec((HEAD_DIM, S), lambda h, i: (h, 0))],
        out_specs=pl.BlockSpec((tq, qw), lambda h, i: (i, h)),
        out_shape=jax.ShapeDtypeStruct((S, D_MODEL), F32),
        scratch_shapes=[pltpu.VMEM((4 * tq, LANES), BF16),
                        pltpu.VMEM((SUBLANES, 4 * tq), F32),
                        pltpu.VMEM((SUBLANES, 4 * tq), F32),
                        pltpu.VMEM((HEAD_DIM, 4 * tq), F32),
                        pltpu.VMEM((tk, 4 * tq), F32),
                        pltpu.VMEM((tk, 4 * tq), F32)],
        compiler_params=_cparams("parallel", "parallel"),
        name="gqa_attention",
    )(qk, qk, vt)


def _proj_c_kernel(x_ref, w_ref, cos_ref, sin_ref, o_ref, slab_ref, *, dil):
    kind = pl.program_id(1)
    tm, d = x_ref.shape[0], o_ref.shape[-1]
    n = tm // dil
    nslab = d // LANES
    r = _dot(x_ref[...].astype(BF16), w_ref[...])

    def emit(vals):
        if dil == 1:
            o_ref[0, 0, 0] = vals.astype(BF16)
            return
        for s in range(nslab):
            slab_ref[s] = vals[:, s * LANES:(s + 1) * LANES]
        for c in range(dil):
            for s in range(nslab):
                o_ref[0, 0, c, :, s * LANES:(s + 1) * LANES] = (
                    slab_ref[s, pl.ds(c, n, stride=dil), :].astype(BF16))

    @pl.when(kind < 2)
    def _():
        scale = jnp.where(kind == 0, ATTN_SCALE, 1.0).astype(F32)
        cos_t = cos_ref[...] * scale
        sin_t = sin_ref[...] * scale
        parts = []
        for s in range(nslab):
            rt = r[:, s * LANES:(s + 1) * LANES]
            parts.append(rt * cos_t + _rotate_partner(rt, HEAD_DIM // 2) * sin_t)
        emit(jnp.concatenate(parts, axis=1))

    @pl.when(kind == 2)
    def _():
        emit(r)


def _proj_c(h, w, cos_t, sin_t, group, dil):
    S = h.shape[0]
    tm = DIL_CHUNK
    n = tm // dil
    ngroups = len(DIL_DILATIONS)
    return pl.pallas_call(
        functools.partial(_proj_c_kernel, dil=dil),
        grid=(S // tm, 3),
        in_specs=[pl.BlockSpec((tm, D_MODEL), lambda i, t: (i, 0)),
                  pl.BlockSpec((D_MODEL, D_MODEL), lambda i, t: (0, ngroups * t + group)),
                  pl.BlockSpec((tm, LANES), lambda i, t: (i, 0)),
                  pl.BlockSpec((tm, LANES), lambda i, t: (i, 0))],
        out_specs=pl.BlockSpec((1, 1, dil, n, D_MODEL), lambda i, t: (t, i, 0, 0, 0)),
        out_shape=jax.ShapeDtypeStruct((3, S // tm, dil, n, D_MODEL), BF16),
        scratch_shapes=[pltpu.VMEM((D_MODEL // LANES, tm, LANES), F32)],
        compiler_params=_cparams("parallel", "arbitrary"),
        name=f"proj_c_dil{dil}",
    )(h, w, cos_t, sin_t)


def _dil_kernel(*refs, nchunks):
    ng = len(DIL_DILATIONS)
    o_ref, osc_ref, lsc_ref = refs[7 * ng:]
    i = pl.program_id(0)
    lo = _lane_lo()
    for g, dil in enumerate(DIL_DILATIONS):
        q_ref, kc_ref, kp_ref, kn_ref, vc_ref, vp_ref, vn_ref = refs[7 * g:7 * g + 7]
        n = DIL_CHUNK // dil
        bq = min(n, 2 * DIL_SIDE)
        nsub = n // bq
        nk = bq + 2 * DIL_SIDE
        qi = lax.broadcasted_iota(jnp.int32, (bq, nk), 0)
        kj = lax.broadcasted_iota(jnp.int32, (bq, nk), 1)
        band = jnp.abs(kj - DIL_SIDE - qi) <= DIL_SIDE

        def do_class(c, g=g, dil=dil, n=n, bq=bq, nsub=nsub, nk=nk, kj=kj, band=band,
                     q_ref=q_ref, kc_ref=kc_ref, kp_ref=kp_ref, kn_ref=kn_ref,
                     vc_ref=vc_ref, vp_ref=vp_ref, vn_ref=vn_ref):
            kcat = jnp.concatenate([kp_ref[0, 0, c], kc_ref[0, 0, c], kn_ref[0, 0, c]], axis=0)
            vcat = jnp.concatenate([vp_ref[0, 0, c], vc_ref[0, 0, c], vn_ref[0, 0, c]], axis=0)
            for b in range(nsub):
                p0 = b * bq
                kw = kcat[p0:p0 + nk]
                vw = vcat[p0:p0 + nk]
                mask = band
                if b == 0:
                    mask = mask & (kj >= jnp.where(i == 0, DIL_SIDE, 0))
                if b == nsub - 1:
                    mask = mask & (kj < jnp.where(i == nchunks - 1, bq + DIL_SIDE, nk))
                qf = q_ref[0, 0, c, p0:p0 + bq, :].astype(F32)
                outs, lses = [], []
                for qh in (jnp.where(lo, qf, 0.0), jnp.where(lo, 0.0, qf)):
                    s = _dot_nt(qh.astype(BF16), kw)
                    s = jnp.where(mask, s, MASK_VALUE)
                    m = jnp.max(s, axis=1, keepdims=True)
                    p = jnp.exp(s - m)
                    l = jnp.sum(p, axis=1, keepdims=True)
                    outs.append(_dot(p.astype(BF16), vw) / l)
                    lses.append(m + jnp.log(l))
                o_pair = jnp.where(lo, outs[0], outs[1])
                l_pair = jnp.where(lo, lses[0], lses[1])
                if dil == 1:
                    osc_ref[g, p0:p0 + bq, :] = o_pair
                    lsc_ref[g, p0:p0 + bq, :] = l_pair
                else:
                    rows = pl.ds(p0 * dil + c, bq, stride=dil)
                    osc_ref[g, rows, :] = o_pair
                    lsc_ref[g, rows, :] = l_pair

        if dil >= 16:
            def loop_body(c, carry, do_class=do_class):
                do_class(c)
                return carry
            lax.fori_loop(0, dil, loop_body, 0)
        else:
            for c in range(dil):
                do_class(c)

    ls = [lsc_ref[g] for g in range(ng)]
    mx = functools.reduce(jnp.maximum, ls)
    ws = [jnp.exp(l - mx) for l in ls]
    num = sum(w * osc_ref[g] for g, w in enumerate(ws))
    o_ref[...] = num / sum(ws)


def _dilated_attention(qkvs):
    nchunks = qkvs[0].shape[1]
    S = nchunks * DIL_CHUNK
    in_specs, args = [], []
    for dil, a in zip(DIL_DILATIONS, qkvs):
        n = DIL_CHUNK // dil
        nblk = n // DIL_SIDE
        full = (1, 1, dil, n, LANES)
        halo = (1, 1, dil, DIL_SIDE, LANES)

        def cur(t):
            return lambda i, m: (t, i, 0, 0, m)

        def prev(t, nblk=nblk):
            return lambda i, m: (t, jnp.maximum(i - 1, 0), 0, nblk - 1, m)

        def nxt(t):
            return lambda i, m: (t, jnp.minimum(i + 1, nchunks - 1), 0, 0, m)

        in_specs += [pl.BlockSpec(full, cur(0)),
                     pl.BlockSpec(full, cur(1)), pl.BlockSpec(halo, prev(1)), pl.BlockSpec(halo, nxt(1)),
                     pl.BlockSpec(full, cur(2)), pl.BlockSpec(halo, prev(2)), pl.BlockSpec(halo, nxt(2))]
        args += [a] * 7
    ng = len(DIL_DILATIONS)
    return pl.pallas_call(
        functools.partial(_dil_kernel, nchunks=nchunks),
        grid=(nchunks, D_MODEL // LANES),
        in_specs=in_specs,
        out_specs=pl.BlockSpec((DIL_CHUNK, LANES), lambda i, m: (i, m)),
        out_shape=jax.ShapeDtypeStruct((S, D_MODEL), F32),
        scratch_shapes=[pltpu.VMEM((ng, DIL_CHUNK, LANES), F32),
                        pltpu.VMEM((ng, DIL_CHUNK, LANES), F32)],
        compiler_params=_cparams("parallel", "parallel"),
        name="dilated_attention",
    )(*args)


def _epilogue_kernel(br_ref, tail_ref, h_ref, mk_ref, mv_ref, wo_ref, g_ref, b_ref, o_ref):
    lo = _lane_lo()
    gate_b = tail_ref[:, :D_MODEL]
    yb = (br_ref[...] * (gate_b * jax.nn.sigmoid(gate_b))).astype(BF16)
    y = _dot(yb, wo_ref[:D_MODEL, :])
    for t in range(MEM_WIDTH // LANES):
        cols = slice(t * LANES, (t + 1) * LANES)
        qf = tail_ref[:, INNER + t * LANES:INNER + (t + 1) * LANES] * ATTN_SCALE
        kt, vt = mk_ref[:, cols], mv_ref[:, cols]
        outs = []
        for qh in (jnp.where(lo, qf, 0.0), jnp.where(lo, 0.0, qf)):
            s = _dot_nt(qh.astype(BF16), kt)
            m = jnp.max(s, axis=1, keepdims=True)
            p = jnp.exp(s - m)
            l = jnp.sum(p, axis=1, keepdims=True)
            outs.append(_dot(p.astype(BF16), vt) / l)
        mem_out = jnp.where(lo, outs[0], outs[1])
        gm = tail_ref[:, D_MODEL + t * LANES:D_MODEL + (t + 1) * LANES]
        ym = (mem_out * (gm * jax.nn.sigmoid(gm))).astype(BF16)
        y = y + _dot(ym, wo_ref[D_MODEL + t * LANES:D_MODEL + (t + 1) * LANES, :])
    z = DEEPNORM_ALPHA * h_ref[...] + y
    o_ref[...] = _layer_norm_rows(z, g_ref[...], b_ref[...])


def _epilogue(branch, tail, h, mk, mv, w_out, g, b, tr=512):
    S = h.shape[0]
    row = lambda i: (i, 0)
    fixed = lambda i: (0, 0)
    return pl.pallas_call(
        _epilogue_kernel,
        grid=(S // tr,),
        in_specs=[pl.BlockSpec((tr, D_MODEL), row),
                  pl.BlockSpec((tr, TAIL), row),
                  pl.BlockSpec((tr, D_MODEL), row),
                  pl.BlockSpec((N_MEM, MEM_WIDTH), fixed),
                  pl.BlockSpec((N_MEM, MEM_WIDTH), fixed),
                  pl.BlockSpec((INNER, D_MODEL), fixed),
                  pl.BlockSpec((1, D_MODEL), fixed),
                  pl.BlockSpec((1, D_MODEL), fixed)],
        out_specs=pl.BlockSpec((tr, D_MODEL), row),
        out_shape=jax.ShapeDtypeStruct((S, D_MODEL), F32),
        compiler_params=_cparams("parallel"),
        name="epilogue",
    )(branch, tail, h, mk, mv, w_out, g.reshape(1, -1), b.reshape(1, -1))


def _dup_kv_columns(w):
    d = w.shape[0]
    w4 = w.reshape(d, GQA_KV_HEADS, 1, HEAD_DIM)
    return jnp.broadcast_to(w4, (d, GQA_KV_HEADS, 2, HEAD_DIM)).reshape(d, 2 * GQA_KV_HEADS * HEAD_DIM)


def kernel(x, mem, ln_in_g, ln_in_b, w_mem_kv, w_in_a, w_in_b, q_norm_g, k_norm_g, w_in_c, w_out, ln_g, ln_b):
    B, S, D = x.shape
    assert B == 1 and D == D_MODEL and S % DIL_CHUNK == 0 and S % (FFT_N2 * SUBLANES) == 0

    mkv = _proj(mem[0], w_mem_kv.astype(BF16), name="proj_mem").astype(BF16)
    mk, mv = mkv[:, :MEM_WIDTH], mkv[:, MEM_WIDTH:]
    h = _ln_in(x[0], ln_in_g, ln_in_b)
    fft_tabs = None
    qw = GQA_Q_HEADS * HEAD_DIM
    kvw = GQA_KV_HEADS * HEAD_DIM

    for i in range(DEPTH):
        kind, j = i % 3, i // 3
        if kind == 0:
            if fft_tabs is None:
                fft_tabs = _fft_tables(S)
            fc, gk, hm = fft_tabs
            tail, z = _proj_a(h, w_in_a[j].astype(BF16), fc)
            branch = _fft_stage2(_fft_stage1(z, gk), hm)
        elif kind == 1:
            w = w_in_b[j]
            w_qkv = jnp.concatenate([w[:, :qw], _dup_kv_columns(w[:, qw:qw + kvw]),
                                     w[:, qw + kvw:qw + 2 * kvw]], axis=1).astype(BF16)
            cos_t, sin_t = _rope_tables_axial(S)
            gq = jnp.tile(q_norm_g[j], 2).reshape(1, LANES)
            gkn = jnp.tile(k_norm_g[j], 2).reshape(1, LANES)
            qk, vt = _proj_b(h, w_qkv, cos_t, sin_t, gq, gkn)
            tail = _proj(h, w[:, qw + 2 * kvw:].astype(BF16), name="proj_tail")
            branch = _gqa_attention(qk, vt)
        else:
            w = w_in_c[j].astype(BF16)
            cos_t, sin_t = _rope_tables_1d(S)
            qkvs = [_proj_c(h, w, cos_t, sin_t, g, dil) for g, dil in enumerate(DIL_DILATIONS)]
            tail = _proj(h, w[:, w.shape[1] - TAIL:], name="proj_tail")
            branch = _dilated_attention(qkvs)
        h = _epilogue(branch, tail, h, mk, mv, w_out[i].astype(BF16), ln_g[i], ln_b[i])
    return h[None]
```

```python
import functools

import jax
import jax.numpy as jnp
import numpy as np
from jax import lax
from jax.experimental import pallas as pl
from jax.experimental.pallas import tpu as pltpu

F32 = jnp.float32
BF16 = jnp.bfloat16

D_MODEL = 1024
DEPTH = 4
N_MEM = 256
GRID_W = 64
HEAD_DIM = 64
ROPE_THETA = 10000.0
LN_EPS = 1e-5
RMS_EPS = 1e-6
MASK_VALUE = -1e30
FNET_GROUPS = 4
FNET_GROUP_W = D_MODEL // FNET_GROUPS
GQA_Q_HEADS = 16
GQA_KV_HEADS = 4
DIL_DILATIONS = (1, 4, 16)
DIL_SIDE = 64
MEM_WIDTH = 256
INNER = D_MODEL + MEM_WIDTH
TAIL = INNER + MEM_WIDTH
DEEPNORM_ALPHA = (2.0 * DEPTH) ** 0.25
ATTN_SCALE = HEAD_DIM ** -0.5
LOG2_E = float(np.log2(np.e))
VT_ROWS = HEAD_DIM + 16

LANES = 128
SUBLANES = 8
FFT_N2 = 128
DIL_CHUNK = 1024
VMEM_LIMIT = 48 << 20


def _cparams(*sem):
    return pltpu.CompilerParams(dimension_semantics=sem, vmem_limit_bytes=VMEM_LIMIT)


def _dot(a, b):
    return jnp.dot(a, b, preferred_element_type=F32)


def _dot_nt(a, b):
    return lax.dot_general(a, b, (((1,), (1,)), ((), ())), preferred_element_type=F32)


def _lane_lo(shape=(1, LANES)):
    return lax.broadcasted_iota(jnp.int32, shape, len(shape) - 1) % LANES < HEAD_DIM


def _layer_norm_rows(z, g, b):
    mu = jnp.mean(z, axis=-1, keepdims=True)
    zc = z - mu
    var = jnp.mean(zc * zc, axis=-1, keepdims=True)
    return zc * lax.rsqrt(var + LN_EPS) * g + b


def _ln_kernel(x_ref, g_ref, b_ref, o_ref):
    o_ref[...] = _layer_norm_rows(x_ref[...], g_ref[...], b_ref[...])


def _ln_in(x, g, b, tr=512):
    S = x.shape[0]
    return pl.pallas_call(
        _ln_kernel,
        grid=(S // tr,),
        in_specs=[pl.BlockSpec((tr, D_MODEL), lambda i: (i, 0)),
                  pl.BlockSpec((1, D_MODEL), lambda i: (0, 0)),
                  pl.BlockSpec((1, D_MODEL), lambda i: (0, 0))],
        out_specs=pl.BlockSpec((tr, D_MODEL), lambda i: (i, 0)),
        out_shape=jax.ShapeDtypeStruct((S, D_MODEL), F32),
        compiler_params=_cparams("parallel"),
        name="ln_in",
    )(x, g.reshape(1, -1), b.reshape(1, -1))


def _proj_kernel(x_ref, w_ref, o_ref):
    o_ref[...] = _dot(x_ref[...].astype(BF16), w_ref[...])


def _proj(x, w, tr=512, name="proj"):
    S, K = x.shape
    N = w.shape[1]
    tr = min(tr, S)
    return pl.pallas_call(
        _proj_kernel,
        grid=(S // tr,),
        in_specs=[pl.BlockSpec((tr, K), lambda i: (i, 0)),
                  pl.BlockSpec((K, N), lambda i: (0, 0))],
        out_specs=pl.BlockSpec((tr, N), lambda i: (i, 0)),
        out_shape=jax.ShapeDtypeStruct((S, N), F32),
        compiler_params=_cparams("parallel"),
        name=name,
    )(x, w)


def _proj_a_kernel(x_ref, w_ref, fc_ref, tail_ref, z_ref):
    xb = x_ref[...].astype(BF16)
    tail_ref[...] = _dot(xb, w_ref[...])
    for g in range(FNET_GROUPS):
        cols = slice(g * FNET_GROUP_W, (g + 1) * FNET_GROUP_W)
        zg = _dot(xb[:, cols], fc_ref[...])
        z_ref[0, :, cols] = zg[:, :FNET_GROUP_W]
        z_ref[1, :, cols] = zg[:, FNET_GROUP_W:]


def _proj_a(h, w_tail, fc, tr=512):
    S = h.shape[0]
    return pl.pallas_call(
        _proj_a_kernel,
        grid=(S // tr,),
        in_specs=[pl.BlockSpec((tr, D_MODEL), lambda i: (i, 0)),
                  pl.BlockSpec((D_MODEL, TAIL), lambda i: (0, 0)),
                  pl.BlockSpec((FNET_GROUP_W, 2 * FNET_GROUP_W), lambda i: (0, 0))],
        out_specs=[pl.BlockSpec((tr, TAIL), lambda i: (i, 0)),
                   pl.BlockSpec((2, tr, D_MODEL), lambda i: (0, i, 0))],
        out_shape=[jax.ShapeDtypeStruct((S, TAIL), F32),
                   jax.ShapeDtypeStruct((2, S, D_MODEL), F32)],
        compiler_params=_cparams("parallel"),
        name="proj_a",
    )(h, w_tail, fc)


def _fft1_kernel(z_ref, g_ref, t_ref):
    two, n1, _, sub, tc = z_ref.shape
    x = z_ref[...].reshape(two * n1 * sub, tc).astype(BF16)
    t_ref[...] = _dot(g_ref[...], x).reshape(t_ref.shape)


def _fft_stage1(z, gk, tc=512):
    _, S, D = z.shape
    n1 = S // FFT_N2
    nu = FFT_N2 // SUBLANES
    z5 = z.reshape(2, n1, nu, SUBLANES, D)
    rows = 2 * n1 * SUBLANES
    t5 = pl.pallas_call(
        _fft1_kernel,
        grid=(nu, D // tc),
        in_specs=[pl.BlockSpec((2, n1, 1, SUBLANES, tc), lambda u, c: (0, 0, u, 0, c)),
                  pl.BlockSpec((rows, rows), lambda u, c: (0, 0))],
        out_specs=pl.BlockSpec((n1, 2, 1, SUBLANES, tc), lambda u, c: (0, 0, u, 0, c)),
        out_shape=jax.ShapeDtypeStruct((n1, 2, nu, SUBLANES, D), F32),
        compiler_params=_cparams("parallel", "parallel"),
        name="fft_stage1",
    )(z5, gk)
    return t5.reshape(n1, 2, FFT_N2, D)


def _fft2_kernel(t_ref, h_ref, o_ref, slab_ref):
    nj, _, n2, d = t_ref.shape
    nslab = d // LANES
    for j in range(nj):
        tj = t_ref[j].reshape(2 * n2, d).astype(BF16)
        r = _dot(h_ref[j], tj)
        for s in range(nslab):
            slab_ref[s, pl.ds(j, n2, stride=nj), :] = r[:, s * LANES:(s + 1) * LANES]
    for s in range(nslab):
        o_ref[:, :, s * LANES:(s + 1) * LANES] = slab_ref[s].reshape(n2, nj, LANES)


def _fft_stage2(t, hmat):
    n1, _, n2, D = t.shape
    nj = SUBLANES
    y3 = pl.pallas_call(
        _fft2_kernel,
        grid=(n1 // nj,),
        in_specs=[pl.BlockSpec((nj, 2, n2, D), lambda a: (a, 0, 0, 0)),
                  pl.BlockSpec((nj, n2, 2 * n2), lambda a: (a, 0, 0))],
        out_specs=pl.BlockSpec((n2, nj, D), lambda a: (0, a, 0)),
        out_shape=jax.ShapeDtypeStruct((n2, n1, D), F32),
        scratch_shapes=[pltpu.VMEM((D // LANES, n2 * nj, LANES), F32)],
        compiler_params=_cparams("parallel"),
        name="fft_stage2",
    )(t, hmat)
    return y3.reshape(n2 * n1, D)


def _fft_tables(S):
    n1, n2 = S // FFT_N2, FFT_N2
    c = np.arange(FNET_GROUP_W)
    ang = (2.0 * np.pi / FNET_GROUP_W) * ((c[:, None] * c[None, :]) % FNET_GROUP_W)
    scale = 1.0 / np.sqrt(float(S) * FNET_GROUP_W)
    fc = np.concatenate([np.cos(ang), -np.sin(ang)], axis=1) * scale
    k1 = np.arange(n1)
    th = (2.0 * np.pi / n1) * ((k1[:, None] * k1[None, :]) % n1)
    cs, sn = np.cos(th), np.sin(th)
    g = np.stack([np.stack([cs, sn], axis=1), np.stack([-sn, cs], axis=1)], axis=1)
    gk = np.kron(g.reshape(2 * n1, 2 * n1), np.eye(SUBLANES))
    k2 = np.arange(n2)
    kk = k1[:, None, None] + n1 * k2[None, :, None]
    ph = (2.0 * np.pi / S) * ((k2[None, None, :] * kk) % S)
    hm = np.concatenate([np.cos(ph), np.sin(ph)], axis=2)
    return tuple(jnp.asarray(t, F32).astype(BF16) for t in (fc, gk, hm))


def _rope_angles(pos, dim):
    inv_freq = ROPE_THETA ** (-(np.arange(0, dim, 2, dtype=np.float64) / dim))
    return pos.astype(np.float64)[:, None] * inv_freq[None, :]


def _rope_tables_axial(S):
    t = np.arange(S)
    ar = _rope_angles(t // GRID_W, HEAD_DIM // 2)
    ac = _rope_angles(t % GRID_W, HEAD_DIM // 2)
    cos = np.concatenate([np.cos(ar), np.cos(ar), np.cos(ac), np.cos(ac)], axis=1)
    sin = np.concatenate([-np.sin(ar), np.sin(ar), -np.sin(ac), np.sin(ac)], axis=1)
    return np.tile(cos, (1, 2)), np.tile(sin, (1, 2))


def _rope_tables_1d(S):
    a = _rope_angles(np.arange(S), HEAD_DIM)
    cos = np.concatenate([np.cos(a), np.cos(a)], axis=1)
    sin = np.concatenate([-np.sin(a), np.sin(a)], axis=1)
    return np.tile(cos, (1, 2)), np.tile(sin, (1, 2))


def _rotate_partner(x, half):
    lane = lax.broadcasted_iota(jnp.int32, (1, LANES), 1)
    first = lane % (2 * half) < half
    return jnp.where(first, pltpu.roll(x, LANES - half, 1), pltpu.roll(x, half, 1))


def _proj_b_kernel(x_ref, w_ref, cos_ref, sin_ref, gq_ref, gk_ref, o_ref, vt_ref):
    xb = x_ref[...].astype(BF16)
    cos_t, sin_t = cos_ref[...], sin_ref[...]
    lo = _lane_lo()
    nb_w = 4 * LANES
    vt = _dot(xb, w_ref[:, 3 * nb_w:]).T.astype(BF16)
    ones = jnp.ones((VT_ROWS - HEAD_DIM, vt.shape[1]), BF16)
    for hd in range(GQA_KV_HEADS):
        vt_ref[hd * VT_ROWS:hd * VT_ROWS + HEAD_DIM, :] = vt[hd * HEAD_DIM:(hd + 1) * HEAD_DIM]
        vt_ref[hd * VT_ROWS + HEAD_DIM:(hd + 1) * VT_ROWS, :] = ones
    for nb in range(3):
        r = _dot(xb, w_ref[:, nb * nb_w:(nb + 1) * nb_w])
        gain = gq_ref[...] if nb < 2 else gk_ref[...]
        scale = ATTN_SCALE * LOG2_E if nb < 2 else 1.0
        for t in range(4):
            rt = r[:, t * LANES:(t + 1) * LANES]
            r2 = rt * rt
            tot = jnp.sum(r2, axis=1, keepdims=True)
            low = jnp.sum(jnp.where(lo, r2, 0.0), axis=1, keepdims=True)
            ss = jnp.where(lo, low, tot - low)
            xn = rt * lax.rsqrt(ss * (1.0 / HEAD_DIM) + RMS_EPS) * gain
            out = (xn * cos_t + _rotate_partner(xn, HEAD_DIM // 4) * sin_t) * scale
            c0 = nb * nb_w + t * LANES
            o_ref[:, c0:c0 + LANES] = out.astype(BF16)


def _proj_b(h, w_qkv, cos_t, sin_t, gq, gk, tr=512):
    S = h.shape[0]
    n_in = w_qkv.shape[1]
    n_out = n_in - GQA_KV_HEADS * HEAD_DIM
    vw = GQA_KV_HEADS * VT_ROWS
    return pl.pallas_call(
        _proj_b_kernel,
        grid=(S // tr,),
        in_specs=[pl.BlockSpec((tr, D_MODEL), lambda i: (i, 0)),
                  pl.BlockSpec((D_MODEL, n_in), lambda i: (0, 0)),
                  pl.BlockSpec((tr, LANES), lambda i: (i, 0)),
                  pl.BlockSpec((tr, LANES), lambda i: (i, 0)),
                  pl.BlockSpec((1, LANES), lambda i: (0, 0)),
                  pl.BlockSpec((1, LANES), lambda i: (0, 0))],
        out_specs=[pl.BlockSpec((tr, n_out), lambda i: (i, 0)),
                   pl.BlockSpec((vw, tr), lambda i: (0, i))],
        out_shape=[jax.ShapeDtypeStruct((S, n_out), BF16),
                   jax.ShapeDtypeStruct((vw, S), BF16)],
        compiler_params=_cparams("parallel"),
        name="proj_b",
    )(h, w_qkv, cos_t, sin_t, gq, gk)


def _gqa_kernel(q_ref, k_ref, vt_ref, o_ref, qs_ref, m_ref, acc_ref, s0_ref, s1_ref, *, tk, depth):
    tq = q_ref.shape[0]
    S = k_ref.shape[0]
    nq = 4 * tq
    lo = _lane_lo()
    for t in range(2):
        qt = q_ref[:, t * LANES:(t + 1) * LANES].astype(F32)
        qs_ref[(2 * t) * tq:(2 * t + 1) * tq, :] = jnp.where(lo, qt, 0.0).astype(BF16)
        qs_ref[(2 * t + 1) * tq:(2 * t + 2) * tq, :] = jnp.where(lo, 0.0, qt).astype(BF16)
    m_ref[...] = jnp.full(m_ref.shape, -jnp.inf, F32)
    acc_ref[...] = jnp.zeros(acc_ref.shape, F32)

    nkv = S // tk
    s_refs = (s0_ref, s1_ref)

    def scores(j, s_ref):
        k0 = pl.multiple_of(j * tk, tk)
        s = _dot_nt(k_ref[pl.ds(k0, tk), :], qs_ref[...])
        s_ref[...] = s
        return jnp.max(s.reshape(tk // SUBLANES, SUBLANES, nq), axis=0)

    def accumulate(j, s_ref, cmax):
        k0 = pl.multiple_of(j * tk, tk)
        m_prev = m_ref[...]
        m_new = jnp.maximum(m_prev, jnp.max(cmax, axis=0, keepdims=True))
        p = jnp.exp2(s_ref[...] - jnp.tile(m_new, (tk // SUBLANES, 1)))
        alpha = jnp.exp2(m_prev - m_new)
        pv = _dot(vt_ref[:, pl.ds(k0, tk)], p.astype(BF16))
        acc_ref[...] = jnp.tile(alpha, (VT_ROWS // SUBLANES, 1)) * acc_ref[...] + pv
        m_ref[...] = m_new

    def body(jj, cmax):
        j = depth * jj
        for u in range(depth):
            nxt = j + u + 1
            if u == depth - 1:
                nxt = jnp.minimum(nxt, nkv - 1)
            cmax_next = scores(nxt, s_refs[(u + 1) % 2])
            accumulate(j + u, s_refs[u % 2], cmax)
            cmax = cmax_next
        return cmax

    lax.fori_loop(0, nkv // depth, body, scores(0, s0_ref))
    acc = acc_ref[...]
    ot = acc[:HEAD_DIM] / acc[HEAD_DIM:HEAD_DIM + 1]
    for t in range(2):
        pair = jnp.concatenate([ot[:, (2 * t) * tq:(2 * t + 1) * tq],
                                ot[:, (2 * t + 1) * tq:(2 * t + 2) * tq]], axis=0)
        o_ref[:, t * LANES:(t + 1) * LANES] = pair.T


def _gqa_attention(qk, vt, tq=128, tk=512, depth=4):
    S = qk.shape[0]
    qw = 4 * HEAD_DIM
    k_blk0 = GQA_Q_HEADS * HEAD_DIM // LANES
    tk = min(tk, S)
    assert depth % 2 == 0 and (S // tk) % depth == 0
    return pl.pallas_call(
        functools.partial(_gqa_kernel, tk=tk, depth=depth),
        grid=(GQA_KV_HEADS, S // tq),
        in_specs=[pl.BlockSpec((tq, qw), lambda h, i: (i, h)),
                  pl.BlockSpec((S, LANES), lambda h, i: (0, k_blk0 + h)),
                  pl.BlockSpec((VT_ROWS, S), lambda h, i: (h, 0))],
        out_specs=pl.BlockSpec((tq, qw), lambda h, i: (i, h)),
        out_shape=jax.ShapeDtypeStruct((S, D_MODEL), F32),
        scratch_shapes=[pltpu.VMEM((4 * tq, LANES), BF16),
                        pltpu.VMEM((SUBLANES, 4 * tq), F32),
                        pltpu.VMEM((VT_ROWS, 4 * tq), F32),
                        pltpu.VMEM((tk, 4 * tq), F32),
                        pltpu.VMEM((tk, 4 * tq), F32)],
        compiler_params=_cparams("parallel", "parallel"),
        name="gqa_attention",
    )(qk, qk, vt)


def _proj_c_kernel(x_ref, w_ref, cos_ref, sin_ref, o_ref, xp_ref, slab_ref, *, dil):
    kind = pl.program_id(1)
    tm, d = x_ref.shape[0], o_ref.shape[-1]
    n = tm // dil
    nslab = d // LANES
    cb_w = 2 * LANES

    @pl.when(kind == 0)
    def _():
        if dil == 1:
            xp_ref[...] = x_ref[...].astype(BF16)
            return
        for s in range(nslab):
            slab_ref[s] = x_ref[:, s * LANES:(s + 1) * LANES]
        for c in range(dil):
            for s in range(nslab):
                xp_ref[c * n:(c + 1) * n, s * LANES:(s + 1) * LANES] = (
                    slab_ref[s, pl.ds(c, n, stride=dil), :].astype(BF16))

    def emit(c0, vals):
        o_ref[0, 0, :, :, c0:c0 + vals.shape[1]] = vals.astype(BF16).reshape(dil, n, vals.shape[1])

    @pl.when(kind < 2)
    def _():
        scale = jnp.where(kind == 0, ATTN_SCALE * LOG2_E, 1.0).astype(F32)
        cos_t = cos_ref[...] * scale
        sin_t = sin_ref[...] * scale
        for cb in range(d // cb_w):
            r = _dot(xp_ref[...], w_ref[:, cb * cb_w:(cb + 1) * cb_w])
            for s in range(cb_w // LANES):
                rt = r[:, s * LANES:(s + 1) * LANES]
                emit(cb * cb_w + s * LANES, rt * cos_t + _rotate_partner(rt, HEAD_DIM // 2) * sin_t)

    @pl.when(kind == 2)
    def _():
        for cb in range(d // cb_w):
            emit(cb * cb_w, _dot(xp_ref[...], w_ref[:, cb * cb_w:(cb + 1) * cb_w]))


def _regroup_rows(tab, dil):
    S, w = tab.shape
    return tab.reshape(S // DIL_CHUNK, DIL_CHUNK // dil, dil, w).transpose(0, 2, 1, 3).reshape(S, w)


def _proj_c(h, w, cos_np, sin_np, group, dil):
    S = h.shape[0]
    tm = DIL_CHUNK
    n = tm // dil
    ngroups = len(DIL_DILATIONS)
    cos_t = jnp.asarray(_regroup_rows(cos_np, dil), F32)
    sin_t = jnp.asarray(_regroup_rows(sin_np, dil), F32)
    return pl.pallas_call(
        functools.partial(_proj_c_kernel, dil=dil),
        grid=(S // tm, 3),
        in_specs=[pl.BlockSpec((tm, D_MODEL), lambda i, t: (i, 0)),
                  pl.BlockSpec((D_MODEL, D_MODEL), lambda i, t: (0, ngroups * t + group)),
                  pl.BlockSpec((tm, LANES), lambda i, t: (i, 0)),
                  pl.BlockSpec((tm, LANES), lambda i, t: (i, 0))],
        out_specs=pl.BlockSpec((1, 1, dil, n, D_MODEL), lambda i, t: (t, i, 0, 0, 0)),
        out_shape=jax.ShapeDtypeStruct((3, S // tm, dil, n, D_MODEL), BF16),
        scratch_shapes=[pltpu.VMEM((tm, D_MODEL), BF16),
                        pltpu.VMEM((D_MODEL // LANES, tm, LANES), F32)],
        compiler_params=_cparams("parallel", "arbitrary"),
        name=f"proj_c_dil{dil}",
    )(h, w, cos_t, sin_t)


def _dil_kernel(*refs, nchunks):
    ng = len(DIL_DILATIONS)
    o_ref, osc_ref, lsc_ref = refs[7 * ng:]
    i = pl.program_id(0)
    lo = _lane_lo()
    for g, dil in enumerate(DIL_DILATIONS):
        q_ref, kc_ref, kp_ref, kn_ref, vc_ref, vp_ref, vn_ref = refs[7 * g:7 * g + 7]
        n = DIL_CHUNK // dil
        bq = min(n, 2 * DIL_SIDE)
        nsub = n // bq
        nk = bq + 2 * DIL_SIDE
        qi = lax.broadcasted_iota(jnp.int32, (2 * bq, nk), 0) % bq
        kj = lax.broadcasted_iota(jnp.int32, (2 * bq, nk), 1)
        band = jnp.abs(kj - DIL_SIDE - qi) <= DIL_SIDE

        for c in range(dil):
            kcat = jnp.concatenate([kp_ref[0, 0, c], kc_ref[0, 0, c], kn_ref[0, 0, c]], axis=0)
            vcat = jnp.concatenate([vp_ref[0, 0, c], vc_ref[0, 0, c], vn_ref[0, 0, c]], axis=0)
            for b in range(nsub):
                p0 = b * bq
                kw = kcat[p0:p0 + nk]
                vw = vcat[p0:p0 + nk]
                mask = band
                if b == 0:
                    mask = mask & (kj >= jnp.where(i == 0, DIL_SIDE, 0))
                if b == nsub - 1:
                    mask = mask & (kj < jnp.where(i == nchunks - 1, bq + DIL_SIDE, nk))
                qf = q_ref[0, 0, c, p0:p0 + bq, :].astype(F32)
                qs = jnp.concatenate([jnp.where(lo, qf, 0.0), jnp.where(lo, 0.0, qf)], axis=0)
                s = _dot_nt(qs.astype(BF16), kw)
                s = jnp.where(mask, s, MASK_VALUE)
                m = jnp.max(s, axis=1, keepdims=True)
                p = jnp.exp2(s - m)
                l = jnp.sum(p, axis=1, keepdims=True)
                o2 = _dot(p.astype(BF16), vw) / l
                lse2 = m + jnp.log2(l)
                o_pair = jnp.where(lo, o2[:bq], o2[bq:])
                l_pair = jnp.where(lo, lse2[:bq], lse2[bq:])
                if dil == 1:
                    osc_ref[g, p0:p0 + bq, :] = o_pair
                    lsc_ref[g, p0:p0 + bq, :] = l_pair
                else:
                    rows = pl.ds(p0 * dil + c, bq, stride=dil)
                    osc_ref[g, rows, :] = o_pair
                    lsc_ref[g, rows, :] = l_pair

    ls = [lsc_ref[g] for g in range(ng)]
    mx = functools.reduce(jnp.maximum, ls)
    ws = [jnp.exp2(l - mx) for l in ls]
    num = sum(w * osc_ref[g] for g, w in enumerate(ws))
    o_ref[...] = num / sum(ws)


def _dilated_attention(qkvs):
    nchunks = qkvs[0].shape[1]
    S = nchunks * DIL_CHUNK
    in_specs, args = [], []
    for dil, a in zip(DIL_DILATIONS, qkvs):
        n = DIL_CHUNK // dil
        nblk = n // DIL_SIDE
        full = (1, 1, dil, n, LANES)
        halo = (1, 1, dil, DIL_SIDE, LANES)

        def cur(t):
            return lambda i, m: (t, i, 0, 0, m)

        def prev(t, nblk=nblk):
            return lambda i, m: (t, jnp.maximum(i - 1, 0), 0, nblk - 1, m)

        def nxt(t):
            return lambda i, m: (t, jnp.minimum(i + 1, nchunks - 1), 0, 0, m)

        in_specs += [pl.BlockSpec(full, cur(0)),
                     pl.BlockSpec(full, cur(1)), pl.BlockSpec(halo, prev(1)), pl.BlockSpec(halo, nxt(1)),
                     pl.BlockSpec(full, cur(2)), pl.BlockSpec(halo, prev(2)), pl.BlockSpec(halo, nxt(2))]
        args += [a] * 7
    ng = len(DIL_DILATIONS)
    return pl.pallas_call(
        functools.partial(_dil_kernel, nchunks=nchunks),
        grid=(nchunks, D_MODEL // LANES),
        in_specs=in_specs,
        out_specs=pl.BlockSpec((DIL_CHUNK, LANES), lambda i, m: (i, m)),
        out_shape=jax.ShapeDtypeStruct((S, D_MODEL), F32),
        scratch_shapes=[pltpu.VMEM((ng, DIL_CHUNK, LANES), F32),
                        pltpu.VMEM((ng, DIL_CHUNK, LANES), F32)],
        compiler_params=_cparams("parallel", "parallel"),
        name="dilated_attention",
    )(*args)


def _epilogue_kernel(br_ref, tail_ref, h_ref, mk_ref, mv_ref, wo_ref, g_ref, b_ref, o_ref):
    lo = _lane_lo()
    gate_b = tail_ref[:, :D_MODEL]
    yb = (br_ref[...] * (gate_b * jax.nn.sigmoid(gate_b))).astype(BF16)
    y = _dot(yb, wo_ref[:D_MODEL, :])
    for t in range(MEM_WIDTH // LANES):
        cols = slice(t * LANES, (t + 1) * LANES)
        qf = tail_ref[:, INNER + t * LANES:INNER + (t + 1) * LANES] * ATTN_SCALE
        kt, vt = mk_ref[:, cols], mv_ref[:, cols]
        outs = []
        for qh in (jnp.where(lo, qf, 0.0), jnp.where(lo, 0.0, qf)):
            s = _dot_nt(qh.astype(BF16), kt)
            m = jnp.max(s, axis=1, keepdims=True)
            p = jnp.exp(s - m)
            l = jnp.sum(p, axis=1, keepdims=True)
            outs.append(_dot(p.astype(BF16), vt) / l)
        mem_out = jnp.where(lo, outs[0], outs[1])
        gm = tail_ref[:, D_MODEL + t * LANES:D_MODEL + (t + 1) * LANES]
        ym = (mem_out * (gm * jax.nn.sigmoid(gm))).astype(BF16)
        y = y + _dot(ym, wo_ref[D_MODEL + t * LANES:D_MODEL + (t + 1) * LANES, :])
    z = DEEPNORM_ALPHA * h_ref[...] + y
    o_ref[...] = _layer_norm_rows(z, g_ref[...], b_ref[...])


def _epilogue(branch, tail, h, mk, mv, w_out, g, b, tr=512):
    S = h.shape[0]
    row = lambda i: (i, 0)
    fixed = lambda i: (0, 0)
    return pl.pallas_call(
        _epilogue_kernel,
        grid=(S // tr,),
        in_specs=[pl.BlockSpec((tr, D_MODEL), row),
                  pl.BlockSpec((tr, TAIL), row),
                  pl.BlockSpec((tr, D_MODEL), row),
                  pl.BlockSpec((N_MEM, MEM_WIDTH), fixed),
                  pl.BlockSpec((N_MEM, MEM_WIDTH), fixed),
                  pl.BlockSpec((INNER, D_MODEL), fixed),
                  pl.BlockSpec((1, D_MODEL), fixed),
                  pl.BlockSpec((1, D_MODEL), fixed)],
        out_specs=pl.BlockSpec((tr, D_MODEL), row),
        out_shape=jax.ShapeDtypeStruct((S, D_MODEL), F32),
        compiler_params=_cparams("parallel"),
        name="epilogue",
    )(branch, tail, h, mk, mv, w_out, g.reshape(1, -1), b.reshape(1, -1))


def _dup_kv_columns(w):
    d = w.shape[0]
    w4 = w.reshape(d, GQA_KV_HEADS, 1, HEAD_DIM)
    return jnp.broadcast_to(w4, (d, GQA_KV_HEADS, 2, HEAD_DIM)).reshape(d, 2 * GQA_KV_HEADS * HEAD_DIM)


def kernel(x, mem, ln_in_g, ln_in_b, w_mem_kv, w_in_a, w_in_b, q_norm_g, k_norm_g, w_in_c, w_out, ln_g, ln_b):
    B, S, D = x.shape
    assert B == 1 and D == D_MODEL and S % DIL_CHUNK == 0 and S % (FFT_N2 * SUBLANES) == 0

    mkv = _proj(mem[0], w_mem_kv.astype(BF16), name="proj_mem").astype(BF16)
    mk, mv = mkv[:, :MEM_WIDTH], mkv[:, MEM_WIDTH:]
    h = _ln_in(x[0], ln_in_g, ln_in_b)
    fft_tabs = None
    qw = GQA_Q_HEADS * HEAD_DIM
    kvw = GQA_KV_HEADS * HEAD_DIM

    for i in range(DEPTH):
        kind, j = i % 3, i // 3
        if kind == 0:
            if fft_tabs is None:
                fft_tabs = _fft_tables(S)
            fc, gk, hm = fft_tabs
            tail, z = _proj_a(h, w_in_a[j].astype(BF16), fc)
            branch = _fft_stage2(_fft_stage1(z, gk), hm)
        elif kind == 1:
            w = w_in_b[j]
            w_qkv = jnp.concatenate([w[:, :qw], _dup_kv_columns(w[:, qw:qw + kvw]),
                                     w[:, qw + kvw:qw + 2 * kvw]], axis=1).astype(BF16)
            cos_t, sin_t = (jnp.asarray(t, F32) for t in _rope_tables_axial(S))
            gq = jnp.tile(q_norm_g[j], 2).reshape(1, LANES)
            gkn = jnp.tile(k_norm_g[j], 2).reshape(1, LANES)
            qk, vt = _proj_b(h, w_qkv, cos_t, sin_t, gq, gkn)
            tail = _proj(h, w[:, qw + 2 * kvw:].astype(BF16), name="proj_tail")
            branch = _gqa_attention(qk, vt)
        else:
            w = w_in_c[j].astype(BF16)
            cos_t, sin_t = _rope_tables_1d(S)
            qkvs = [_proj_c(h, w, cos_t, sin_t, g, dil) for g, dil in enumerate(DIL_DILATIONS)]
            tail = _proj(h, w[:, w.shape[1] - TAIL:], name="proj_tail")
            branch = _dilated_attention(qkvs)
        h = _epilogue(branch, tail, h, mk, mv, w_out[i].astype(BF16), ln_g[i], ln_b[i])
    return h[None]
```

```python
import functools

import jax
import jax.numpy as jnp
import numpy as np
from jax import lax
from jax.experimental import pallas as pl
from jax.experimental.pallas import tpu as pltpu

F32 = jnp.float32
BF16 = jnp.bfloat16

D_MODEL = 1024
DEPTH = 4
N_MEM = 256
GRID_W = 64
HEAD_DIM = 64
ROPE_THETA = 10000.0
LN_EPS = 1e-5
RMS_EPS = 1e-6
MASK_VALUE = -1e30
FNET_GROUPS = 4
FNET_GROUP_W = D_MODEL // FNET_GROUPS
GQA_Q_HEADS = 16
GQA_KV_HEADS = 4
DIL_DILATIONS = (1, 4, 16)
DIL_SIDE = 64
MEM_WIDTH = 256
INNER = D_MODEL + MEM_WIDTH
TAIL = INNER + MEM_WIDTH
DEEPNORM_ALPHA = (2.0 * DEPTH) ** 0.25
ATTN_SCALE = HEAD_DIM ** -0.5
LOG2_E = float(np.log2(np.e))
VT_ROWS = HEAD_DIM + 16
LANES = 128
SUBLANES = 8
FFT_N2 = 128
DIL_CHUNK = 1024
VMEM_LIMIT = 48 << 20


def _cparams(*sem):
    return pltpu.CompilerParams(dimension_semantics=sem, vmem_limit_bytes=VMEM_LIMIT)


def _dot(a, b):
    return jnp.dot(a, b, preferred_element_type=F32)


def _dot_nt(a, b):
    return lax.dot_general(a, b, (((1,), (1,)), ((), ())), preferred_element_type=F32)


def _lane_lo(shape=(1, LANES)):
    return lax.broadcasted_iota(jnp.int32, shape, len(shape) - 1) % LANES < HEAD_DIM


def _layer_norm_rows(z, g, b):
    mu = jnp.mean(z, axis=-1, keepdims=True)
    zc = z - mu
    var = jnp.mean(zc * zc, axis=-1, keepdims=True)
    return zc * lax.rsqrt(var + LN_EPS) * g + b


def _ln_kernel(x_ref, g_ref, b_ref, o_ref):
    o_ref[...] = _layer_norm_rows(x_ref[...], g_ref[...], b_ref[...])


def _ln_in(x, g, b, tr=512):
    S = x.shape[0]
    return pl.pallas_call(
        _ln_kernel,
        grid=(S // tr,),
        in_specs=[pl.BlockSpec((tr, D_MODEL), lambda i: (i, 0)),
                  pl.BlockSpec((1, D_MODEL), lambda i: (0, 0)),
                  pl.BlockSpec((1, D_MODEL), lambda i: (0, 0))],
        out_specs=pl.BlockSpec((tr, D_MODEL), lambda i: (i, 0)),
        out_shape=jax.ShapeDtypeStruct((S, D_MODEL), F32),
        compiler_params=_cparams("parallel"),
        name="ln_in",
    )(x, g.reshape(1, -1), b.reshape(1, -1))


def _proj_kernel(x_ref, w_ref, o_ref):
    o_ref[...] = _dot(x_ref[...].astype(BF16), w_ref[...])


def _proj(x, w, tr=512, name="proj"):
    S, K = x.shape
    N = w.shape[1]
    tr = min(tr, S)
    return pl.pallas_call(
        _proj_kernel,
        grid=(S // tr,),
        in_specs=[pl.BlockSpec((tr, K), lambda i: (i, 0)),
                  pl.BlockSpec((K, N), lambda i: (0, 0))],
        out_specs=pl.BlockSpec((tr, N), lambda i: (i, 0)),
        out_shape=jax.ShapeDtypeStruct((S, N), F32),
        compiler_params=_cparams("parallel"),
        name=name,
    )(x, w)


def _proj_a_kernel(x_ref, w_ref, fc_ref, tail_ref, z_ref):
    xb = x_ref[...].astype(BF16)
    tail_ref[...] = _dot(xb, w_ref[...])
    for g in range(FNET_GROUPS):
        cols = slice(g * FNET_GROUP_W, (g + 1) * FNET_GROUP_W)
        zg = _dot(xb[:, cols], fc_ref[...])
        z_ref[0, :, cols] = zg[:, :FNET_GROUP_W]
        z_ref[1, :, cols] = zg[:, FNET_GROUP_W:]


def _proj_a(h, w_tail, fc, tr=512):
    S = h.shape[0]
    return pl.pallas_call(
        _proj_a_kernel,
        grid=(S // tr,),
        in_specs=[pl.BlockSpec((tr, D_MODEL), lambda i: (i, 0)),
                  pl.BlockSpec((D_MODEL, TAIL), lambda i: (0, 0)),
                  pl.BlockSpec((FNET_GROUP_W, 2 * FNET_GROUP_W), lambda i: (0, 0))],
        out_specs=[pl.BlockSpec((tr, TAIL), lambda i: (i, 0)),
                   pl.BlockSpec((2, tr, D_MODEL), lambda i: (0, i, 0))],
        out_shape=[jax.ShapeDtypeStruct((S, TAIL), F32),
                   jax.ShapeDtypeStruct((2, S, D_MODEL), F32)],
        compiler_params=_cparams("parallel"),
        name="proj_a",
    )(h, w_tail, fc)


def _fft1_kernel(z_ref, g_ref, t_ref):
    two, n1, _, sub, tc = z_ref.shape
    x = z_ref[...].reshape(two * n1 * sub, tc).astype(BF16)
    t_ref[...] = _dot(g_ref[...], x).reshape(t_ref.shape)


def _fft_stage1(z, gk, tc=512):
    _, S, D = z.shape
    n1 = S // FFT_N2
    nu = FFT_N2 // SUBLANES
    z5 = z.reshape(2, n1, nu, SUBLANES, D)
    rows = 2 * n1 * SUBLANES
    t5 = pl.pallas_call(
        _fft1_kernel,
        grid=(nu, D // tc),
        in_specs=[pl.BlockSpec((2, n1, 1, SUBLANES, tc), lambda u, c: (0, 0, u, 0, c)),
                  pl.BlockSpec((rows, rows), lambda u, c: (0, 0))],
        out_specs=pl.BlockSpec((n1, 2, 1, SUBLANES, tc), lambda u, c: (0, 0, u, 0, c)),
        out_shape=jax.ShapeDtypeStruct((n1, 2, nu, SUBLANES, D), F32),
        compiler_params=_cparams("parallel", "parallel"),
        name="fft_stage1",
    )(z5, gk)
    return t5.reshape(n1, 2, FFT_N2, D)


def _fft2_kernel(t_ref, h_ref, o_ref, slab_ref):
    nj, _, n2, d = t_ref.shape
    nslab = d // LANES
    for j in range(nj):
        tj = t_ref[j].reshape(2 * n2, d).astype(BF16)
        r = _dot(h_ref[j], tj)
        for s in range(nslab):
            slab_ref[s, pl.ds(j, n2, stride=nj), :] = r[:, s * LANES:(s + 1) * LANES]
    for s in range(nslab):
        o_ref[:, :, s * LANES:(s + 1) * LANES] = slab_ref[s].reshape(n2, nj, LANES)


def _fft_stage2(t, hmat):
    n1, _, n2, D = t.shape
    nj = SUBLANES
    y3 = pl.pallas_call(
        _fft2_kernel,
        grid=(n1 // nj,),
        in_specs=[pl.BlockSpec((nj, 2, n2, D), lambda a: (a, 0, 0, 0)),
                  pl.BlockSpec((nj, n2, 2 * n2), lambda a: (a, 0, 0))],
        out_specs=pl.BlockSpec((n2, nj, D), lambda a: (0, a, 0)),
        out_shape=jax.ShapeDtypeStruct((n2, n1, D), F32),
        scratch_shapes=[pltpu.VMEM((D // LANES, n2 * nj, LANES), F32)],
        compiler_params=_cparams("parallel"),
        name="fft_stage2",
    )(t, hmat)
    return y3.reshape(n2 * n1, D)


def _fft_tables(S):
    n1, n2 = S // FFT_N2, FFT_N2
    c = np.arange(FNET_GROUP_W)
    ang = (2.0 * np.pi / FNET_GROUP_W) * ((c[:, None] * c[None, :]) % FNET_GROUP_W)
    scale = 1.0 / np.sqrt(float(S) * FNET_GROUP_W)
    fc = np.concatenate([np.cos(ang), -np.sin(ang)], axis=1) * scale
    k1 = np.arange(n1)
    th = (2.0 * np.pi / n1) * ((k1[:, None] * k1[None, :]) % n1)
    cs, sn = np.cos(th), np.sin(th)
    g = np.stack([np.stack([cs, sn], axis=1), np.stack([-sn, cs], axis=1)], axis=1)
    gk = np.kron(g.reshape(2 * n1, 2 * n1), np.eye(SUBLANES))
    k2 = np.arange(n2)
    kk = k1[:, None, None] + n1 * k2[None, :, None]
    ph = (2.0 * np.pi / S) * ((k2[None, None, :] * kk) % S)
    hm = np.concatenate([np.cos(ph), np.sin(ph)], axis=2)
    return tuple(jnp.asarray(t, F32).astype(BF16) for t in (fc, gk, hm))


def _rope_angles(pos, dim):
    inv_freq = ROPE_THETA ** (-(np.arange(0, dim, 2, dtype=np.float64) / dim))
    return pos.astype(np.float64)[:, None] * inv_freq[None, :]


def _rope_tables_axial(S):
    t = np.arange(S)
    ar = _rope_angles(t // GRID_W, HEAD_DIM // 2)
    ac = _rope_angles(t % GRID_W, HEAD_DIM // 2)
    cos = np.concatenate([np.cos(ar), np.cos(ar), np.cos(ac), np.cos(ac)], axis=1)
    sin = np.concatenate([-np.sin(ar), np.sin(ar), -np.sin(ac), np.sin(ac)], axis=1)
    return np.tile(cos, (1, 2)), np.tile(sin, (1, 2))


def _rope_tables_1d(S):
    a = _rope_angles(np.arange(S), HEAD_DIM)
    cos = np.concatenate([np.cos(a), np.cos(a)], axis=1)
    sin = np.concatenate([-np.sin(a), np.sin(a)], axis=1)
    return np.tile(cos, (1, 2)), np.tile(sin, (1, 2))


def _rotate_partner(x, half):
    lane = lax.broadcasted_iota(jnp.int32, (1, LANES), 1)
    first = lane % (2 * half) < half
    return jnp.where(first, pltpu.roll(x, LANES - half, 1), pltpu.roll(x, half, 1))


def _proj_b_kernel(x_ref, w_ref, cos_ref, sin_ref, gq_ref, gk_ref, o_ref, vt_ref):
    xb = x_ref[...].astype(BF16)
    cos_t, sin_t = cos_ref[...], sin_ref[...]
    lo = _lane_lo()
    nb_w = 4 * LANES
    vt = _dot(xb, w_ref[:, 3 * nb_w:]).T.astype(BF16)
    ones = jnp.ones((VT_ROWS - HEAD_DIM, vt.shape[1]), BF16)
    for hd in range(GQA_KV_HEADS):
        vt_ref[hd * VT_ROWS:hd * VT_ROWS + HEAD_DIM, :] = vt[hd * HEAD_DIM:(hd + 1) * HEAD_DIM]
        vt_ref[hd * VT_ROWS + HEAD_DIM:(hd + 1) * VT_ROWS, :] = ones
    for nb in range(3):
        r = _dot(xb, w_ref[:, nb * nb_w:(nb + 1) * nb_w])
        gain = gq_ref[...] if nb < 2 else gk_ref[...]
        scale = ATTN_SCALE * LOG2_E if nb < 2 else 1.0
        for t in range(4):
            rt = r[:, t * LANES:(t + 1) * LANES]
            r2 = rt * rt
            tot = jnp.sum(r2, axis=1, keepdims=True)
            low = jnp.sum(jnp.where(lo, r2, 0.0), axis=1, keepdims=True)
            ss = jnp.where(lo, low, tot - low)
            xn = rt * lax.rsqrt(ss * (1.0 / HEAD_DIM) + RMS_EPS) * gain
            out = (xn * cos_t + _rotate_partner(xn, HEAD_DIM // 4) * sin_t) * scale
            c0 = nb * nb_w + t * LANES
            o_ref[:, c0:c0 + LANES] = out.astype(BF16)


def _proj_b(h, w_qkv, cos_t, sin_t, gq, gk, tr=512):
    S = h.shape[0]
    n_in = w_qkv.shape[1]
    n_out = n_in - GQA_KV_HEADS * HEAD_DIM
    vw = GQA_KV_HEADS * VT_ROWS
    return pl.pallas_call(
        _proj_b_kernel,
        grid=(S // tr,),
        in_specs=[pl.BlockSpec((tr, D_MODEL), lambda i: (i, 0)),
                  pl.BlockSpec((D_MODEL, n_in), lambda i: (0, 0)),
                  pl.BlockSpec((tr, LANES), lambda i: (i, 0)),
                  pl.BlockSpec((tr, LANES), lambda i: (i, 0)),
                  pl.BlockSpec((1, LANES), lambda i: (0, 0)),
                  pl.BlockSpec((1, LANES), lambda i: (0, 0))],
        out_specs=[pl.BlockSpec((tr, n_out), lambda i: (i, 0)),
                   pl.BlockSpec((vw, tr), lambda i: (0, i))],
        out_shape=[jax.ShapeDtypeStruct((S, n_out), BF16),
                   jax.ShapeDtypeStruct((vw, S), BF16)],
        compiler_params=_cparams("parallel"),
        name="proj_b",
    )(h, w_qkv, cos_t, sin_t, gq, gk)


def _gqa_kernel(q_ref, k_ref, vt_ref, o_ref, qs_ref, m_ref, acc_ref, *s_refs, tk, depth):
    tq = q_ref.shape[0]
    S = k_ref.shape[0]
    nq = 4 * tq
    lo = _lane_lo()
    for t in range(2):
        qt = q_ref[:, t * LANES:(t + 1) * LANES].astype(F32)
        qs_ref[(2 * t) * tq:(2 * t + 1) * tq, :] = jnp.where(lo, qt, 0.0).astype(BF16)
        qs_ref[(2 * t + 1) * tq:(2 * t + 2) * tq, :] = jnp.where(lo, 0.0, qt).astype(BF16)
    m_ref[...] = jnp.full(m_ref.shape, -jnp.inf, F32)
    acc_ref[...] = jnp.zeros(acc_ref.shape, F32)

    nkv = S // tk
    nbuf = len(s_refs)

    def scores(j, s_ref):
        k0 = pl.multiple_of(j * tk, tk)
        s = _dot_nt(k_ref[pl.ds(k0, tk), :], qs_ref[...])
        s_ref[...] = s
        return jnp.max(s.reshape(tk // SUBLANES, SUBLANES, nq), axis=0)

    def accumulate(j, s_ref, cmax):
        k0 = pl.multiple_of(j * tk, tk)
        m_prev = m_ref[...]
        m_new = jnp.maximum(m_prev, jnp.max(cmax, axis=0, keepdims=True))
        p = jnp.exp2(s_ref[...] - jnp.tile(m_new, (tk // SUBLANES, 1)))
        alpha = jnp.exp2(m_prev - m_new)
        pv = _dot(vt_ref[:, pl.ds(k0, tk)], p.astype(BF16))
        acc_ref[...] = jnp.tile(alpha, (VT_ROWS // SUBLANES, 1)) * acc_ref[...] + pv
        m_ref[...] = m_new

    def body(jj, cmax):
        j = depth * jj
        for u in range(depth):
            nxt = j + u + 1
            if u == depth - 1:
                nxt = jnp.minimum(nxt, nkv - 1)
            cmax_next = scores(nxt, s_refs[(u + 1) % nbuf])
            accumulate(j + u, s_refs[u % nbuf], cmax)
            cmax = cmax_next
        return cmax

    lax.fori_loop(0, nkv // depth, body, scores(0, s_refs[0]))
    acc = acc_ref[...]
    ot = acc[:HEAD_DIM] / acc[HEAD_DIM:HEAD_DIM + 1]
    for t in range(2):
        pair = jnp.concatenate([ot[:, (2 * t) * tq:(2 * t + 1) * tq],
                                ot[:, (2 * t + 1) * tq:(2 * t + 2) * tq]], axis=0)
        o_ref[:, t * LANES:(t + 1) * LANES] = pair.T


def _gqa_attention(qk, vt, tq=128, tk=512, depth=8, nbuf=2):
    S = qk.shape[0]
    qw = 4 * HEAD_DIM
    k_blk0 = GQA_Q_HEADS * HEAD_DIM // LANES
    tk = min(tk, S)
    depth = min(depth, S // tk)
    assert depth % nbuf == 0 and (S // tk) % depth == 0
    return pl.pallas_call(
        functools.partial(_gqa_kernel, tk=tk, depth=depth),
        grid=(GQA_KV_HEADS, S // tq),
        in_specs=[pl.BlockSpec((tq, qw), lambda h, i: (i, h)),
                  pl.BlockSpec((S, LANES), lambda h, i: (0, k_blk0 + h)),
                  pl.BlockSpec((VT_ROWS, S), lambda h, i: (h, 0))],
        out_specs=pl.BlockSpec((tq, qw), lambda h, i: (i, h)),
        out_shape=jax.ShapeDtypeStruct((S, D_MODEL), F32),
        scratch_shapes=[pltpu.VMEM((4 * tq, LANES), BF16),
                        pltpu.VMEM((SUBLANES, 4 * tq), F32),
                        pltpu.VMEM((VT_ROWS, 4 * tq), F32)]
        + [pltpu.VMEM((tk, 4 * tq), F32)] * nbuf,
        compiler_params=_cparams("parallel", "parallel"),
        name="gqa_attention",
    )(qk, qk, vt)


def _proj_c_kernel(x_ref, w_ref, cos_ref, sin_ref, o_ref, xp_ref, slab_ref, *, dil):
    kind = pl.program_id(1)
    tm, d = x_ref.shape[0], o_ref.shape[-1]
    n = tm // dil
    nslab = d // LANES
    cb_w = 2 * LANES

    @pl.when(kind == 0)
    def _():
        if dil == 1:
            xp_ref[...] = x_ref[...].astype(BF16)
            return
        for s in range(nslab):
            slab_ref[s] = x_ref[:, s * LANES:(s + 1) * LANES]
        for c in range(dil):
            for s in range(nslab):
                xp_ref[c * n:(c + 1) * n, s * LANES:(s + 1) * LANES] = (
                    slab_ref[s, pl.ds(c, n, stride=dil), :].astype(BF16))

    def emit(c0, vals):
        o_ref[0, 0, :, :, c0:c0 + vals.shape[1]] = vals.astype(BF16).reshape(dil, n, vals.shape[1])

    @pl.when(kind < 2)
    def _():
        scale = jnp.where(kind == 0, ATTN_SCALE * LOG2_E, 1.0).astype(F32)
        cos_t = cos_ref[...] * scale
        sin_t = sin_ref[...] * scale
        for cb in range(d // cb_w):
            r = _dot(xp_ref[...], w_ref[:, cb * cb_w:(cb + 1) * cb_w])
            for s in range(cb_w // LANES):
                rt = r[:, s * LANES:(s + 1) * LANES]
                emit(cb * cb_w + s * LANES, rt * cos_t + _rotate_partner(rt, HEAD_DIM // 2) * sin_t)

    @pl.when(kind == 2)
    def _():
        for cb in range(d // cb_w):
            emit(cb * cb_w, _dot(xp_ref[...], w_ref[:, cb * cb_w:(cb + 1) * cb_w]))


def _regroup_rows(tab, dil):
    S, w = tab.shape
    return tab.reshape(S // DIL_CHUNK, DIL_CHUNK // dil, dil, w).transpose(0, 2, 1, 3).reshape(S, w)


def _proj_c(h, w, cos_np, sin_np, group, dil):
    S = h.shape[0]
    tm = DIL_CHUNK
    n = tm // dil
    ngroups = len(DIL_DILATIONS)
    cos_t = jnp.asarray(_regroup_rows(cos_np, dil), F32)
    sin_t = jnp.asarray(_regroup_rows(sin_np, dil), F32)
    return pl.pallas_call(
        functools.partial(_proj_c_kernel, dil=dil),
        grid=(S // tm, 3),
        in_specs=[pl.BlockSpec((tm, D_MODEL), lambda i, t: (i, 0)),
                  pl.BlockSpec((D_MODEL, D_MODEL), lambda i, t: (0, ngroups * t + group)),
                  pl.BlockSpec((tm, LANES), lambda i, t: (i, 0)),
                  pl.BlockSpec((tm, LANES), lambda i, t: (i, 0))],
        out_specs=pl.BlockSpec((1, 1, dil, n, D_MODEL), lambda i, t: (t, i, 0, 0, 0)),
        out_shape=jax.ShapeDtypeStruct((3, S // tm, dil, n, D_MODEL), BF16),
        scratch_shapes=[pltpu.VMEM((tm, D_MODEL), BF16),
                        pltpu.VMEM((D_MODEL // LANES, tm, LANES), F32)],
        compiler_params=_cparams("parallel", "arbitrary"),
        name=f"proj_c_dil{dil}",
    )(h, w, cos_t, sin_t)


def _dil_kernel(*refs, nchunks):
    ng = len(DIL_DILATIONS)
    o_ref, osc_ref, lsc_ref = refs[7 * ng:]
    i = pl.program_id(0)
    lo = _lane_lo()
    for g, dil in enumerate(DIL_DILATIONS):
        q_ref, kc_ref, kp_ref, kn_ref, vc_ref, vp_ref, vn_ref = refs[7 * g:7 * g + 7]
        n = DIL_CHUNK // dil
        bq = min(n, 2 * DIL_SIDE)
        nsub = n // bq
        nk = bq + 2 * DIL_SIDE
        kpad = -(n + 2 * DIL_SIDE) % LANES
        qi = lax.broadcasted_iota(jnp.int32, (2 * bq, nk), 0) % bq
        kj = lax.broadcasted_iota(jnp.int32, (2 * bq, nk), 1)
        band = jnp.where(jnp.abs(kj - DIL_SIDE - qi) <= DIL_SIDE, 0.0, MASK_VALUE).astype(F32)

        for c in range(dil):
            kparts = [kp_ref[0, 0, c], kc_ref[0, 0, c], kn_ref[0, 0, c]]
            if kpad:
                kparts.append(jnp.zeros((kpad, LANES), BF16))
            kcat_t = jnp.concatenate(kparts, axis=0).astype(F32).T.astype(BF16)
            vcat = jnp.concatenate([vp_ref[0, 0, c], vc_ref[0, 0, c], vn_ref[0, 0, c]], axis=0)
            for b in range(nsub):
                p0 = b * bq
                kw_t = kcat_t[:, p0:p0 + nk]
                vw = vcat[p0:p0 + nk]
                bias = band
                if b == 0:
                    bias = jnp.where(kj >= jnp.where(i == 0, DIL_SIDE, 0), bias, MASK_VALUE)
                if b == nsub - 1:
                    bias = jnp.where(kj < jnp.where(i == nchunks - 1, bq + DIL_SIDE, nk), bias, MASK_VALUE)
                qf = q_ref[0, 0, c, p0:p0 + bq, :].astype(F32)
                qs = jnp.concatenate([jnp.where(lo, qf, 0.0), jnp.where(lo, 0.0, qf)], axis=0)
                s = _dot(qs.astype(BF16), kw_t) + bias
                m = jnp.max(s, axis=1, keepdims=True)
                p = jnp.exp2(s - m)
                l = jnp.sum(p, axis=1, keepdims=True)
                o2 = _dot(p.astype(BF16), vw) / l
                lse2 = m + jnp.log2(l)
                o_pair = jnp.where(lo, o2[:bq], o2[bq:])
                l_pair = jnp.where(lo, lse2[:bq], lse2[bq:])
                if dil == 1:
                    osc_ref[g, p0:p0 + bq, :] = o_pair
                    lsc_ref[g, p0:p0 + bq, :] = l_pair
                else:
                    rows = pl.ds(p0 * dil + c, bq, stride=dil)
                    osc_ref[g, rows, :] = o_pair
                    lsc_ref[g, rows, :] = l_pair

    ls = [lsc_ref[g] for g in range(ng)]
    mx = functools.reduce(jnp.maximum, ls)
    ws = [jnp.exp2(l - mx) for l in ls]
    num = sum(w * osc_ref[g] for g, w in enumerate(ws))
    o_ref[...] = num / sum(ws)


def _dilated_attention(qkvs):
    nchunks = qkvs[0].shape[1]
    S = nchunks * DIL_CHUNK
    in_specs, args = [], []
    for dil, a in zip(DIL_DILATIONS, qkvs):
        n = DIL_CHUNK // dil
        nblk = n // DIL_SIDE
        full = (1, 1, dil, n, LANES)
        halo = (1, 1, dil, DIL_SIDE, LANES)

        def cur(t):
            return lambda i, m: (t, i, 0, 0, m)

        def prev(t, nblk=nblk):
            return lambda i, m: (t, jnp.maximum(i - 1, 0), 0, nblk - 1, m)

        def nxt(t):
            return lambda i, m: (t, jnp.minimum(i + 1, nchunks - 1), 0, 0, m)

        in_specs += [pl.BlockSpec(full, cur(0)),
                     pl.BlockSpec(full, cur(1)), pl.BlockSpec(halo, prev(1)), pl.BlockSpec(halo, nxt(1)),
                     pl.BlockSpec(full, cur(2)), pl.BlockSpec(halo, prev(2)), pl.BlockSpec(halo, nxt(2))]
        args += [a] * 7
    ng = len(DIL_DILATIONS)
    return pl.pallas_call(
        functools.partial(_dil_kernel, nchunks=nchunks),
        grid=(nchunks, D_MODEL // LANES),
        in_specs=in_specs,
        out_specs=pl.BlockSpec((DIL_CHUNK, LANES), lambda i, m: (i, m)),
        out_shape=jax.ShapeDtypeStruct((S, D_MODEL), F32),
        scratch_shapes=[pltpu.VMEM((ng, DIL_CHUNK, LANES), F32),
                        pltpu.VMEM((ng, DIL_CHUNK, LANES), F32)],
        compiler_params=_cparams("parallel", "parallel"),
        name="dilated_attention",
    )(*args)


def _epilogue_kernel(br_ref, tail_ref, h_ref, mk_ref, mv_ref, wo_ref, g_ref, b_ref, o_ref):
    lo = _lane_lo()
    gate_b = tail_ref[:, :D_MODEL]
    yb = (br_ref[...] * (gate_b * jax.nn.sigmoid(gate_b))).astype(BF16)
    y = _dot(yb, wo_ref[:D_MODEL, :])
    for t in range(MEM_WIDTH // LANES):
        cols = slice(t * LANES, (t + 1) * LANES)
        qf = tail_ref[:, INNER + t * LANES:INNER + (t + 1) * LANES] * ATTN_SCALE
        kt, vt = mk_ref[:, cols], mv_ref[:, cols]
        outs = []
        for qh in (jnp.where(lo, qf, 0.0), jnp.where(lo, 0.0, qf)):
            s = _dot_nt(qh.astype(BF16), kt)
            m = jnp.max(s, axis=1, keepdims=True)
            p = jnp.exp(s - m)
            l = jnp.sum(p, axis=1, keepdims=True)
            outs.append(_dot(p.astype(BF16), vt) / l)
        mem_out = jnp.where(lo, outs[0], outs[1])
        gm = tail_ref[:, D_MODEL + t * LANES:D_MODEL + (t + 1) * LANES]
        ym = (mem_out * (gm * jax.nn.sigmoid(gm))).astype(BF16)
        y = y + _dot(ym, wo_ref[D_MODEL + t * LANES:D_MODEL + (t + 1) * LANES, :])
    z = DEEPNORM_ALPHA * h_ref[...] + y
    o_ref[...] = _layer_norm_rows(z, g_ref[...], b_ref[...])


def _epilogue(branch, tail, h, mk, mv, w_out, g, b, tr=512):
    S = h.shape[0]
    row = lambda i: (i, 0)
    fixed = lambda i: (0, 0)
    return pl.pallas_call(
        _epilogue_kernel,
        grid=(S // tr,),
        in_specs=[pl.BlockSpec((tr, D_MODEL), row),
                  pl.BlockSpec((tr, TAIL), row),
                  pl.BlockSpec((tr, D_MODEL), row),
                  pl.BlockSpec((N_MEM, MEM_WIDTH), fixed),
                  pl.BlockSpec((N_MEM, MEM_WIDTH), fixed),
                  pl.BlockSpec((INNER, D_MODEL), fixed),
                  pl.BlockSpec((1, D_MODEL), fixed),
                  pl.BlockSpec((1, D_MODEL), fixed)],
        out_specs=pl.BlockSpec((tr, D_MODEL), row),
        out_shape=jax.ShapeDtypeStruct((S, D_MODEL), F32),
        compiler_params=_cparams("parallel"),
        name="epilogue",
    )(branch, tail, h, mk, mv, w_out, g.reshape(1, -1), b.reshape(1, -1))


def _dup_kv_columns(w):
    d = w.shape[0]
    w4 = w.reshape(d, GQA_KV_HEADS, 1, HEAD_DIM)
    return jnp.broadcast_to(w4, (d, GQA_KV_HEADS, 2, HEAD_DIM)).reshape(d, 2 * GQA_KV_HEADS * HEAD_DIM)


def kernel(x, mem, ln_in_g, ln_in_b, w_mem_kv, w_in_a, w_in_b, q_norm_g, k_norm_g, w_in_c, w_out, ln_g, ln_b):
    B, S, D = x.shape
    assert B == 1 and D == D_MODEL and S % DIL_CHUNK == 0 and S % (FFT_N2 * SUBLANES) == 0

    mkv = _proj(mem[0], w_mem_kv.astype(BF16), name="proj_mem").astype(BF16)
    mk, mv = mkv[:, :MEM_WIDTH], mkv[:, MEM_WIDTH:]
    h = _ln_in(x[0], ln_in_g, ln_in_b)
    fft_tabs = None
    qw = GQA_Q_HEADS * HEAD_DIM
    kvw = GQA_KV_HEADS * HEAD_DIM

    for i in range(DEPTH):
        kind, j = i % 3, i // 3
        if kind == 0:
            if fft_tabs is None:
                fft_tabs = _fft_tables(S)
            fc, gk, hm = fft_tabs
            tail, z = _proj_a(h, w_in_a[j].astype(BF16), fc)
            branch = _fft_stage2(_fft_stage1(z, gk), hm)
        elif kind == 1:
            w = w_in_b[j]
            w_qkv = jnp.concatenate([w[:, :qw], _dup_kv_columns(w[:, qw:qw + kvw]),
                                     w[:, qw + kvw:qw + 2 * kvw]], axis=1).astype(BF16)
            cos_t, sin_t = (jnp.asarray(t, F32) for t in _rope_tables_axial(S))
            gq = jnp.tile(q_norm_g[j], 2).reshape(1, LANES)
            gkn = jnp.tile(k_norm_g[j], 2).reshape(1, LANES)
            qk, vt = _proj_b(h, w_qkv, cos_t, sin_t, gq, gkn)
            tail = _proj(h, w[:, qw + 2 * kvw:].astype(BF16), name="proj_tail")
            branch = _gqa_attention(qk, vt)
        else:
            w = w_in_c[j].astype(BF16)
            cos_t, sin_t = _rope_tables_1d(S)
            qkvs = [_proj_c(h, w, cos_t, sin_t, g, dil) for g, dil in enumerate(DIL_DILATIONS)]
            tail = _proj(h, w[:, w.shape[1] - TAIL:], name="proj_tail")
            branch = _dilated_attention(qkvs)
        h = _epilogue(branch, tail, h, mk, mv, w_out[i].astype(BF16), ln_g[i], ln_b[i])
    return h[None]
```

```python
import functools

import jax
import jax.numpy as jnp
import numpy as np
from jax import lax
from jax.experimental import pallas as pl
from jax.experimental.pallas import tpu as pltpu

F32 = jnp.float32
BF16 = jnp.bfloat16

D_MODEL = 1024
DEPTH = 4
N_MEM = 256
GRID_W = 64
HEAD_DIM = 64
ROPE_THETA = 10000.0
LN_EPS = 1e-5
RMS_EPS = 1e-6
MASK_VALUE = -1e30
FNET_GROUPS = 4
FNET_GROUP_W = D_MODEL // FNET_GROUPS
GQA_Q_HEADS = 16
GQA_KV_HEADS = 4
DIL_DILATIONS = (1, 4, 16)
DIL_SIDE = 64
MEM_WIDTH = 256
INNER = D_MODEL + MEM_WIDTH
TAIL = INNER + MEM_WIDTH
DEEPNORM_ALPHA = (2.0 * DEPTH) ** 0.25
ATTN_SCALE = HEAD_DIM ** -0.5
LOG2_E = float(np.log2(np.e))
VT_ROWS = HEAD_DIM + 16
LANES = 128
SUBLANES = 8
FFT_N2 = 128
DIL_CHUNK = 1024
VMEM_LIMIT = 48 << 20


def _cparams(*sem):
    return pltpu.CompilerParams(dimension_semantics=sem, vmem_limit_bytes=VMEM_LIMIT)


def _dot(a, b):
    return jnp.dot(a, b, preferred_element_type=F32)


def _dot_nt(a, b):
    return lax.dot_general(a, b, (((1,), (1,)), ((), ())), preferred_element_type=F32)


def _lane_lo(shape=(1, LANES)):
    return lax.broadcasted_iota(jnp.int32, shape, len(shape) - 1) % LANES < HEAD_DIM


def _layer_norm_rows(z, g, b):
    mu = jnp.mean(z, axis=-1, keepdims=True)
    zc = z - mu
    var = jnp.mean(zc * zc, axis=-1, keepdims=True)
    return zc * lax.rsqrt(var + LN_EPS) * g + b


def _ln_kernel(x_ref, g_ref, b_ref, o_ref):
    o_ref[...] = _layer_norm_rows(x_ref[...], g_ref[...], b_ref[...])


def _ln_in(x, g, b, tr=512):
    S = x.shape[0]
    return pl.pallas_call(
        _ln_kernel,
        grid=(S // tr,),
        in_specs=[pl.BlockSpec((tr, D_MODEL), lambda i: (i, 0)),
                  pl.BlockSpec((1, D_MODEL), lambda i: (0, 0)),
                  pl.BlockSpec((1, D_MODEL), lambda i: (0, 0))],
        out_specs=pl.BlockSpec((tr, D_MODEL), lambda i: (i, 0)),
        out_shape=jax.ShapeDtypeStruct((S, D_MODEL), F32),
        compiler_params=_cparams("parallel"),
        name="ln_in",
    )(x, g.reshape(1, -1), b.reshape(1, -1))


def _proj_kernel(x_ref, w_ref, o_ref):
    o_ref[...] = _dot(x_ref[...].astype(BF16), w_ref[...])


def _proj(x, w, tr=512, name="proj"):
    S, K = x.shape
    N = w.shape[1]
    tr = min(tr, S)
    return pl.pallas_call(
        _proj_kernel,
        grid=(S // tr,),
        in_specs=[pl.BlockSpec((tr, K), lambda i: (i, 0)),
                  pl.BlockSpec((K, N), lambda i: (0, 0))],
        out_specs=pl.BlockSpec((tr, N), lambda i: (i, 0)),
        out_shape=jax.ShapeDtypeStruct((S, N), F32),
        compiler_params=_cparams("parallel"),
        name=name,
    )(x, w)


def _proj_a_kernel(x_ref, w_ref, fc_ref, tail_ref, z_ref):
    xb = x_ref[...].astype(BF16)
    tail_ref[...] = _dot(xb, w_ref[...])
    for g in range(FNET_GROUPS):
        cols = slice(g * FNET_GROUP_W, (g + 1) * FNET_GROUP_W)
        zg = _dot(xb[:, cols], fc_ref[...])
        z_ref[0, :, cols] = zg[:, :FNET_GROUP_W]
        z_ref[1, :, cols] = zg[:, FNET_GROUP_W:]


def _proj_a(h, w_tail, fc, tr=512):
    S = h.shape[0]
    return pl.pallas_call(
        _proj_a_kernel,
        grid=(S // tr,),
        in_specs=[pl.BlockSpec((tr, D_MODEL), lambda i: (i, 0)),
                  pl.BlockSpec((D_MODEL, TAIL), lambda i: (0, 0)),
                  pl.BlockSpec((FNET_GROUP_W, 2 * FNET_GROUP_W), lambda i: (0, 0))],
        out_specs=[pl.BlockSpec((tr, TAIL), lambda i: (i, 0)),
                   pl.BlockSpec((2, tr, D_MODEL), lambda i: (0, i, 0))],
        out_shape=[jax.ShapeDtypeStruct((S, TAIL), F32),
                   jax.ShapeDtypeStruct((2, S, D_MODEL), F32)],
        compiler_params=_cparams("parallel"),
        name="proj_a",
    )(h, w_tail, fc)


def _fft1_kernel(z_ref, g_ref, t_ref):
    two, n1, _, sub, tc = z_ref.shape
    x = z_ref[...].reshape(two * n1 * sub, tc).astype(BF16)
    t_ref[...] = _dot(g_ref[...], x).reshape(t_ref.shape)


def _fft_stage1(z, gk, tc=512):
    _, S, D = z.shape
    n1 = S // FFT_N2
    nu = FFT_N2 // SUBLANES
    z5 = z.reshape(2, n1, nu, SUBLANES, D)
    rows = 2 * n1 * SUBLANES
    t5 = pl.pallas_call(
        _fft1_kernel,
        grid=(nu, D // tc),
        in_specs=[pl.BlockSpec((2, n1, 1, SUBLANES, tc), lambda u, c: (0, 0, u, 0, c)),
                  pl.BlockSpec((rows, rows), lambda u, c: (0, 0))],
        out_specs=pl.BlockSpec((n1, 2, 1, SUBLANES, tc), lambda u, c: (0, 0, u, 0, c)),
        out_shape=jax.ShapeDtypeStruct((n1, 2, nu, SUBLANES, D), F32),
        compiler_params=_cparams("parallel", "parallel"),
        name="fft_stage1",
    )(z5, gk)
    return t5.reshape(n1, 2, FFT_N2, D)


def _fft2_kernel(t_ref, h_ref, o_ref, slab_ref):
    nj, _, n2, d = t_ref.shape
    nslab = d // LANES
    for j in range(nj):
        tj = t_ref[j].reshape(2 * n2, d).astype(BF16)
        r = _dot(h_ref[j], tj)
        for s in range(nslab):
            slab_ref[s, pl.ds(j, n2, stride=nj), :] = r[:, s * LANES:(s + 1) * LANES]
    for s in range(nslab):
        o_ref[:, :, s * LANES:(s + 1) * LANES] = slab_ref[s].reshape(n2, nj, LANES)


def _fft_stage2(t, hmat):
    n1, _, n2, D = t.shape
    nj = SUBLANES
    y3 = pl.pallas_call(
        _fft2_kernel,
        grid=(n1 // nj,),
        in_specs=[pl.BlockSpec((nj, 2, n2, D), lambda a: (a, 0, 0, 0)),
                  pl.BlockSpec((nj, n2, 2 * n2), lambda a: (a, 0, 0))],
        out_specs=pl.BlockSpec((n2, nj, D), lambda a: (0, a, 0)),
        out_shape=jax.ShapeDtypeStruct((n2, n1, D), F32),
        scratch_shapes=[pltpu.VMEM((D // LANES, n2 * nj, LANES), F32)],
        compiler_params=_cparams("parallel"),
        name="fft_stage2",
    )(t, hmat)
    return y3.reshape(n2 * n1, D)


def _fft_tables(S):
    n1, n2 = S // FFT_N2, FFT_N2
    c = np.arange(FNET_GROUP_W)
    ang = (2.0 * np.pi / FNET_GROUP_W) * ((c[:, None] * c[None, :]) % FNET_GROUP_W)
    scale = 1.0 / np.sqrt(float(S) * FNET_GROUP_W)
    fc = np.concatenate([np.cos(ang), -np.sin(ang)], axis=1) * scale
    k1 = np.arange(n1)
    th = (2.0 * np.pi / n1) * ((k1[:, None] * k1[None, :]) % n1)
    cs, sn = np.cos(th), np.sin(th)
    g = np.stack([np.stack([cs, sn], axis=1), np.stack([-sn, cs], axis=1)], axis=1)
    gk = np.kron(g.reshape(2 * n1, 2 * n1), np.eye(SUBLANES))
    k2 = np.arange(n2)
    kk = k1[:, None, None] + n1 * k2[None, :, None]
    ph = (2.0 * np.pi / S) * ((k2[None, None, :] * kk) % S)
    hm = np.concatenate([np.cos(ph), np.sin(ph)], axis=2)
    return tuple(jnp.asarray(t, F32).astype(BF16) for t in (fc, gk, hm))


def _rope_angles(pos, dim):
    inv_freq = ROPE_THETA ** (-(np.arange(0, dim, 2, dtype=np.float64) / dim))
    return pos.astype(np.float64)[:, None] * inv_freq[None, :]


def _rope_tables_axial(S):
    t = np.arange(S)
    ar = _rope_angles(t // GRID_W, HEAD_DIM // 2)
    ac = _rope_angles(t % GRID_W, HEAD_DIM // 2)
    cos = np.concatenate([np.cos(ar), np.cos(ar), np.cos(ac), np.cos(ac)], axis=1)
    sin = np.concatenate([-np.sin(ar), np.sin(ar), -np.sin(ac), np.sin(ac)], axis=1)
    return np.tile(cos, (1, 2)), np.tile(sin, (1, 2))


def _rope_tables_1d(S):
    a = _rope_angles(np.arange(S), HEAD_DIM)
    cos = np.concatenate([np.cos(a), np.cos(a)], axis=1)
    sin = np.concatenate([-np.sin(a), np.sin(a)], axis=1)
    return np.tile(cos, (1, 2)), np.tile(sin, (1, 2))


def _rotate_partner(x, half):
    lane = lax.broadcasted_iota(jnp.int32, (1, LANES), 1)
    first = lane % (2 * half) < half
    return jnp.where(first, pltpu.roll(x, LANES - half, 1), pltpu.roll(x, half, 1))


def _proj_b_kernel(x_ref, w_ref, cos_ref, sin_ref, gq_ref, gk_ref, o_ref, vt_ref):
    xb = x_ref[...].astype(BF16)
    cos_t, sin_t = cos_ref[...], sin_ref[...]
    lo = _lane_lo()
    nb_w = 4 * LANES
    vt = _dot(xb, w_ref[:, 3 * nb_w:]).T.astype(BF16)
    ones = jnp.ones((VT_ROWS - HEAD_DIM, vt.shape[1]), BF16)
    for hd in range(GQA_KV_HEADS):
        vt_ref[hd * VT_ROWS:hd * VT_ROWS + HEAD_DIM, :] = vt[hd * HEAD_DIM:(hd + 1) * HEAD_DIM]
        vt_ref[hd * VT_ROWS + HEAD_DIM:(hd + 1) * VT_ROWS, :] = ones
    for nb in range(3):
        r = _dot(xb, w_ref[:, nb * nb_w:(nb + 1) * nb_w])
        gain = gq_ref[...] if nb < 2 else gk_ref[...]
        scale = ATTN_SCALE * LOG2_E if nb < 2 else 1.0
        for t in range(4):
            rt = r[:, t * LANES:(t + 1) * LANES]
            r2 = rt * rt
            tot = jnp.sum(r2, axis=1, keepdims=True)
            low = jnp.sum(jnp.where(lo, r2, 0.0), axis=1, keepdims=True)
            ss = jnp.where(lo, low, tot - low)
            xn = rt * lax.rsqrt(ss * (1.0 / HEAD_DIM) + RMS_EPS) * gain
            out = (xn * cos_t + _rotate_partner(xn, HEAD_DIM // 4) * sin_t) * scale
            c0 = nb * nb_w + t * LANES
            o_ref[:, c0:c0 + LANES] = out.astype(BF16)


def _proj_b(h, w_qkv, cos_t, sin_t, gq, gk, tr=512):
    S = h.shape[0]
    n_in = w_qkv.shape[1]
    n_out = n_in - GQA_KV_HEADS * HEAD_DIM
    vw = GQA_KV_HEADS * VT_ROWS
    return pl.pallas_call(
        _proj_b_kernel,
        grid=(S // tr,),
        in_specs=[pl.BlockSpec((tr, D_MODEL), lambda i: (i, 0)),
                  pl.BlockSpec((D_MODEL, n_in), lambda i: (0, 0)),
                  pl.BlockSpec((tr, LANES), lambda i: (i, 0)),
                  pl.BlockSpec((tr, LANES), lambda i: (i, 0)),
                  pl.BlockSpec((1, LANES), lambda i: (0, 0)),
                  pl.BlockSpec((1, LANES), lambda i: (0, 0))],
        out_specs=[pl.BlockSpec((tr, n_out), lambda i: (i, 0)),
                   pl.BlockSpec((vw, tr), lambda i: (0, i))],
        out_shape=[jax.ShapeDtypeStruct((S, n_out), BF16),
                   jax.ShapeDtypeStruct((vw, S), BF16)],
        compiler_params=_cparams("parallel"),
        name="proj_b",
    )(h, w_qkv, cos_t, sin_t, gq, gk)


def _gqa_kernel(q_ref, k_ref, vt_ref, o_ref, qs_ref, m_ref, acc_ref, *s_refs, tk, depth):
    tq = q_ref.shape[0]
    S = k_ref.shape[0]
    nq = 4 * tq
    lo = _lane_lo()
    for t in range(2):
        qt = q_ref[:, t * LANES:(t + 1) * LANES].astype(F32)
        qs_ref[(2 * t) * tq:(2 * t + 1) * tq, :] = jnp.where(lo, qt, 0.0).astype(BF16)
        qs_ref[(2 * t + 1) * tq:(2 * t + 2) * tq, :] = jnp.where(lo, 0.0, qt).astype(BF16)
    m_ref[...] = jnp.full(m_ref.shape, -jnp.inf, F32)
    acc_ref[...] = jnp.zeros(acc_ref.shape, F32)

    nkv = S // tk
    nbuf = len(s_refs)

    def scores(j, s_ref):
        k0 = pl.multiple_of(j * tk, tk)
        s = _dot_nt(k_ref[pl.ds(k0, tk), :], qs_ref[...])
        s_ref[...] = s
        return jnp.max(s.reshape(tk // SUBLANES, SUBLANES, nq), axis=0)

    def accumulate(j, s_ref, cmax):
        k0 = pl.multiple_of(j * tk, tk)
        m_prev = m_ref[...]
        m_new = jnp.maximum(m_prev, jnp.max(cmax, axis=0, keepdims=True))
        p = jnp.exp2(s_ref[...] - jnp.tile(m_new, (tk // SUBLANES, 1)))
        alpha = jnp.exp2(m_prev - m_new)
        pv = _dot(vt_ref[:, pl.ds(k0, tk)], p.astype(BF16))
        acc_ref[...] = jnp.tile(alpha, (VT_ROWS // SUBLANES, 1)) * acc_ref[...] + pv
        m_ref[...] = m_new

    def body(jj, cmax):
        j = depth * jj
        for u in range(depth):
            nxt = j + u + 1
            if u == depth - 1:
                nxt = jnp.minimum(nxt, nkv - 1)
            cmax_next = scores(nxt, s_refs[(u + 1) % nbuf])
            accumulate(j + u, s_refs[u % nbuf], cmax)
            cmax = cmax_next
        return cmax

    lax.fori_loop(0, nkv // depth, body, scores(0, s_refs[0]))
    acc = acc_ref[...]
    ot = acc[:HEAD_DIM] / acc[HEAD_DIM:HEAD_DIM + 1]
    for t in range(2):
        pair = jnp.concatenate([ot[:, (2 * t) * tq:(2 * t + 1) * tq],
                                ot[:, (2 * t + 1) * tq:(2 * t + 2) * tq]], axis=0)
        o_ref[:, t * LANES:(t + 1) * LANES] = pair.T


def _gqa_attention(qk, vt, tq=128, tk=512, depth=8, nbuf=2):
    S = qk.shape[0]
    qw = 4 * HEAD_DIM
    k_blk0 = GQA_Q_HEADS * HEAD_DIM // LANES
    tk = min(tk, S)
    depth = min(depth, S // tk)
    assert depth % nbuf == 0 and (S // tk) % depth == 0
    return pl.pallas_call(
        functools.partial(_gqa_kernel, tk=tk, depth=depth),
        grid=(GQA_KV_HEADS, S // tq),
        in_specs=[pl.BlockSpec((tq, qw), lambda h, i: (i, h)),
                  pl.BlockSpec((S, LANES), lambda h, i: (0, k_blk0 + h)),
                  pl.BlockSpec((VT_ROWS, S), lambda h, i: (h, 0))],
        out_specs=pl.BlockSpec((tq, qw), lambda h, i: (i, h)),
        out_shape=jax.ShapeDtypeStruct((S, D_MODEL), F32),
        scratch_shapes=[pltpu.VMEM((4 * tq, LANES), BF16),
                        pltpu.VMEM((SUBLANES, 4 * tq), F32),
                        pltpu.VMEM((VT_ROWS, 4 * tq), F32)]
        + [pltpu.VMEM((tk, 4 * tq), F32)] * nbuf,
        compiler_params=_cparams("parallel", "parallel"),
        name="gqa_attention",
    )(qk, qk, vt)


def _proj_c_kernel(x_ref, w_ref, cos_ref, sin_ref, o_ref, xp_ref, slab_ref, *, dil):
    kind = pl.program_id(1)
    tm, d = x_ref.shape[0], o_ref.shape[-1]
    n = tm // dil
    nslab = d // LANES
    cb_w = 2 * LANES

    @pl.when(kind == 0)
    def _():
        if dil == 1:
            xp_ref[...] = x_ref[...].astype(BF16)
            return
        for s in range(nslab):
            slab_ref[s] = x_ref[:, s * LANES:(s + 1) * LANES]
        for c in range(dil):
            for s in range(nslab):
                xp_ref[c * n:(c + 1) * n, s * LANES:(s + 1) * LANES] = (
                    slab_ref[s, pl.ds(c, n, stride=dil), :].astype(BF16))

    def emit(c0, vals):
        o_ref[0, 0, :, :, c0:c0 + vals.shape[1]] = vals.astype(BF16).reshape(dil, n, vals.shape[1])

    @pl.when(kind < 2)
    def _():
        scale = jnp.where(kind == 0, ATTN_SCALE * LOG2_E, 1.0).astype(F32)
        cos_t = cos_ref[...] * scale
        sin_t = sin_ref[...] * scale
        ncb = d // cb_w
        r_next = _dot(xp_ref[...], w_ref[:, :cb_w])
        for cb in range(ncb):
            r = r_next
            if cb + 1 < ncb:
                r_next = _dot(xp_ref[...], w_ref[:, (cb + 1) * cb_w:(cb + 2) * cb_w])
            for s in range(cb_w // LANES):
                rt = r[:, s * LANES:(s + 1) * LANES]
                emit(cb * cb_w + s * LANES, rt * cos_t + _rotate_partner(rt, HEAD_DIM // 2) * sin_t)

    @pl.when(kind == 2)
    def _():
        for cb in range(d // cb_w):
            emit(cb * cb_w, _dot(xp_ref[...], w_ref[:, cb * cb_w:(cb + 1) * cb_w]))


def _regroup_rows(tab, dil):
    S, w = tab.shape
    return tab.reshape(S // DIL_CHUNK, DIL_CHUNK // dil, dil, w).transpose(0, 2, 1, 3).reshape(S, w)


def _proj_c(h, w, cos_np, sin_np, group, dil):
    S = h.shape[0]
    tm = DIL_CHUNK
    n = tm // dil
    ngroups = len(DIL_DILATIONS)
    cos_t = jnp.asarray(_regroup_rows(cos_np, dil), F32)
    sin_t = jnp.asarray(_regroup_rows(sin_np, dil), F32)
    return pl.pallas_call(
        functools.partial(_proj_c_kernel, dil=dil),
        grid=(S // tm, 3),
        in_specs=[pl.BlockSpec((tm, D_MODEL), lambda i, t: (i, 0)),
                  pl.BlockSpec((D_MODEL, D_MODEL), lambda i, t: (0, ngroups * t + group)),
                  pl.BlockSpec((tm, LANES), lambda i, t: (i, 0)),
                  pl.BlockSpec((tm, LANES), lambda i, t: (i, 0))],
        out_specs=pl.BlockSpec((1, 1, dil, n, D_MODEL), lambda i, t: (t, i, 0, 0, 0)),
        out_shape=jax.ShapeDtypeStruct((3, S // tm, dil, n, D_MODEL), BF16),
        scratch_shapes=[pltpu.VMEM((tm, D_MODEL), BF16),
                        pltpu.VMEM((D_MODEL // LANES, tm, LANES), F32)],
        compiler_params=_cparams("parallel", "arbitrary"),
        name=f"proj_c_dil{dil}",
    )(h, w, cos_t, sin_t)


def _dil_kernel(*refs, nchunks):
    ng = len(DIL_DILATIONS)
    o_ref, osc_ref, lsc_ref = refs[7 * ng:]
    i = pl.program_id(0)
    lo = _lane_lo()

    groups, blocks = [], []
    for g, dil in enumerate(DIL_DILATIONS):
        n = DIL_CHUNK // dil
        bq = min(n, 2 * DIL_SIDE)
        nk = bq + 2 * DIL_SIDE
        qi = lax.broadcasted_iota(jnp.int32, (2 * bq, nk), 0) % bq
        kj = lax.broadcasted_iota(jnp.int32, (2 * bq, nk), 1)
        band = jnp.where(jnp.abs(kj - DIL_SIDE - qi) <= DIL_SIDE, 0.0, MASK_VALUE).astype(F32)
        groups.append(dict(dil=dil, n=n, bq=bq, nk=nk, nsub=n // bq, kj=kj, band=band,
                           kpad=-(n + 2 * DIL_SIDE) % LANES, refs=refs[7 * g:7 * g + 7]))
        blocks += [(g, c, b) for c in range(dil) for b in range(n // bq)]

    class_kv = {}

    def keys_values(g, c):
        if (g, c) not in class_kv:
            G = groups[g]
            _, kc_ref, kp_ref, kn_ref, vc_ref, vp_ref, vn_ref = G["refs"]
            kparts = [kp_ref[0, 0, c], kc_ref[0, 0, c], kn_ref[0, 0, c]]
            if G["kpad"]:
                kparts.append(jnp.zeros((G["kpad"], LANES), BF16))
            kcat_t = jnp.concatenate(kparts, axis=0).astype(F32).T.astype(BF16)
            vcat = jnp.concatenate([vp_ref[0, 0, c], vc_ref[0, 0, c], vn_ref[0, 0, c]], axis=0)
            class_kv[(g, c)] = (kcat_t, vcat)
        return class_kv[(g, c)]

    def scores(blk):
        g, c, b = blk
        G = groups[g]
        bq, nk, kj = G["bq"], G["nk"], G["kj"]
        p0 = b * bq
        bias = G["band"]
        if b == 0:
            bias = jnp.where(kj >= jnp.where(i == 0, DIL_SIDE, 0), bias, MASK_VALUE)
        if b == G["nsub"] - 1:
            bias = jnp.where(kj < jnp.where(i == nchunks - 1, bq + DIL_SIDE, nk), bias, MASK_VALUE)
        qf = G["refs"][0][0, 0, c, p0:p0 + bq, :].astype(F32)
        qs = jnp.concatenate([jnp.where(lo, qf, 0.0), jnp.where(lo, 0.0, qf)], axis=0)
        return _dot(qs.astype(BF16), keys_values(g, c)[0][:, p0:p0 + nk]) + bias

    def finish(blk, s):
        g, c, b = blk
        G = groups[g]
        bq, nk, dil = G["bq"], G["nk"], G["dil"]
        p0 = b * bq
        m = jnp.max(s, axis=1, keepdims=True)
        p = jnp.exp2(s - m)
        l = jnp.sum(p, axis=1, keepdims=True)
        o2 = _dot(p.astype(BF16), keys_values(g, c)[1][p0:p0 + nk]) / l
        lse2 = m + jnp.log2(l)
        o_pair = jnp.where(lo, o2[:bq], o2[bq:])
        l_pair = jnp.where(lo, lse2[:bq], lse2[bq:])
        rows = pl.ds(p0, bq) if dil == 1 else pl.ds(p0 * dil + c, bq, stride=dil)
        osc_ref[g, rows, :] = o_pair
        lsc_ref[g, rows, :] = l_pair

    s_next = scores(blocks[0])
    for idx, blk in enumerate(blocks):
        s_cur = s_next
        if idx + 1 < len(blocks):
            s_next = scores(blocks[idx + 1])
        finish(blk, s_cur)

    ls = [lsc_ref[g] for g in range(ng)]
    mx = functools.reduce(jnp.maximum, ls)
    ws = [jnp.exp2(l - mx) for l in ls]
    num = sum(w * osc_ref[g] for g, w in enumerate(ws))
    o_ref[...] = num / sum(ws)


def _dilated_attention(qkvs):
    nchunks = qkvs[0].shape[1]
    S = nchunks * DIL_CHUNK
    in_specs, args = [], []
    for dil, a in zip(DIL_DILATIONS, qkvs):
        n = DIL_CHUNK // dil
        nblk = n // DIL_SIDE
        full = (1, 1, dil, n, LANES)
        halo = (1, 1, dil, DIL_SIDE, LANES)

        def cur(t):
            return lambda i, m: (t, i, 0, 0, m)

        def prev(t, nblk=nblk):
            return lambda i, m: (t, jnp.maximum(i - 1, 0), 0, nblk - 1, m)

        def nxt(t):
            return lambda i, m: (t, jnp.minimum(i + 1, nchunks - 1), 0, 0, m)

        in_specs += [pl.BlockSpec(full, cur(0)),
                     pl.BlockSpec(full, cur(1)), pl.BlockSpec(halo, prev(1)), pl.BlockSpec(halo, nxt(1)),
                     pl.BlockSpec(full, cur(2)), pl.BlockSpec(halo, prev(2)), pl.BlockSpec(halo, nxt(2))]
        args += [a] * 7
    ng = len(DIL_DILATIONS)
    return pl.pallas_call(
        functools.partial(_dil_kernel, nchunks=nchunks),
        grid=(nchunks, D_MODEL // LANES),
        in_specs=in_specs,
        out_specs=pl.BlockSpec((DIL_CHUNK, LANES), lambda i, m: (i, m)),
        out_shape=jax.ShapeDtypeStruct((S, D_MODEL), F32),
        scratch_shapes=[pltpu.VMEM((ng, DIL_CHUNK, LANES), F32),
                        pltpu.VMEM((ng, DIL_CHUNK, LANES), F32)],
        compiler_params=_cparams("parallel", "parallel"),
        name="dilated_attention",
    )(*args)


def _epilogue_kernel(br_ref, tail_ref, h_ref, mk_ref, mv_ref, wo_ref, g_ref, b_ref, o_ref):
    lo = _lane_lo()
    gate_b = tail_ref[:, :D_MODEL]
    yb = (br_ref[...] * (gate_b * jax.nn.sigmoid(gate_b))).astype(BF16)
    y = _dot(yb, wo_ref[:D_MODEL, :])
    for t in range(MEM_WIDTH // LANES):
        cols = slice(t * LANES, (t + 1) * LANES)
        qf = tail_ref[:, INNER + t * LANES:INNER + (t + 1) * LANES] * ATTN_SCALE
        kt, vt = mk_ref[:, cols], mv_ref[:, cols]
        outs = []
        for qh in (jnp.where(lo, qf, 0.0), jnp.where(lo, 0.0, qf)):
            s = _dot_nt(qh.astype(BF16), kt)
            m = jnp.max(s, axis=1, keepdims=True)
            p = jnp.exp(s - m)
            l = jnp.sum(p, axis=1, keepdims=True)
            outs.append(_dot(p.astype(BF16), vt) / l)
        mem_out = jnp.where(lo, outs[0], outs[1])
        gm = tail_ref[:, D_MODEL + t * LANES:D_MODEL + (t + 1) * LANES]
        ym = (mem_out * (gm * jax.nn.sigmoid(gm))).astype(BF16)
        y = y + _dot(ym, wo_ref[D_MODEL + t * LANES:D_MODEL + (t + 1) * LANES, :])
    z = DEEPNORM_ALPHA * h_ref[...] + y
    o_ref[...] = _layer_norm_rows(z, g_ref[...], b_ref[...])


def _epilogue(branch, tail, h, mk, mv, w_out, g, b, tr=512):
    S = h.shape[0]
    row = lambda i: (i, 0)
    fixed = lambda i: (0, 0)
    return pl.pallas_call(
        _epilogue_kernel,
        grid=(S // tr,),
        in_specs=[pl.BlockSpec((tr, D_MODEL), row),
                  pl.BlockSpec((tr, TAIL), row),
                  pl.BlockSpec((tr, D_MODEL), row),
                  pl.BlockSpec((N_MEM, MEM_WIDTH), fixed),
                  pl.BlockSpec((N_MEM, MEM_WIDTH), fixed),
                  pl.BlockSpec((INNER, D_MODEL), fixed),
                  pl.BlockSpec((1, D_MODEL), fixed),
                  pl.BlockSpec((1, D_MODEL), fixed)],
        out_specs=pl.BlockSpec((tr, D_MODEL), row),
        out_shape=jax.ShapeDtypeStruct((S, D_MODEL), F32),
        compiler_params=_cparams("parallel"),
        name="epilogue",
    )(branch, tail, h, mk, mv, w_out, g.reshape(1, -1), b.reshape(1, -1))


def _dup_kv_columns(w):
    d = w.shape[0]
    w4 = w.reshape(d, GQA_KV_HEADS, 1, HEAD_DIM)
    return jnp.broadcast_to(w4, (d, GQA_KV_HEADS, 2, HEAD_DIM)).reshape(d, 2 * GQA_KV_HEADS * HEAD_DIM)


def kernel(x, mem, ln_in_g, ln_in_b, w_mem_kv, w_in_a, w_in_b, q_norm_g, k_norm_g, w_in_c, w_out, ln_g, ln_b):
    B, S, D = x.shape
    assert B == 1 and D == D_MODEL and S % DIL_CHUNK == 0 and S % (FFT_N2 * SUBLANES) == 0

    mkv = _proj(mem[0], w_mem_kv.astype(BF16), name="proj_mem").astype(BF16)
    mk, mv = mkv[:, :MEM_WIDTH], mkv[:, MEM_WIDTH:]
    h = _ln_in(x[0], ln_in_g, ln_in_b)
    fft_tabs = None
    qw = GQA_Q_HEADS * HEAD_DIM
    kvw = GQA_KV_HEADS * HEAD_DIM

    for i in range(DEPTH):
        kind, j = i % 3, i // 3
        if kind == 0:
            if fft_tabs is None:
                fft_tabs = _fft_tables(S)
            fc, gk, hm = fft_tabs
            tail, z = _proj_a(h, w_in_a[j].astype(BF16), fc)
            branch = _fft_stage2(_fft_stage1(z, gk), hm)
        elif kind == 1:
            w = w_in_b[j]
            w_qkv = jnp.concatenate([w[:, :qw], _dup_kv_columns(w[:, qw:qw + kvw]),
                                     w[:, qw + kvw:qw + 2 * kvw]], axis=1).astype(BF16)
            cos_t, sin_t = (jnp.asarray(t, F32) for t in _rope_tables_axial(S))
            gq = jnp.tile(q_norm_g[j], 2).reshape(1, LANES)
            gkn = jnp.tile(k_norm_g[j], 2).reshape(1, LANES)
            qk, vt = _proj_b(h, w_qkv, cos_t, sin_t, gq, gkn)
            tail = _proj(h, w[:, qw + 2 * kvw:].astype(BF16), name="proj_tail")
            branch = _gqa_attention(qk, vt)
        else:
            w = w_in_c[j].astype(BF16)
            cos_t, sin_t = _rope_tables_1d(S)
            qkvs = [_proj_c(h, w, cos_t, sin_t, g, dil) for g, dil in enumerate(DIL_DILATIONS)]
            tail = _proj(h, w[:, w.shape[1] - TAIL:], name="proj_tail")
            branch = _dilated_attention(qkvs)
        h = _epilogue(branch, tail, h, mk, mv, w_out[i].astype(BF16), ln_g[i], ln_b[i])
    return h[None]
```

```python
import functools

import jax
import jax.numpy as jnp
import numpy as np
from jax import lax
from jax.experimental import pallas as pl
from jax.experimental.pallas import tpu as pltpu

F32 = jnp.float32
BF16 = jnp.bfloat16

D_MODEL = 1024
DEPTH = 4
N_MEM = 256
GRID_W = 64
HEAD_DIM = 64
ROPE_THETA = 10000.0
LN_EPS = 1e-5
RMS_EPS = 1e-6
MASK_VALUE = -1e30
FNET_GROUPS = 4
FNET_GROUP_W = D_MODEL // FNET_GROUPS
GQA_Q_HEADS = 16
GQA_KV_HEADS = 4
DIL_DILATIONS = (1, 4, 16)
DIL_SIDE = 64
MEM_WIDTH = 256
INNER = D_MODEL + MEM_WIDTH
TAIL = INNER + MEM_WIDTH
DEEPNORM_ALPHA = (2.0 * DEPTH) ** 0.25
ATTN_SCALE = HEAD_DIM ** -0.5
LOG2_E = float(np.log2(np.e))
VT_ROWS = HEAD_DIM + 16
LANES = 128
SUBLANES = 8
FFT_N2 = 256
DIL_CHUNK = 1024
VMEM_LIMIT = 48 << 20


def _cparams(*sem):
    return pltpu.CompilerParams(dimension_semantics=sem, vmem_limit_bytes=VMEM_LIMIT)


def _dot(a, b):
    return jnp.dot(a, b, preferred_element_type=F32)


def _dot_nt(a, b):
    return lax.dot_general(a, b, (((1,), (1,)), ((), ())), preferred_element_type=F32)


def _lane_lo(shape=(1, LANES)):
    return lax.broadcasted_iota(jnp.int32, shape, len(shape) - 1) % LANES < HEAD_DIM


def _layer_norm_rows(z, g, b):
    mu = jnp.mean(z, axis=-1, keepdims=True)
    zc = z - mu
    var = jnp.mean(zc * zc, axis=-1, keepdims=True)
    return zc * lax.rsqrt(var + LN_EPS) * g + b


def _ln_kernel(x_ref, g_ref, b_ref, o_ref):
    o_ref[...] = _layer_norm_rows(x_ref[...], g_ref[...], b_ref[...])


def _ln_in(x, g, b, tr=512):
    S = x.shape[0]
    return pl.pallas_call(
        _ln_kernel,
        grid=(S // tr,),
        in_specs=[pl.BlockSpec((tr, D_MODEL), lambda i: (i, 0)),
                  pl.BlockSpec((1, D_MODEL), lambda i: (0, 0)),
                  pl.BlockSpec((1, D_MODEL), lambda i: (0, 0))],
        out_specs=pl.BlockSpec((tr, D_MODEL), lambda i: (i, 0)),
        out_shape=jax.ShapeDtypeStruct((S, D_MODEL), F32),
        compiler_params=_cparams("parallel"),
        name="ln_in",
    )(x, g.reshape(1, -1), b.reshape(1, -1))


def _proj_kernel(x_ref, w_ref, o_ref):
    o_ref[...] = _dot(x_ref[...].astype(BF16), w_ref[...])


def _proj(x, w, tr=512, name="proj"):
    S, K = x.shape
    N = w.shape[1]
    tr = min(tr, S)
    return pl.pallas_call(
        _proj_kernel,
        grid=(S // tr,),
        in_specs=[pl.BlockSpec((tr, K), lambda i: (i, 0)),
                  pl.BlockSpec((K, N), lambda i: (0, 0))],
        out_specs=pl.BlockSpec((tr, N), lambda i: (i, 0)),
        out_shape=jax.ShapeDtypeStruct((S, N), F32),
        compiler_params=_cparams("parallel"),
        name=name,
    )(x, w)


def _proj_a_kernel(x_ref, fc_ref, z_ref):
    xb = x_ref[...].astype(BF16)
    for g in range(FNET_GROUPS):
        cols = slice(g * FNET_GROUP_W, (g + 1) * FNET_GROUP_W)
        zg = _dot(xb[:, cols], fc_ref[...])
        z_ref[0, :, cols] = zg[:, :FNET_GROUP_W]
        z_ref[1, :, cols] = zg[:, FNET_GROUP_W:]


def _proj_a(h, fc, tr=512):
    S = h.shape[0]
    return pl.pallas_call(
        _proj_a_kernel,
        grid=(S // tr,),
        in_specs=[pl.BlockSpec((tr, D_MODEL), lambda i: (i, 0)),
                  pl.BlockSpec((FNET_GROUP_W, 2 * FNET_GROUP_W), lambda i: (0, 0))],
        out_specs=pl.BlockSpec((2, tr, D_MODEL), lambda i: (0, i, 0)),
        out_shape=jax.ShapeDtypeStruct((2, S, D_MODEL), F32),
        compiler_params=_cparams("parallel"),
        name="proj_a",
    )(h, fc)


def _fft1_kernel(z_ref, g_ref, t_ref):
    two, n1, _, sub, tc = z_ref.shape
    x = z_ref[...].reshape(two * n1 * sub, tc).astype(BF16)
    t_ref[...] = _dot(g_ref[...], x).reshape(t_ref.shape)


def _fft_stage1(z, gk, tc=1024):
    _, S, D = z.shape
    n1 = S // FFT_N2
    nu = FFT_N2 // SUBLANES
    z5 = z.reshape(2, n1, nu, SUBLANES, D)
    rows = 2 * n1 * SUBLANES
    t5 = pl.pallas_call(
        _fft1_kernel,
        grid=(nu, D // tc),
        in_specs=[pl.BlockSpec((2, n1, 1, SUBLANES, tc), lambda u, c: (0, 0, u, 0, c)),
                  pl.BlockSpec((rows, rows), lambda u, c: (0, 0))],
        out_specs=pl.BlockSpec((n1, 2, 1, SUBLANES, tc), lambda u, c: (0, 0, u, 0, c)),
        out_shape=jax.ShapeDtypeStruct((n1, 2, nu, SUBLANES, D), F32),
        compiler_params=_cparams("parallel", "parallel"),
        name="fft_stage1",
    )(z5, gk)
    return t5.reshape(n1, 2, FFT_N2, D)


def _fft2_kernel(t_ref, h_ref, o_ref, slab_ref):
    nj, _, n2, d = t_ref.shape
    nslab = d // LANES
    for j in range(nj):
        tj = t_ref[j].reshape(2 * n2, d).astype(BF16)
        r = _dot(h_ref[j], tj)
        for s in range(nslab):
            slab_ref[s, pl.ds(j, n2, stride=nj), :] = r[:, s * LANES:(s + 1) * LANES]
    for s in range(nslab):
        o_ref[:, :, s * LANES:(s + 1) * LANES] = slab_ref[s].reshape(n2, nj, LANES)


def _fft_stage2(t, hmat, tc=512):
    n1, _, n2, D = t.shape
    nj = SUBLANES
    y3 = pl.pallas_call(
        _fft2_kernel,
        grid=(n1 // nj, D // tc),
        in_specs=[pl.BlockSpec((nj, 2, n2, tc), lambda a, c: (a, 0, 0, c)),
                  pl.BlockSpec((nj, n2, 2 * n2), lambda a, c: (a, 0, 0))],
        out_specs=pl.BlockSpec((n2, nj, tc), lambda a, c: (0, a, c)),
        out_shape=jax.ShapeDtypeStruct((n2, n1, D), F32),
        scratch_shapes=[pltpu.VMEM((tc // LANES, n2 * nj, LANES), F32)],
        compiler_params=_cparams("parallel", "parallel"),
        name="fft_stage2",
    )(t, hmat)
    return y3.reshape(n2 * n1, D)


def _fft_tables(S):
    n1, n2 = S // FFT_N2, FFT_N2
    c = np.arange(FNET_GROUP_W)
    ang = (2.0 * np.pi / FNET_GROUP_W) * ((c[:, None] * c[None, :]) % FNET_GROUP_W)
    scale = 1.0 / np.sqrt(float(S) * FNET_GROUP_W)
    fc = np.concatenate([np.cos(ang), -np.sin(ang)], axis=1) * scale
    k1 = np.arange(n1)
    th = (2.0 * np.pi / n1) * ((k1[:, None] * k1[None, :]) % n1)
    cs, sn = np.cos(th), np.sin(th)
    g = np.stack([np.stack([cs, sn], axis=1), np.stack([-sn, cs], axis=1)], axis=1)
    gk = np.kron(g.reshape(2 * n1, 2 * n1), np.eye(SUBLANES))
    k2 = np.arange(n2)
    kk = k1[:, None, None] + n1 * k2[None, :, None]
    ph = (2.0 * np.pi / S) * ((k2[None, None, :] * kk) % S)
    hm = np.concatenate([np.cos(ph), np.sin(ph)], axis=2)
    return tuple(jnp.asarray(t, F32).astype(BF16) for t in (fc, gk, hm))


def _rope_angles(pos, dim):
    inv_freq = ROPE_THETA ** (-(np.arange(0, dim, 2, dtype=np.float64) / dim))
    return pos.astype(np.float64)[:, None] * inv_freq[None, :]


def _rope_tables_axial(S):
    t = np.arange(S)
    ar = _rope_angles(t // GRID_W, HEAD_DIM // 2)
    ac = _rope_angles(t % GRID_W, HEAD_DIM // 2)
    cos = np.concatenate([np.cos(ar), np.cos(ar), np.cos(ac), np.cos(ac)], axis=1)
    sin = np.concatenate([-np.sin(ar), np.sin(ar), -np.sin(ac), np.sin(ac)], axis=1)
    return np.tile(cos, (1, 2)), np.tile(sin, (1, 2))


def _rope_tables_1d(S):
    a = _rope_angles(np.arange(S), HEAD_DIM)
    cos = np.concatenate([np.cos(a), np.cos(a)], axis=1)
    sin = np.concatenate([-np.sin(a), np.sin(a)], axis=1)
    return np.tile(cos, (1, 2)), np.tile(sin, (1, 2))


def _rotate_partner(x, half):
    lane = lax.broadcasted_iota(jnp.int32, (1, LANES), 1)
    first = lane % (2 * half) < half
    return jnp.where(first, pltpu.roll(x, LANES - half, 1), pltpu.roll(x, half, 1))


def _proj_b_kernel(x_ref, w_ref, cos_ref, sin_ref, gq_ref, gk_ref, o_ref, vt_ref):
    xb = x_ref[...].astype(BF16)
    cos_t, sin_t = cos_ref[...], sin_ref[...]
    lo = _lane_lo()
    nb_w = 4 * LANES
    vt = _dot(xb, w_ref[:, 3 * nb_w:]).T.astype(BF16)
    ones = jnp.ones((VT_ROWS - HEAD_DIM, vt.shape[1]), BF16)
    for hd in range(GQA_KV_HEADS):
        vt_ref[hd * VT_ROWS:hd * VT_ROWS + HEAD_DIM, :] = vt[hd * HEAD_DIM:(hd + 1) * HEAD_DIM]
        vt_ref[hd * VT_ROWS + HEAD_DIM:(hd + 1) * VT_ROWS, :] = ones
    for nb in range(3):
        r = _dot(xb, w_ref[:, nb * nb_w:(nb + 1) * nb_w])
        gain = gq_ref[...] if nb < 2 else gk_ref[...]
        scale = ATTN_SCALE * LOG2_E if nb < 2 else 1.0
        for t in range(4):
            rt = r[:, t * LANES:(t + 1) * LANES]
            r2 = rt * rt
            tot = jnp.sum(r2, axis=1, keepdims=True)
            low = jnp.sum(jnp.where(lo, r2, 0.0), axis=1, keepdims=True)
            ss = jnp.where(lo, low, tot - low)
            xn = rt * lax.rsqrt(ss * (1.0 / HEAD_DIM) + RMS_EPS) * gain
            out = (xn * cos_t + _rotate_partner(xn, HEAD_DIM // 4) * sin_t) * scale
            c0 = nb * nb_w + t * LANES
            o_ref[:, c0:c0 + LANES] = out.astype(BF16)


def _proj_b(h, w_qkv, cos_t, sin_t, gq, gk, tr=512):
    S = h.shape[0]
    n_in = w_qkv.shape[1]
    n_out = n_in - GQA_KV_HEADS * HEAD_DIM
    vw = GQA_KV_HEADS * VT_ROWS
    return pl.pallas_call(
        _proj_b_kernel,
        grid=(S // tr,),
        in_specs=[pl.BlockSpec((tr, D_MODEL), lambda i: (i, 0)),
                  pl.BlockSpec((D_MODEL, n_in), lambda i: (0, 0)),
                  pl.BlockSpec((tr, LANES), lambda i: (i, 0)),
                  pl.BlockSpec((tr, LANES), lambda i: (i, 0)),
                  pl.BlockSpec((1, LANES), lambda i: (0, 0)),
                  pl.BlockSpec((1, LANES), lambda i: (0, 0))],
        out_specs=[pl.BlockSpec((tr, n_out), lambda i: (i, 0)),
                   pl.BlockSpec((vw, tr), lambda i: (0, i))],
        out_shape=[jax.ShapeDtypeStruct((S, n_out), BF16),
                   jax.ShapeDtypeStruct((vw, S), BF16)],
        compiler_params=_cparams("parallel"),
        name="proj_b",
    )(h, w_qkv, cos_t, sin_t, gq, gk)


def _gqa_kernel(q_ref, k_ref, vt_ref, o_ref, qs_ref, m_ref, acc_ref, *s_refs, tk, depth):
    tq = q_ref.shape[0]
    S = k_ref.shape[0]
    nq = 4 * tq
    lo = _lane_lo()
    for t in range(2):
        qt = q_ref[:, t * LANES:(t + 1) * LANES].astype(F32)
        qs_ref[(2 * t) * tq:(2 * t + 1) * tq, :] = jnp.where(lo, qt, 0.0).astype(BF16)
        qs_ref[(2 * t + 1) * tq:(2 * t + 2) * tq, :] = jnp.where(lo, 0.0, qt).astype(BF16)
    m_ref[...] = jnp.full(m_ref.shape, -jnp.inf, F32)
    acc_ref[...] = jnp.zeros(acc_ref.shape, F32)

    nkv = S // tk
    nbuf = len(s_refs)

    def scores(j, s_ref):
        k0 = pl.multiple_of(j * tk, tk)
        s = _dot_nt(k_ref[pl.ds(k0, tk), :], qs_ref[...])
        s_ref[...] = s
        return jnp.max(s.reshape(tk // SUBLANES, SUBLANES, nq), axis=0)

    def accumulate(j, s_ref, cmax):
        k0 = pl.multiple_of(j * tk, tk)
        m_prev = m_ref[...]
        m_new = jnp.maximum(m_prev, jnp.max(cmax, axis=0, keepdims=True))
        p = jnp.exp2(s_ref[...] - jnp.tile(m_new, (tk // SUBLANES, 1)))
        alpha = jnp.exp2(m_prev - m_new)
        pv = _dot(vt_ref[:, pl.ds(k0, tk)], p.astype(BF16))
        acc_ref[...] = jnp.tile(alpha, (VT_ROWS // SUBLANES, 1)) * acc_ref[...] + pv
        m_ref[...] = m_new

    def body(jj, cmax):
        j = depth * jj
        for u in range(depth):
            nxt = j + u + 1
            if u == depth - 1:
                nxt = jnp.minimum(nxt, nkv - 1)
            cmax_next = scores(nxt, s_refs[(u + 1) % nbuf])
            accumulate(j + u, s_refs[u % nbuf], cmax)
            cmax = cmax_next
        return cmax

    lax.fori_loop(0, nkv // depth, body, scores(0, s_refs[0]))
    acc = acc_ref[...]
    ot = acc[:HEAD_DIM] / acc[HEAD_DIM:HEAD_DIM + 1]
    for t in range(2):
        pair = jnp.concatenate([ot[:, (2 * t) * tq:(2 * t + 1) * tq],
                                ot[:, (2 * t + 1) * tq:(2 * t + 2) * tq]], axis=0)
        o_ref[:, t * LANES:(t + 1) * LANES] = pair.T


def _gqa_attention(qk, vt, tq=128, tk=512, depth=8, nbuf=2):
    S = qk.shape[0]
    qw = 4 * HEAD_DIM
    k_blk0 = GQA_Q_HEADS * HEAD_DIM // LANES
    tk = min(tk, S)
    depth = min(depth, S // tk)
    assert depth % nbuf == 0 and (S // tk) % depth == 0
    return pl.pallas_call(
        functools.partial(_gqa_kernel, tk=tk, depth=depth),
        grid=(GQA_KV_HEADS, S // tq),
        in_specs=[pl.BlockSpec((tq, qw), lambda h, i: (i, h)),
                  pl.BlockSpec((S, LANES), lambda h, i: (0, k_blk0 + h)),
                  pl.BlockSpec((VT_ROWS, S), lambda h, i: (h, 0))],
        out_specs=pl.BlockSpec((tq, qw), lambda h, i: (i, h)),
        out_shape=jax.ShapeDtypeStruct((S, D_MODEL), F32),
        scratch_shapes=[pltpu.VMEM((4 * tq, LANES), BF16),
                        pltpu.VMEM((SUBLANES, 4 * tq), F32),
                        pltpu.VMEM((VT_ROWS, 4 * tq), F32)]
        + [pltpu.VMEM((tk, 4 * tq), F32)] * nbuf,
        compiler_params=_cparams("parallel", "parallel"),
        name="gqa_attention",
    )(qk, qk, vt)


def _proj_c_kernel(x_ref, w_ref, cos_ref, sin_ref, o_ref, xp_ref, slab_ref, *, dil):
    kind = pl.program_id(1)
    tm, d = x_ref.shape[0], o_ref.shape[-1]
    n = tm // dil
    nslab = d // LANES
    cb_w = 2 * LANES

    @pl.when(kind == 0)
    def _():
        if dil == 1:
            xp_ref[...] = x_ref[...].astype(BF16)
            return
        for s in range(nslab):
            slab_ref[s] = x_ref[:, s * LANES:(s + 1) * LANES]
        for c in range(dil):
            for s in range(nslab):
                xp_ref[c * n:(c + 1) * n, s * LANES:(s + 1) * LANES] = (
                    slab_ref[s, pl.ds(c, n, stride=dil), :].astype(BF16))

    def emit(c0, vals):
        o_ref[0, 0, :, :, c0:c0 + vals.shape[1]] = vals.astype(BF16).reshape(dil, n, vals.shape[1])

    @pl.when(kind < 2)
    def _():
        scale = jnp.where(kind == 0, ATTN_SCALE * LOG2_E, 1.0).astype(F32)
        cos_t = cos_ref[...] * scale
        sin_t = sin_ref[...] * scale
        ncb = d // cb_w
        r_next = _dot(xp_ref[...], w_ref[:, :cb_w])
        for cb in range(ncb):
            r = r_next
            if cb + 1 < ncb:
                r_next = _dot(xp_ref[...], w_ref[:, (cb + 1) * cb_w:(cb + 2) * cb_w])
            for s in range(cb_w // LANES):
                rt = r[:, s * LANES:(s + 1) * LANES]
                emit(cb * cb_w + s * LANES, rt * cos_t + _rotate_partner(rt, HEAD_DIM // 2) * sin_t)

    @pl.when(kind == 2)
    def _():
        for cb in range(d // cb_w):
            emit(cb * cb_w, _dot(xp_ref[...], w_ref[:, cb * cb_w:(cb + 1) * cb_w]))


def _regroup_rows(tab, dil):
    S, w = tab.shape
    return tab.reshape(S // DIL_CHUNK, DIL_CHUNK // dil, dil, w).transpose(0, 2, 1, 3).reshape(S, w)


def _proj_c(h, w, cos_np, sin_np, group, dil):
    S = h.shape[0]
    tm = DIL_CHUNK
    n = tm // dil
    ngroups = len(DIL_DILATIONS)
    cos_t = jnp.asarray(_regroup_rows(cos_np, dil), F32)
    sin_t = jnp.asarray(_regroup_rows(sin_np, dil), F32)
    return pl.pallas_call(
        functools.partial(_proj_c_kernel, dil=dil),
        grid=(S // tm, 3),
        in_specs=[pl.BlockSpec((tm, D_MODEL), lambda i, t: (i, 0)),
                  pl.BlockSpec((D_MODEL, D_MODEL), lambda i, t: (0, ngroups * t + group)),
                  pl.BlockSpec((tm, LANES), lambda i, t: (i, 0)),
                  pl.BlockSpec((tm, LANES), lambda i, t: (i, 0))],
        out_specs=pl.BlockSpec((1, 1, dil, n, D_MODEL), lambda i, t: (t, i, 0, 0, 0)),
        out_shape=jax.ShapeDtypeStruct((3, S // tm, dil, n, D_MODEL), BF16),
        scratch_shapes=[pltpu.VMEM((tm, D_MODEL), BF16),
                        pltpu.VMEM((D_MODEL // LANES, tm, LANES), F32)],
        compiler_params=_cparams("parallel", "arbitrary"),
        name=f"proj_c_dil{dil}",
    )(h, w, cos_t, sin_t)


def _dil_kernel(*refs, nchunks):
    ng = len(DIL_DILATIONS)
    o_ref, osc_ref, lsc_ref = refs[7 * ng:]
    i = pl.program_id(0)
    lo = _lane_lo()

    groups, blocks = [], []
    for g, dil in enumerate(DIL_DILATIONS):
        n = DIL_CHUNK // dil
        bq = min(n, 2 * DIL_SIDE)
        nk = bq + 2 * DIL_SIDE
        qi = lax.broadcasted_iota(jnp.int32, (2 * bq, nk), 0) % bq
        kj = lax.broadcasted_iota(jnp.int32, (2 * bq, nk), 1)
        band = jnp.where(jnp.abs(kj - DIL_SIDE - qi) <= DIL_SIDE, 0.0, MASK_VALUE).astype(F32)
        groups.append(dict(dil=dil, n=n, bq=bq, nk=nk, nsub=n // bq, kj=kj, band=band,
                           kpad=-(n + 2 * DIL_SIDE) % LANES, refs=refs[7 * g:7 * g + 7]))
        blocks += [(g, c, b) for c in range(dil) for b in range(n // bq)]

    class_kv = {}

    def keys_values(g, c):
        if (g, c) not in class_kv:
            G = groups[g]
            _, kc_ref, kp_ref, kn_ref, vc_ref, vp_ref, vn_ref = G["refs"]
            kparts = [kp_ref[0, 0, c], kc_ref[0, 0, c], kn_ref[0, 0, c]]
            if G["kpad"]:
                kparts.append(jnp.zeros((G["kpad"], LANES), BF16))
            kcat_t = jnp.concatenate(kparts, axis=0).astype(F32).T.astype(BF16)
            vcat = jnp.concatenate([vp_ref[0, 0, c], vc_ref[0, 0, c], vn_ref[0, 0, c]], axis=0)
            class_kv[(g, c)] = (kcat_t, vcat)
        return class_kv[(g, c)]

    def scores(blk):
        g, c, b = blk
        G = groups[g]
        bq, nk, kj = G["bq"], G["nk"], G["kj"]
        p0 = b * bq
        bias = G["band"]
        if b == 0:
            bias = jnp.where(kj >= jnp.where(i == 0, DIL_SIDE, 0), bias, MASK_VALUE)
        if b == G["nsub"] - 1:
            bias = jnp.where(kj < jnp.where(i == nchunks - 1, bq + DIL_SIDE, nk), bias, MASK_VALUE)
        qf = G["refs"][0][0, 0, c, p0:p0 + bq, :].astype(F32)
        qs = jnp.concatenate([jnp.where(lo, qf, 0.0), jnp.where(lo, 0.0, qf)], axis=0)
        return _dot(qs.astype(BF16), keys_values(g, c)[0][:, p0:p0 + nk]) + bias

    def finish(blk, s):
        g, c, b = blk
        G = groups[g]
        bq, nk, dil = G["bq"], G["nk"], G["dil"]
        p0 = b * bq
        m = jnp.max(s, axis=1, keepdims=True)
        p = jnp.exp2(s - m)
        l = jnp.sum(p, axis=1, keepdims=True)
        o2 = _dot(p.astype(BF16), keys_values(g, c)[1][p0:p0 + nk]) / l
        lse2 = m + jnp.log2(l)
        o_pair = jnp.where(lo, o2[:bq], o2[bq:])
        l_pair = jnp.where(lo, lse2[:bq], lse2[bq:])
        rows = pl.ds(p0, bq) if dil == 1 else pl.ds(p0 * dil + c, bq, stride=dil)
        osc_ref[g, rows, :] = o_pair
        lsc_ref[g, rows, :] = l_pair

    s_next = scores(blocks[0])
    for idx, blk in enumerate(blocks):
        s_cur = s_next
        if idx + 1 < len(blocks):
            s_next = scores(blocks[idx + 1])
        finish(blk, s_cur)

    ls = [lsc_ref[g] for g in range(ng)]
    mx = functools.reduce(jnp.maximum, ls)
    ws = [jnp.exp2(l - mx) for l in ls]
    num = sum(w * osc_ref[g] for g, w in enumerate(ws))
    o_ref[...] = num / sum(ws)


def _dilated_attention(qkvs):
    nchunks = qkvs[0].shape[1]
    S = nchunks * DIL_CHUNK
    in_specs, args = [], []
    for dil, a in zip(DIL_DILATIONS, qkvs):
        n = DIL_CHUNK // dil
        nblk = n // DIL_SIDE
        full = (1, 1, dil, n, LANES)
        halo = (1, 1, dil, DIL_SIDE, LANES)

        def cur(t):
            return lambda i, m: (t, i, 0, 0, m)

        def prev(t, nblk=nblk):
            return lambda i, m: (t, jnp.maximum(i - 1, 0), 0, nblk - 1, m)

        def nxt(t):
            return lambda i, m: (t, jnp.minimum(i + 1, nchunks - 1), 0, 0, m)

        in_specs += [pl.BlockSpec(full, cur(0)),
                     pl.BlockSpec(full, cur(1)), pl.BlockSpec(halo, prev(1)), pl.BlockSpec(halo, nxt(1)),
                     pl.BlockSpec(full, cur(2)), pl.BlockSpec(halo, prev(2)), pl.BlockSpec(halo, nxt(2))]
        args += [a] * 7
    ng = len(DIL_DILATIONS)
    return pl.pallas_call(
        functools.partial(_dil_kernel, nchunks=nchunks),
        grid=(nchunks, D_MODEL // LANES),
        in_specs=in_specs,
        out_specs=pl.BlockSpec((DIL_CHUNK, LANES), lambda i, m: (i, m)),
        out_shape=jax.ShapeDtypeStruct((S, D_MODEL), F32),
        scratch_shapes=[pltpu.VMEM((ng, DIL_CHUNK, LANES), F32),
                        pltpu.VMEM((ng, DIL_CHUNK, LANES), F32)],
        compiler_params=_cparams("parallel", "parallel"),
        name="dilated_attention",
    )(*args)


def _epilogue_kernel(br_ref, h_ref, wt_ref, mk_ref, mv_ref, wo_ref, g_ref, b_ref, o_ref):
    lo = _lane_lo()
    hb = h_ref[...].astype(BF16)
    gate_b = _dot(hb, wt_ref[:, :D_MODEL])
    yb = (br_ref[...] * (gate_b * jax.nn.sigmoid(gate_b))).astype(BF16)
    y = _dot(yb, wo_ref[:D_MODEL, :])
    tail_m = _dot(hb, wt_ref[:, D_MODEL:])
    for t in range(MEM_WIDTH // LANES):
        cols = slice(t * LANES, (t + 1) * LANES)
        qf = tail_m[:, MEM_WIDTH + t * LANES:MEM_WIDTH + (t + 1) * LANES] * ATTN_SCALE
        kt, vt = mk_ref[:, cols], mv_ref[:, cols]
        outs = []
        for qh in (jnp.where(lo, qf, 0.0), jnp.where(lo, 0.0, qf)):
            s = _dot_nt(qh.astype(BF16), kt)
            m = jnp.max(s, axis=1, keepdims=True)
            p = jnp.exp(s - m)
            l = jnp.sum(p, axis=1, keepdims=True)
            outs.append(_dot(p.astype(BF16), vt) / l)
        mem_out = jnp.where(lo, outs[0], outs[1])
        gm = tail_m[:, cols]
        ym = (mem_out * (gm * jax.nn.sigmoid(gm))).astype(BF16)
        y = y + _dot(ym, wo_ref[D_MODEL + t * LANES:D_MODEL + (t + 1) * LANES, :])
    z = DEEPNORM_ALPHA * h_ref[...] + y
    o_ref[...] = _layer_norm_rows(z, g_ref[...], b_ref[...])


def _epilogue(branch, h, w_tail, mk, mv, w_out, g, b, tr=512):
    S = h.shape[0]
    row = lambda i: (i, 0)
    fixed = lambda i: (0, 0)
    return pl.pallas_call(
        _epilogue_kernel,
        grid=(S // tr,),
        in_specs=[pl.BlockSpec((tr, D_MODEL), row),
                  pl.BlockSpec((tr, D_MODEL), row),
                  pl.BlockSpec((D_MODEL, TAIL), fixed),
                  pl.BlockSpec((N_MEM, MEM_WIDTH), fixed),
                  pl.BlockSpec((N_MEM, MEM_WIDTH), fixed),
                  pl.BlockSpec((INNER, D_MODEL), fixed),
                  pl.BlockSpec((1, D_MODEL), fixed),
                  pl.BlockSpec((1, D_MODEL), fixed)],
        out_specs=pl.BlockSpec((tr, D_MODEL), row),
        out_shape=jax.ShapeDtypeStruct((S, D_MODEL), F32),
        compiler_params=_cparams("parallel"),
        name="epilogue",
    )(branch, h, w_tail, mk, mv, w_out, g.reshape(1, -1), b.reshape(1, -1))


def _dup_kv_columns(w):
    d = w.shape[0]
    w4 = w.reshape(d, GQA_KV_HEADS, 1, HEAD_DIM)
    return jnp.broadcast_to(w4, (d, GQA_KV_HEADS, 2, HEAD_DIM)).reshape(d, 2 * GQA_KV_HEADS * HEAD_DIM)


def kernel(x, mem, ln_in_g, ln_in_b, w_mem_kv, w_in_a, w_in_b, q_norm_g, k_norm_g, w_in_c, w_out, ln_g, ln_b):
    B, S, D = x.shape
    assert B == 1 and D == D_MODEL and S % DIL_CHUNK == 0 and S % (FFT_N2 * SUBLANES) == 0

    mkv = _proj(mem[0], w_mem_kv.astype(BF16), name="proj_mem").astype(BF16)
    mk, mv = mkv[:, :MEM_WIDTH], mkv[:, MEM_WIDTH:]
    h = _ln_in(x[0], ln_in_g, ln_in_b)
    fft_tabs = None
    qw = GQA_Q_HEADS * HEAD_DIM
    kvw = GQA_KV_HEADS * HEAD_DIM

    for i in range(DEPTH):
        kind, j = i % 3, i // 3
        if kind == 0:
            if fft_tabs is None:
                fft_tabs = _fft_tables(S)
            fc, gk, hm = fft_tabs
            w_tail = w_in_a[j].astype(BF16)
            branch = _fft_stage2(_fft_stage1(_proj_a(h, fc), gk), hm)
        elif kind == 1:
            w = w_in_b[j]
            w_qkv = jnp.concatenate([w[:, :qw], _dup_kv_columns(w[:, qw:qw + kvw]),
                                     w[:, qw + kvw:qw + 2 * kvw]], axis=1).astype(BF16)
            cos_t, sin_t = (jnp.asarray(t, F32) for t in _rope_tables_axial(S))
            gq = jnp.tile(q_norm_g[j], 2).reshape(1, LANES)
            gkn = jnp.tile(k_norm_g[j], 2).reshape(1, LANES)
            qk, vt = _proj_b(h, w_qkv, cos_t, sin_t, gq, gkn)
            w_tail = w[:, qw + 2 * kvw:].astype(BF16)
            branch = _gqa_attention(qk, vt)
        else:
            w = w_in_c[j].astype(BF16)
            cos_t, sin_t = _rope_tables_1d(S)
            qkvs = [_proj_c(h, w, cos_t, sin_t, g, dil) for g, dil in enumerate(DIL_DILATIONS)]
            w_tail = w[:, w.shape[1] - TAIL:]
            branch = _dilated_attention(qkvs)
        h = _epilogue(branch, h, w_tail, mk, mv, w_out[i].astype(BF16), ln_g[i], ln_b[i])
    return h[None]
```

```python
import functools

import jax
import jax.numpy as jnp
import numpy as np
from jax import lax
from jax.experimental import pallas as pl
from jax.experimental.pallas import tpu as pltpu

F32 = jnp.float32
BF16 = jnp.bfloat16

D_MODEL = 1024
DEPTH = 4
N_MEM = 256
GRID_W = 64
HEAD_DIM = 64
ROPE_THETA = 10000.0
LN_EPS = 1e-5
RMS_EPS = 1e-6
MASK_VALUE = -1e30
FNET_GROUPS = 4
FNET_GROUP_W = D_MODEL // FNET_GROUPS
GQA_Q_HEADS = 16
GQA_KV_HEADS = 4
DIL_DILATIONS = (1, 4, 16)
DIL_SIDE = 64
MEM_WIDTH = 256
INNER = D_MODEL + MEM_WIDTH
TAIL = INNER + MEM_WIDTH
DEEPNORM_ALPHA = (2.0 * DEPTH) ** 0.25
ATTN_SCALE = HEAD_DIM ** -0.5
LOG2_E = float(np.log2(np.e))
VT_ROWS = HEAD_DIM + 16
MAX_REF_LAG = 100.0
LANES = 128
SUBLANES = 8
FFT_N2 = 256
DIL_CHUNK = 1024
VMEM_LIMIT = 48 << 20


def _cparams(*sem):
    return pltpu.CompilerParams(dimension_semantics=sem, vmem_limit_bytes=VMEM_LIMIT)


def _dot(a, b):
    return jnp.dot(a, b, preferred_element_type=F32)


def _dot_nt(a, b):
    return lax.dot_general(a, b, (((1,), (1,)), ((), ())), preferred_element_type=F32)


def _lane_lo(shape=(1, LANES)):
    return lax.broadcasted_iota(jnp.int32, shape, len(shape) - 1) % LANES < HEAD_DIM


def _layer_norm_rows(z, g, b):
    mu = jnp.mean(z, axis=-1, keepdims=True)
    zc = z - mu
    var = jnp.mean(zc * zc, axis=-1, keepdims=True)
    return zc * lax.rsqrt(var + LN_EPS) * g + b


def _ln_kernel(x_ref, g_ref, b_ref, o_ref):
    o_ref[...] = _layer_norm_rows(x_ref[...], g_ref[...], b_ref[...])


def _ln_in(x, g, b, tr=512):
    S = x.shape[0]
    return pl.pallas_call(
        _ln_kernel,
        grid=(S // tr,),
        in_specs=[pl.BlockSpec((tr, D_MODEL), lambda i: (i, 0)),
                  pl.BlockSpec((1, D_MODEL), lambda i: (0, 0)),
                  pl.BlockSpec((1, D_MODEL), lambda i: (0, 0))],
        out_specs=pl.BlockSpec((tr, D_MODEL), lambda i: (i, 0)),
        out_shape=jax.ShapeDtypeStruct((S, D_MODEL), F32),
        compiler_params=_cparams("parallel"),
        name="ln_in",
    )(x, g.reshape(1, -1), b.reshape(1, -1))


def _proj_kernel(x_ref, w_ref, o_ref):
    o_ref[...] = _dot(x_ref[...].astype(BF16), w_ref[...])


def _proj(x, w, tr=512, name="proj"):
    S, K = x.shape
    N = w.shape[1]
    tr = min(tr, S)
    return pl.pallas_call(
        _proj_kernel,
        grid=(S // tr,),
        in_specs=[pl.BlockSpec((tr, K), lambda i: (i, 0)),
                  pl.BlockSpec((K, N), lambda i: (0, 0))],
        out_specs=pl.BlockSpec((tr, N), lambda i: (i, 0)),
        out_shape=jax.ShapeDtypeStruct((S, N), F32),
        compiler_params=_cparams("parallel"),
        name=name,
    )(x, w)


def _proj_a_kernel(x_ref, fc_ref, z_ref):
    xb = x_ref[...].astype(BF16)
    for g in range(FNET_GROUPS):
        cols = slice(g * FNET_GROUP_W, (g + 1) * FNET_GROUP_W)
        zg = _dot(xb[:, cols], fc_ref[...])
        z_ref[0, :, cols] = zg[:, :FNET_GROUP_W]
        z_ref[1, :, cols] = zg[:, FNET_GROUP_W:]


def _proj_a(h, fc, tr=512):
    S = h.shape[0]
    return pl.pallas_call(
        _proj_a_kernel,
        grid=(S // tr,),
        in_specs=[pl.BlockSpec((tr, D_MODEL), lambda i: (i, 0)),
                  pl.BlockSpec((FNET_GROUP_W, 2 * FNET_GROUP_W), lambda i: (0, 0))],
        out_specs=pl.BlockSpec((2, tr, D_MODEL), lambda i: (0, i, 0)),
        out_shape=jax.ShapeDtypeStruct((2, S, D_MODEL), F32),
        compiler_params=_cparams("parallel"),
        name="proj_a",
    )(h, fc)


def _fft1_kernel(z_ref, g_ref, t_ref):
    two, n1, _, sub, tc = z_ref.shape
    x = z_ref[...].reshape(two * n1 * sub, tc).astype(BF16)
    t_ref[...] = _dot(g_ref[...], x).reshape(t_ref.shape)


def _fft_stage1(z, gk, tc=1024):
    _, S, D = z.shape
    n1 = S // FFT_N2
    nu = FFT_N2 // SUBLANES
    z5 = z.reshape(2, n1, nu, SUBLANES, D)
    rows = 2 * n1 * SUBLANES
    t5 = pl.pallas_call(
        _fft1_kernel,
        grid=(nu, D // tc),
        in_specs=[pl.BlockSpec((2, n1, 1, SUBLANES, tc), lambda u, c: (0, 0, u, 0, c)),
                  pl.BlockSpec((rows, rows), lambda u, c: (0, 0))],
        out_specs=pl.BlockSpec((n1, 2, 1, SUBLANES, tc), lambda u, c: (0, 0, u, 0, c)),
        out_shape=jax.ShapeDtypeStruct((n1, 2, nu, SUBLANES, D), F32),
        compiler_params=_cparams("parallel", "parallel"),
        name="fft_stage1",
    )(z5, gk)
    return t5.reshape(n1, 2, FFT_N2, D)


def _fft2_kernel(t_ref, h_ref, o_ref, slab_ref):
    nj, _, n2, d = t_ref.shape
    nslab = d // LANES
    for j in range(nj):
        tj = t_ref[j].reshape(2 * n2, d).astype(BF16)
        r = _dot(h_ref[j], tj)
        for s in range(nslab):
            slab_ref[s, pl.ds(j, n2, stride=nj), :] = r[:, s * LANES:(s + 1) * LANES]
    for s in range(nslab):
        o_ref[:, :, s * LANES:(s + 1) * LANES] = slab_ref[s].reshape(n2, nj, LANES)


def _fft_stage2(t, hmat, tc=512):
    n1, _, n2, D = t.shape
    nj = SUBLANES
    y3 = pl.pallas_call(
        _fft2_kernel,
        grid=(n1 // nj, D // tc),
        in_specs=[pl.BlockSpec((nj, 2, n2, tc), lambda a, c: (a, 0, 0, c)),
                  pl.BlockSpec((nj, n2, 2 * n2), lambda a, c: (a, 0, 0))],
        out_specs=pl.BlockSpec((n2, nj, tc), lambda a, c: (0, a, c)),
        out_shape=jax.ShapeDtypeStruct((n2, n1, D), F32),
        scratch_shapes=[pltpu.VMEM((tc // LANES, n2 * nj, LANES), F32)],
        compiler_params=_cparams("parallel", "parallel"),
        name="fft_stage2",
    )(t, hmat)
    return y3.reshape(n2 * n1, D)


def _fft_tables(S):
    n1, n2 = S // FFT_N2, FFT_N2
    c = np.arange(FNET_GROUP_W)
    ang = (2.0 * np.pi / FNET_GROUP_W) * ((c[:, None] * c[None, :]) % FNET_GROUP_W)
    scale = 1.0 / np.sqrt(float(S) * FNET_GROUP_W)
    fc = np.concatenate([np.cos(ang), -np.sin(ang)], axis=1) * scale
    k1 = np.arange(n1)
    th = (2.0 * np.pi / n1) * ((k1[:, None] * k1[None, :]) % n1)
    cs, sn = np.cos(th), np.sin(th)
    g = np.stack([np.stack([cs, sn], axis=1), np.stack([-sn, cs], axis=1)], axis=1)
    gk = np.kron(g.reshape(2 * n1, 2 * n1), np.eye(SUBLANES))
    k2 = np.arange(n2)
    kk = k1[:, None, None] + n1 * k2[None, :, None]
    ph = (2.0 * np.pi / S) * ((k2[None, None, :] * kk) % S)
    hm = np.concatenate([np.cos(ph), np.sin(ph)], axis=2)
    return tuple(jnp.asarray(t, F32).astype(BF16) for t in (fc, gk, hm))


def _rope_angles(pos, dim):
    inv_freq = ROPE_THETA ** (-(np.arange(0, dim, 2, dtype=np.float64) / dim))
    return pos.astype(np.float64)[:, None] * inv_freq[None, :]


def _rope_tables_axial(S):
    t = np.arange(S)
    ar = _rope_angles(t // GRID_W, HEAD_DIM // 2)
    ac = _rope_angles(t % GRID_W, HEAD_DIM // 2)
    cos = np.concatenate([np.cos(ar), np.cos(ar), np.cos(ac), np.cos(ac)], axis=1)
    sin = np.concatenate([-np.sin(ar), np.sin(ar), -np.sin(ac), np.sin(ac)], axis=1)
    return np.tile(cos, (1, 2)), np.tile(sin, (1, 2))


def _rope_tables_1d(S):
    a = _rope_angles(np.arange(S), HEAD_DIM)
    cos = np.concatenate([np.cos(a), np.cos(a)], axis=1)
    sin = np.concatenate([-np.sin(a), np.sin(a)], axis=1)
    return np.tile(cos, (1, 2)), np.tile(sin, (1, 2))


def _rotate_partner(x, half):
    lane = lax.broadcasted_iota(jnp.int32, (1, LANES), 1)
    first = lane % (2 * half) < half
    return jnp.where(first, pltpu.roll(x, LANES - half, 1), pltpu.roll(x, half, 1))


def _proj_b_kernel(x_ref, w_ref, cos_ref, sin_ref, gq_ref, gk_ref, o_ref, vt_ref):
    xb = x_ref[...].astype(BF16)
    cos_t, sin_t = cos_ref[...], sin_ref[...]
    lo = _lane_lo()
    nb_w = 4 * LANES
    vt = _dot(xb, w_ref[:, 3 * nb_w:]).T.astype(BF16)
    ones = jnp.ones((VT_ROWS - HEAD_DIM, vt.shape[1]), BF16)
    for hd in range(GQA_KV_HEADS):
        vt_ref[hd * VT_ROWS:hd * VT_ROWS + HEAD_DIM, :] = vt[hd * HEAD_DIM:(hd + 1) * HEAD_DIM]
        vt_ref[hd * VT_ROWS + HEAD_DIM:(hd + 1) * VT_ROWS, :] = ones
    for nb in range(3):
        r = _dot(xb, w_ref[:, nb * nb_w:(nb + 1) * nb_w])
        gain = gq_ref[...] if nb < 2 else gk_ref[...]
        scale = ATTN_SCALE * LOG2_E if nb < 2 else 1.0
        for t in range(4):
            rt = r[:, t * LANES:(t + 1) * LANES]
            r2 = rt * rt
            tot = jnp.sum(r2, axis=1, keepdims=True)
            low = jnp.sum(jnp.where(lo, r2, 0.0), axis=1, keepdims=True)
            ss = jnp.where(lo, low, tot - low)
            xn = rt * lax.rsqrt(ss * (1.0 / HEAD_DIM) + RMS_EPS) * gain
            out = (xn * cos_t + _rotate_partner(xn, HEAD_DIM // 4) * sin_t) * scale
            c0 = nb * nb_w + t * LANES
            o_ref[:, c0:c0 + LANES] = out.astype(BF16)


def _proj_b(h, w_qkv, cos_t, sin_t, gq, gk, tr=512):
    S = h.shape[0]
    n_in = w_qkv.shape[1]
    n_out = n_in - GQA_KV_HEADS * HEAD_DIM
    vw = GQA_KV_HEADS * VT_ROWS
    return pl.pallas_call(
        _proj_b_kernel,
        grid=(S // tr,),
        in_specs=[pl.BlockSpec((tr, D_MODEL), lambda i: (i, 0)),
                  pl.BlockSpec((D_MODEL, n_in), lambda i: (0, 0)),
                  pl.BlockSpec((tr, LANES), lambda i: (i, 0)),
                  pl.BlockSpec((tr, LANES), lambda i: (i, 0)),
                  pl.BlockSpec((1, LANES), lambda i: (0, 0)),
                  pl.BlockSpec((1, LANES), lambda i: (0, 0))],
        out_specs=[pl.BlockSpec((tr, n_out), lambda i: (i, 0)),
                   pl.BlockSpec((vw, tr), lambda i: (0, i))],
        out_shape=[jax.ShapeDtypeStruct((S, n_out), BF16),
                   jax.ShapeDtypeStruct((vw, S), BF16)],
        compiler_params=_cparams("parallel"),
        name="proj_b",
    )(h, w_qkv, cos_t, sin_t, gq, gk)


def _gqa_kernel(q_ref, k_ref, vt_ref, o_ref, qs_ref, m_ref, acc_ref, *p_refs, tk, depth):
    tq = q_ref.shape[0]
    S = k_ref.shape[0]
    nq = 4 * tq
    lo = _lane_lo()
    for t in range(2):
        qt = q_ref[:, t * LANES:(t + 1) * LANES].astype(F32)
        qs_ref[(2 * t) * tq:(2 * t + 1) * tq, :] = jnp.where(lo, qt, 0.0).astype(BF16)
        qs_ref[(2 * t + 1) * tq:(2 * t + 2) * tq, :] = jnp.where(lo, 0.0, qt).astype(BF16)
    nkv = S // tk
    nbuf = len(p_refs)

    def scores(j):
        k0 = pl.multiple_of(j * tk, tk)
        return _dot_nt(k_ref[pl.ds(k0, tk), :], qs_ref[...])

    def values(j, p):
        k0 = pl.multiple_of(j * tk, tk)
        return _dot(vt_ref[:, pl.ds(k0, tk)], p)

    def colmax(s):
        return jnp.max(s.reshape(tk // SUBLANES, SUBLANES, nq), axis=0)

    def all_rows(x):
        return jnp.broadcast_to(jnp.max(x, axis=0, keepdims=True), x.shape)

    def rows(x, n):
        return jnp.tile(x, (n // SUBLANES, 1))

    m_ref[...] = all_rows(colmax(scores(0)))
    acc_ref[...] = jnp.zeros(acc_ref.shape, F32)

    def trip(jj, carry):
        j0 = depth * jj
        m_old = m_ref[...]
        m_old_t = rows(m_old, tk)

        def probs(u):
            s = scores(j0 + u)
            p_refs[u % nbuf][...] = jnp.exp2(s - m_old_t).astype(BF16)
            return colmax(s)

        tmax = jnp.maximum(m_old, probs(0))
        pv = None
        for u in range(depth):
            if u + 1 < depth:
                tmax = jnp.maximum(tmax, probs(u + 1))
            d = values(j0 + u, p_refs[u % nbuf][...])
            pv = d if pv is None else pv + d
        m_new = all_rows(tmax)
        safe = jnp.max(m_new - m_old) <= MAX_REF_LAG

        @pl.when(safe)
        def _():
            acc_ref[...] = rows(jnp.exp2(m_old - m_new), VT_ROWS) * (acc_ref[...] + pv)
            m_ref[...] = m_new

        @pl.when(jnp.logical_not(safe))
        def _():
            def exact_tile(u, c):
                s = scores(j0 + u)
                m_prev = m_ref[...]
                m_cur = jnp.maximum(m_prev, all_rows(colmax(s)))
                d = values(j0 + u, jnp.exp2(s - rows(m_cur, tk)).astype(BF16))
                acc_ref[...] = rows(jnp.exp2(m_prev - m_cur), VT_ROWS) * acc_ref[...] + d
                m_ref[...] = m_cur
                return c
            lax.fori_loop(0, depth, exact_tile, 0)

        return carry

    lax.fori_loop(0, nkv // depth, trip, 0)
    acc = acc_ref[...]
    ot = acc[:HEAD_DIM] / acc[HEAD_DIM:HEAD_DIM + 1]
    for t in range(2):
        pair = jnp.concatenate([ot[:, (2 * t) * tq:(2 * t + 1) * tq],
                                ot[:, (2 * t + 1) * tq:(2 * t + 2) * tq]], axis=0)
        o_ref[:, t * LANES:(t + 1) * LANES] = pair.T


def _gqa_attention(qk, vt, tq=128, tk=512, depth=8, nbuf=2):
    S = qk.shape[0]
    qw = 4 * HEAD_DIM
    k_blk0 = GQA_Q_HEADS * HEAD_DIM // LANES
    tk = min(tk, S)
    depth = min(depth, S // tk)
    assert (S // tk) % depth == 0
    return pl.pallas_call(
        functools.partial(_gqa_kernel, tk=tk, depth=depth),
        grid=(GQA_KV_HEADS, S // tq),
        in_specs=[pl.BlockSpec((tq, qw), lambda h, i: (i, h)),
                  pl.BlockSpec((S, LANES), lambda h, i: (0, k_blk0 + h)),
                  pl.BlockSpec((VT_ROWS, S), lambda h, i: (h, 0))],
        out_specs=pl.BlockSpec((tq, qw), lambda h, i: (i, h)),
        out_shape=jax.ShapeDtypeStruct((S, D_MODEL), F32),
        scratch_shapes=[pltpu.VMEM((4 * tq, LANES), BF16),
                        pltpu.VMEM((SUBLANES, 4 * tq), F32),
                        pltpu.VMEM((VT_ROWS, 4 * tq), F32)]
        + [pltpu.VMEM((tk, 4 * tq), BF16)] * nbuf,
        compiler_params=_cparams("parallel", "parallel"),
        name="gqa_attention",
    )(qk, qk, vt)


def _proj_c_kernel(x_ref, w_ref, cos_ref, sin_ref, o_ref, xp_ref, slab_ref, *, dil):
    kind = pl.program_id(1)
    tm, d = x_ref.shape[0], o_ref.shape[-1]
    n = tm // dil
    nslab = d // LANES
    cb_w = 2 * LANES

    @pl.when(kind == 0)
    def _():
        if dil == 1:
            xp_ref[...] = x_ref[...].astype(BF16)
            return
        for s in range(nslab):
            slab_ref[s] = x_ref[:, s * LANES:(s + 1) * LANES]
        for c in range(dil):
            for s in range(nslab):
                xp_ref[c * n:(c + 1) * n, s * LANES:(s + 1) * LANES] = (
                    slab_ref[s, pl.ds(c, n, stride=dil), :].astype(BF16))

    def emit(c0, vals):
        o_ref[0, 0, :, :, c0:c0 + vals.shape[1]] = vals.astype(BF16).reshape(dil, n, vals.shape[1])

    @pl.when(kind < 2)
    def _():
        scale = jnp.where(kind == 0, ATTN_SCALE * LOG2_E, 1.0).astype(F32)
        cos_t = cos_ref[...] * scale
        sin_t = sin_ref[...] * scale
        ncb = d // cb_w
        r_next = _dot(xp_ref[...], w_ref[:, :cb_w])
        for cb in range(ncb):
            r = r_next
            if cb + 1 < ncb:
                r_next = _dot(xp_ref[...], w_ref[:, (cb + 1) * cb_w:(cb + 2) * cb_w])
            for s in range(cb_w // LANES):
                rt = r[:, s * LANES:(s + 1) * LANES]
                emit(cb * cb_w + s * LANES, rt * cos_t + _rotate_partner(rt, HEAD_DIM // 2) * sin_t)

    @pl.when(kind == 2)
    def _():
        for cb in range(d // cb_w):
            emit(cb * cb_w, _dot(xp_ref[...], w_ref[:, cb * cb_w:(cb + 1) * cb_w]))


def _regroup_rows(tab, dil):
    S, w = tab.shape
    return tab.reshape(S // DIL_CHUNK, DIL_CHUNK // dil, dil, w).transpose(0, 2, 1, 3).reshape(S, w)


def _proj_c(h, w, cos_np, sin_np, group, dil):
    S = h.shape[0]
    tm = DIL_CHUNK
    n = tm // dil
    ngroups = len(DIL_DILATIONS)
    cos_t = jnp.asarray(_regroup_rows(cos_np, dil), F32)
    sin_t = jnp.asarray(_regroup_rows(sin_np, dil), F32)
    return pl.pallas_call(
        functools.partial(_proj_c_kernel, dil=dil),
        grid=(S // tm, 3),
        in_specs=[pl.BlockSpec((tm, D_MODEL), lambda i, t: (i, 0)),
                  pl.BlockSpec((D_MODEL, D_MODEL), lambda i, t: (0, ngroups * t + group)),
                  pl.BlockSpec((tm, LANES), lambda i, t: (i, 0)),
                  pl.BlockSpec((tm, LANES), lambda i, t: (i, 0))],
        out_specs=pl.BlockSpec((1, 1, dil, n, D_MODEL), lambda i, t: (t, i, 0, 0, 0)),
        out_shape=jax.ShapeDtypeStruct((3, S // tm, dil, n, D_MODEL), BF16),
        scratch_shapes=[pltpu.VMEM((tm, D_MODEL), BF16),
                        pltpu.VMEM((D_MODEL // LANES, tm, LANES), F32)],
        compiler_params=_cparams("parallel", "arbitrary"),
        name=f"proj_c_dil{dil}",
    )(h, w, cos_t, sin_t)


def _dil_kernel(*refs, nchunks):
    ng = len(DIL_DILATIONS)
    o_ref, osc_ref, lsc_ref = refs[7 * ng:]
    i = pl.program_id(0)
    lo = _lane_lo()

    groups, blocks = [], []
    for g, dil in enumerate(DIL_DILATIONS):
        n = DIL_CHUNK // dil
        bq = min(n, 2 * DIL_SIDE)
        nk = bq + 2 * DIL_SIDE
        qi = lax.broadcasted_iota(jnp.int32, (2 * bq, nk), 0) % bq
        kj = lax.broadcasted_iota(jnp.int32, (2 * bq, nk), 1)
        band = jnp.where(jnp.abs(kj - DIL_SIDE - qi) <= DIL_SIDE, 0.0, MASK_VALUE).astype(F32)
        groups.append(dict(dil=dil, n=n, bq=bq, nk=nk, nsub=n // bq, kj=kj, band=band,
                           kpad=-(n + 2 * DIL_SIDE) % LANES, refs=refs[7 * g:7 * g + 7]))
        blocks += [(g, c, b) for c in range(dil) for b in range(n // bq)]

    class_kv = {}

    def keys_values(g, c):
        if (g, c) not in class_kv:
            G = groups[g]
            _, kc_ref, kp_ref, kn_ref, vc_ref, vp_ref, vn_ref = G["refs"]
            kparts = [kp_ref[0, 0, c], kc_ref[0, 0, c], kn_ref[0, 0, c]]
            if G["kpad"]:
                kparts.append(jnp.zeros((G["kpad"], LANES), BF16))
            kcat_t = jnp.concatenate(kparts, axis=0).astype(F32).T.astype(BF16)
            vcat = jnp.concatenate([vp_ref[0, 0, c], vc_ref[0, 0, c], vn_ref[0, 0, c]], axis=0)
            class_kv[(g, c)] = (kcat_t, vcat)
        return class_kv[(g, c)]

    def scores(blk):
        g, c, b = blk
        G = groups[g]
        bq, nk, kj = G["bq"], G["nk"], G["kj"]
        p0 = b * bq
        bias = G["band"]
        if b == 0:
            bias = jnp.where(kj >= jnp.where(i == 0, DIL_SIDE, 0), bias, MASK_VALUE)
        if b == G["nsub"] - 1:
            bias = jnp.where(kj < jnp.where(i == nchunks - 1, bq + DIL_SIDE, nk), bias, MASK_VALUE)
        qf = G["refs"][0][0, 0, c, p0:p0 + bq, :].astype(F32)
        qs = jnp.concatenate([jnp.where(lo, qf, 0.0), jnp.where(lo, 0.0, qf)], axis=0)
        return _dot(qs.astype(BF16), keys_values(g, c)[0][:, p0:p0 + nk]) + bias

    def finish(blk, s):
        g, c, b = blk
        G = groups[g]
        bq, nk, dil = G["bq"], G["nk"], G["dil"]
        p0 = b * bq
        m = jnp.max(s, axis=1, keepdims=True)
        p = jnp.exp2(s - m)
        l = jnp.sum(p, axis=1, keepdims=True)
        o2 = _dot(p.astype(BF16), keys_values(g, c)[1][p0:p0 + nk]) / l
        lse2 = m + jnp.log2(l)
        o_pair = jnp.where(lo, o2[:bq], o2[bq:])
        l_pair = jnp.where(lo, lse2[:bq], lse2[bq:])
        rows = pl.ds(p0, bq) if dil == 1 else pl.ds(p0 * dil + c, bq, stride=dil)
        osc_ref[g, rows, :] = o_pair
        lsc_ref[g, rows, :] = l_pair

    s_next = scores(blocks[0])
    for idx, blk in enumerate(blocks):
        s_cur = s_next
        if idx + 1 < len(blocks):
            s_next = scores(blocks[idx + 1])
        finish(blk, s_cur)

    ls = [lsc_ref[g] for g in range(ng)]
    mx = functools.reduce(jnp.maximum, ls)
    ws = [jnp.exp2(l - mx) for l in ls]
    num = sum(w * osc_ref[g] for g, w in enumerate(ws))
    o_ref[...] = num / sum(ws)


def _dilated_attention(qkvs):
    nchunks = qkvs[0].shape[1]
    S = nchunks * DIL_CHUNK
    in_specs, args = [], []
    for dil, a in zip(DIL_DILATIONS, qkvs):
        n = DIL_CHUNK // dil
        nblk = n // DIL_SIDE
        full = (1, 1, dil, n, LANES)
        halo = (1, 1, dil, DIL_SIDE, LANES)

        def cur(t):
            return lambda i, m: (t, i, 0, 0, m)

        def prev(t, nblk=nblk):
            return lambda i, m: (t, jnp.maximum(i - 1, 0), 0, nblk - 1, m)

        def nxt(t):
            return lambda i, m: (t, jnp.minimum(i + 1, nchunks - 1), 0, 0, m)

        in_specs += [pl.BlockSpec(full, cur(0)),
                     pl.BlockSpec(full, cur(1)), pl.BlockSpec(halo, prev(1)), pl.BlockSpec(halo, nxt(1)),
                     pl.BlockSpec(full, cur(2)), pl.BlockSpec(halo, prev(2)), pl.BlockSpec(halo, nxt(2))]
        args += [a] * 7
    ng = len(DIL_DILATIONS)
    return pl.pallas_call(
        functools.partial(_dil_kernel, nchunks=nchunks),
        grid=(nchunks, D_MODEL // LANES),
        in_specs=in_specs,
        out_specs=pl.BlockSpec((DIL_CHUNK, LANES), lambda i, m: (i, m)),
        out_shape=jax.ShapeDtypeStruct((S, D_MODEL), F32),
        scratch_shapes=[pltpu.VMEM((ng, DIL_CHUNK, LANES), F32),
                        pltpu.VMEM((ng, DIL_CHUNK, LANES), F32)],
        compiler_params=_cparams("parallel", "parallel"),
        name="dilated_attention",
    )(*args)


def _epilogue_kernel(br_ref, h_ref, wt_ref, mk_ref, mv_ref, wo_ref, g_ref, b_ref, o_ref):
    lo = _lane_lo()
    hb = h_ref[...].astype(BF16)
    gate_b = _dot(hb, wt_ref[:, :D_MODEL])
    yb = (br_ref[...] * (gate_b * jax.nn.sigmoid(gate_b))).astype(BF16)
    y = _dot(yb, wo_ref[:D_MODEL, :])
    tail_m = _dot(hb, wt_ref[:, D_MODEL:])
    for t in range(MEM_WIDTH // LANES):
        cols = slice(t * LANES, (t + 1) * LANES)
        qf = tail_m[:, MEM_WIDTH + t * LANES:MEM_WIDTH + (t + 1) * LANES] * ATTN_SCALE
        kt, vt = mk_ref[:, cols], mv_ref[:, cols]
        outs = []
        for qh in (jnp.where(lo, qf, 0.0), jnp.where(lo, 0.0, qf)):
            s = _dot_nt(qh.astype(BF16), kt)
            m = jnp.max(s, axis=1, keepdims=True)
            p = jnp.exp(s - m)
            l = jnp.sum(p, axis=1, keepdims=True)
            outs.append(_dot(p.astype(BF16), vt) / l)
        mem_out = jnp.where(lo, outs[0], outs[1])
        gm = tail_m[:, cols]
        ym = (mem_out * (gm * jax.nn.sigmoid(gm))).astype(BF16)
        y = y + _dot(ym, wo_ref[D_MODEL + t * LANES:D_MODEL + (t + 1) * LANES, :])
    z = DEEPNORM_ALPHA * h_ref[...] + y
    o_ref[...] = _layer_norm_rows(z, g_ref[...], b_ref[...])


def _epilogue(branch, h, w_tail, mk, mv, w_out, g, b, tr=512):
    S = h.shape[0]
    row = lambda i: (i, 0)
    fixed = lambda i: (0, 0)
    return pl.pallas_call(
        _epilogue_kernel,
        grid=(S // tr,),
        in_specs=[pl.BlockSpec((tr, D_MODEL), row),
                  pl.BlockSpec((tr, D_MODEL), row),
                  pl.BlockSpec((D_MODEL, TAIL), fixed),
                  pl.BlockSpec((N_MEM, MEM_WIDTH), fixed),
                  pl.BlockSpec((N_MEM, MEM_WIDTH), fixed),
                  pl.BlockSpec((INNER, D_MODEL), fixed),
                  pl.BlockSpec((1, D_MODEL), fixed),
                  pl.BlockSpec((1, D_MODEL), fixed)],
        out_specs=pl.BlockSpec((tr, D_MODEL), row),
        out_shape=jax.ShapeDtypeStruct((S, D_MODEL), F32),
        compiler_params=_cparams("parallel"),
        name="epilogue",
    )(branch, h, w_tail, mk, mv, w_out, g.reshape(1, -1), b.reshape(1, -1))


def _dup_kv_columns(w):
    d = w.shape[0]
    w4 = w.reshape(d, GQA_KV_HEADS, 1, HEAD_DIM)
    return jnp.broadcast_to(w4, (d, GQA_KV_HEADS, 2, HEAD_DIM)).reshape(d, 2 * GQA_KV_HEADS * HEAD_DIM)


def kernel(x, mem, ln_in_g, ln_in_b, w_mem_kv, w_in_a, w_in_b, q_norm_g, k_norm_g, w_in_c, w_out, ln_g, ln_b):
    B, S, D = x.shape
    assert B == 1 and D == D_MODEL and S % DIL_CHUNK == 0 and S % (FFT_N2 * SUBLANES) == 0

    mkv = _proj(mem[0], w_mem_kv.astype(BF16), name="proj_mem").astype(BF16)
    mk, mv = mkv[:, :MEM_WIDTH], mkv[:, MEM_WIDTH:]
    h = _ln_in(x[0], ln_in_g, ln_in_b)
    fft_tabs = None
    qw = GQA_Q_HEADS * HEAD_DIM
    kvw = GQA_KV_HEADS * HEAD_DIM

    for i in range(DEPTH):
        kind, j = i % 3, i // 3
        if kind == 0:
            if fft_tabs is None:
                fft_tabs = _fft_tables(S)
            fc, gk, hm = fft_tabs
            w_tail = w_in_a[j].astype(BF16)
            branch = _fft_stage2(_fft_stage1(_proj_a(h, fc), gk), hm)
        elif kind == 1:
            w = w_in_b[j]
            w_qkv = jnp.concatenate([w[:, :qw], _dup_kv_columns(w[:, qw:qw + kvw]),
                                     w[:, qw + kvw:qw + 2 * kvw]], axis=1).astype(BF16)
            cos_t, sin_t = (jnp.asarray(t, F32) for t in _rope_tables_axial(S))
            gq = jnp.tile(q_norm_g[j], 2).reshape(1, LANES)
            gkn = jnp.tile(k_norm_g[j], 2).reshape(1, LANES)
            qk, vt = _proj_b(h, w_qkv, cos_t, sin_t, gq, gkn)
            w_tail = w[:, qw + 2 * kvw:].astype(BF16)
            branch = _gqa_attention(qk, vt)
        else:
            w = w_in_c[j].astype(BF16)
            cos_t, sin_t = _rope_tables_1d(S)
            qkvs = [_proj_c(h, w, cos_t, sin_t, g, dil) for g, dil in enumerate(DIL_DILATIONS)]
            w_tail = w[:, w.shape[1] - TAIL:]
            branch = _dilated_attention(qkvs)
        h = _epilogue(branch, h, w_tail, mk, mv, w_out[i].astype(BF16), ln_g[i], ln_b[i])
    return h[None]
```

```python
import functools

import jax
import jax.numpy as jnp
import numpy as np
from jax import lax
from jax.experimental import pallas as pl
from jax.experimental.pallas import tpu as pltpu

F32 = jnp.float32
BF16 = jnp.bfloat16

D_MODEL = 1024
DEPTH = 4
N_MEM = 256
GRID_W = 64
HEAD_DIM = 64
ROPE_THETA = 10000.0
LN_EPS = 1e-5
RMS_EPS = 1e-6
MASK_VALUE = -1e30
FNET_GROUPS = 4
FNET_GROUP_W = D_MODEL // FNET_GROUPS
GQA_Q_HEADS = 16
GQA_KV_HEADS = 4
DIL_DILATIONS = (1, 4, 16)
DIL_SIDE = 64
MEM_WIDTH = 256
INNER = D_MODEL + MEM_WIDTH
TAIL = INNER + MEM_WIDTH
DEEPNORM_ALPHA = (2.0 * DEPTH) ** 0.25
ATTN_SCALE = HEAD_DIM ** -0.5
LOG2_E = float(np.log2(np.e))
VT_ROWS = HEAD_DIM + 16
MAX_REF_LAG = 100.0
LANES = 128
SUBLANES = 8
FFT_N2 = 256
DIL_CHUNK = 1024
VMEM_LIMIT = 48 << 20


def _cparams(*sem):
    return pltpu.CompilerParams(dimension_semantics=sem, vmem_limit_bytes=VMEM_LIMIT)


def _dot(a, b):
    return jnp.dot(a, b, preferred_element_type=F32)


def _dot_nt(a, b):
    return lax.dot_general(a, b, (((1,), (1,)), ((), ())), preferred_element_type=F32)


def _lane_lo(shape=(1, LANES)):
    return lax.broadcasted_iota(jnp.int32, shape, len(shape) - 1) % LANES < HEAD_DIM


def _layer_norm_rows(z, g, b):
    mu = jnp.mean(z, axis=-1, keepdims=True)
    zc = z - mu
    var = jnp.mean(zc * zc, axis=-1, keepdims=True)
    return zc * lax.rsqrt(var + LN_EPS) * g + b


def _ln_kernel(x_ref, g_ref, b_ref, o_ref):
    o_ref[...] = _layer_norm_rows(x_ref[...], g_ref[...], b_ref[...])


def _ln_in(x, g, b, tr=512):
    S = x.shape[0]
    return pl.pallas_call(
        _ln_kernel,
        grid=(S // tr,),
        in_specs=[pl.BlockSpec((tr, D_MODEL), lambda i: (i, 0)),
                  pl.BlockSpec((1, D_MODEL), lambda i: (0, 0)),
                  pl.BlockSpec((1, D_MODEL), lambda i: (0, 0))],
        out_specs=pl.BlockSpec((tr, D_MODEL), lambda i: (i, 0)),
        out_shape=jax.ShapeDtypeStruct((S, D_MODEL), F32),
        compiler_params=_cparams("parallel"),
        name="ln_in",
    )(x, g.reshape(1, -1), b.reshape(1, -1))


def _proj_kernel(x_ref, w_ref, o_ref):
    o_ref[...] = _dot(x_ref[...].astype(BF16), w_ref[...])


def _proj(x, w, tr=512, name="proj"):
    S, K = x.shape
    N = w.shape[1]
    tr = min(tr, S)
    return pl.pallas_call(
        _proj_kernel,
        grid=(S // tr,),
        in_specs=[pl.BlockSpec((tr, K), lambda i: (i, 0)),
                  pl.BlockSpec((K, N), lambda i: (0, 0))],
        out_specs=pl.BlockSpec((tr, N), lambda i: (i, 0)),
        out_shape=jax.ShapeDtypeStruct((S, N), F32),
        compiler_params=_cparams("parallel"),
        name=name,
    )(x, w)


def _proj_a_kernel(x_ref, fc_ref, z_ref):
    xb = x_ref[...].astype(BF16)
    for g in range(FNET_GROUPS):
        cols = slice(g * FNET_GROUP_W, (g + 1) * FNET_GROUP_W)
        zg = _dot(xb[:, cols], fc_ref[...])
        z_ref[0, :, cols] = zg[:, :FNET_GROUP_W]
        z_ref[1, :, cols] = zg[:, FNET_GROUP_W:]


def _proj_a(h, fc, tr=512):
    S = h.shape[0]
    return pl.pallas_call(
        _proj_a_kernel,
        grid=(S // tr,),
        in_specs=[pl.BlockSpec((tr, D_MODEL), lambda i: (i, 0)),
                  pl.BlockSpec((FNET_GROUP_W, 2 * FNET_GROUP_W), lambda i: (0, 0))],
        out_specs=pl.BlockSpec((2, tr, D_MODEL), lambda i: (0, i, 0)),
        out_shape=jax.ShapeDtypeStruct((2, S, D_MODEL), F32),
        compiler_params=_cparams("parallel"),
        name="proj_a",
    )(h, fc)


def _fft1_kernel(z_ref, g_ref, t_ref):
    two, n1, _, sub, tc = z_ref.shape
    x = z_ref[...].reshape(two * n1 * sub, tc).astype(BF16)
    t_ref[...] = _dot(g_ref[...], x).reshape(t_ref.shape)


def _fft_stage1(z, gk, tc=1024):
    _, S, D = z.shape
    n1 = S // FFT_N2
    nu = FFT_N2 // SUBLANES
    z5 = z.reshape(2, n1, nu, SUBLANES, D)
    rows = 2 * n1 * SUBLANES
    t5 = pl.pallas_call(
        _fft1_kernel,
        grid=(nu, D // tc),
        in_specs=[pl.BlockSpec((2, n1, 1, SUBLANES, tc), lambda u, c: (0, 0, u, 0, c)),
                  pl.BlockSpec((rows, rows), lambda u, c: (0, 0))],
        out_specs=pl.BlockSpec((n1, 2, 1, SUBLANES, tc), lambda u, c: (0, 0, u, 0, c)),
        out_shape=jax.ShapeDtypeStruct((n1, 2, nu, SUBLANES, D), F32),
        compiler_params=_cparams("parallel", "parallel"),
        name="fft_stage1",
    )(z5, gk)
    return t5.reshape(n1, 2, FFT_N2, D)


def _fft2_kernel(t_ref, h_ref, o_ref, slab_ref):
    nj, _, n2, d = t_ref.shape
    nslab = d // LANES
    for j in range(nj):
        tj = t_ref[j].reshape(2 * n2, d).astype(BF16)
        r = _dot(h_ref[j], tj)
        for s in range(nslab):
            slab_ref[s, pl.ds(j, n2, stride=nj), :] = r[:, s * LANES:(s + 1) * LANES]
    for s in range(nslab):
        o_ref[:, :, s * LANES:(s + 1) * LANES] = slab_ref[s].reshape(n2, nj, LANES)


def _fft_stage2(t, hmat, tc=512):
    n1, _, n2, D = t.shape
    nj = SUBLANES
    y3 = pl.pallas_call(
        _fft2_kernel,
        grid=(n1 // nj, D // tc),
        in_specs=[pl.BlockSpec((nj, 2, n2, tc), lambda a, c: (a, 0, 0, c)),
                  pl.BlockSpec((nj, n2, 2 * n2), lambda a, c: (a, 0, 0))],
        out_specs=pl.BlockSpec((n2, nj, tc), lambda a, c: (0, a, c)),
        out_shape=jax.ShapeDtypeStruct((n2, n1, D), F32),
        scratch_shapes=[pltpu.VMEM((tc // LANES, n2 * nj, LANES), F32)],
        compiler_params=_cparams("parallel", "parallel"),
        name="fft_stage2",
    )(t, hmat)
    return y3.reshape(n2 * n1, D)


def _fft_tables(S):
    n1, n2 = S // FFT_N2, FFT_N2
    c = np.arange(FNET_GROUP_W)
    ang = (2.0 * np.pi / FNET_GROUP_W) * ((c[:, None] * c[None, :]) % FNET_GROUP_W)
    scale = 1.0 / np.sqrt(float(S) * FNET_GROUP_W)
    fc = np.concatenate([np.cos(ang), -np.sin(ang)], axis=1) * scale
    k1 = np.arange(n1)
    th = (2.0 * np.pi / n1) * ((k1[:, None] * k1[None, :]) % n1)
    cs, sn = np.cos(th), np.sin(th)
    g = np.stack([np.stack([cs, sn], axis=1), np.stack([-sn, cs], axis=1)], axis=1)
    gk = np.kron(g.reshape(2 * n1, 2 * n1), np.eye(SUBLANES))
    k2 = np.arange(n2)
    kk = k1[:, None, None] + n1 * k2[None, :, None]
    ph = (2.0 * np.pi / S) * ((k2[None, None, :] * kk) % S)
    hm = np.concatenate([np.cos(ph), np.sin(ph)], axis=2)
    return tuple(jnp.asarray(t, F32).astype(BF16) for t in (fc, gk, hm))


def _rope_angles(pos, dim):
    inv_freq = ROPE_THETA ** (-(np.arange(0, dim, 2, dtype=np.float64) / dim))
    return pos.astype(np.float64)[:, None] * inv_freq[None, :]


def _rope_tables_axial(S):
    t = np.arange(S)
    ar = _rope_angles(t // GRID_W, HEAD_DIM // 2)
    ac = _rope_angles(t % GRID_W, HEAD_DIM // 2)
    cos = np.concatenate([np.cos(ar), np.cos(ar), np.cos(ac), np.cos(ac)], axis=1)
    sin = np.concatenate([-np.sin(ar), np.sin(ar), -np.sin(ac), np.sin(ac)], axis=1)
    return np.tile(cos, (1, 2)), np.tile(sin, (1, 2))


def _rope_tables_1d(S):
    a = _rope_angles(np.arange(S), HEAD_DIM)
    cos = np.concatenate([np.cos(a), np.cos(a)], axis=1)
    sin = np.concatenate([-np.sin(a), np.sin(a)], axis=1)
    return np.tile(cos, (1, 2)), np.tile(sin, (1, 2))


def _rotate_partner(x, half):
    lane = lax.broadcasted_iota(jnp.int32, (1, LANES), 1)
    first = lane % (2 * half) < half
    return jnp.where(first, pltpu.roll(x, LANES - half, 1), pltpu.roll(x, half, 1))


def _proj_b_kernel(x_ref, w_ref, cos_ref, sin_ref, gq_ref, gk_ref, o_ref, vt_ref):
    xb = x_ref[...].astype(BF16)
    cos_t, sin_t = cos_ref[...], sin_ref[...]
    lo = _lane_lo()
    nb_w = 4 * LANES
    vt = _dot(xb, w_ref[:, 3 * nb_w:]).T.astype(BF16)
    ones = jnp.ones((VT_ROWS - HEAD_DIM, vt.shape[1]), BF16)
    for hd in range(GQA_KV_HEADS):
        vt_ref[hd * VT_ROWS:hd * VT_ROWS + HEAD_DIM, :] = vt[hd * HEAD_DIM:(hd + 1) * HEAD_DIM]
        vt_ref[hd * VT_ROWS + HEAD_DIM:(hd + 1) * VT_ROWS, :] = ones
    for nb in range(3):
        r = _dot(xb, w_ref[:, nb * nb_w:(nb + 1) * nb_w])
        gain = gq_ref[...] if nb < 2 else gk_ref[...]
        scale = ATTN_SCALE * LOG2_E if nb < 2 else 1.0
        for t in range(4):
            rt = r[:, t * LANES:(t + 1) * LANES]
            r2 = rt * rt
            tot = jnp.sum(r2, axis=1, keepdims=True)
            low = jnp.sum(jnp.where(lo, r2, 0.0), axis=1, keepdims=True)
            ss = jnp.where(lo, low, tot - low)
            xn = rt * lax.rsqrt(ss * (1.0 / HEAD_DIM) + RMS_EPS) * gain
            out = (xn * cos_t + _rotate_partner(xn, HEAD_DIM // 4) * sin_t) * scale
            c0 = nb * nb_w + t * LANES
            o_ref[:, c0:c0 + LANES] = out.astype(BF16)


def _proj_b(h, w_qkv, cos_t, sin_t, gq, gk, tr=512):
    S = h.shape[0]
    n_in = w_qkv.shape[1]
    n_out = n_in - GQA_KV_HEADS * HEAD_DIM
    vw = GQA_KV_HEADS * VT_ROWS
    return pl.pallas_call(
        _proj_b_kernel,
        grid=(S // tr,),
        in_specs=[pl.BlockSpec((tr, D_MODEL), lambda i: (i, 0)),
                  pl.BlockSpec((D_MODEL, n_in), lambda i: (0, 0)),
                  pl.BlockSpec((tr, LANES), lambda i: (i, 0)),
                  pl.BlockSpec((tr, LANES), lambda i: (i, 0)),
                  pl.BlockSpec((1, LANES), lambda i: (0, 0)),
                  pl.BlockSpec((1, LANES), lambda i: (0, 0))],
        out_specs=[pl.BlockSpec((tr, n_out), lambda i: (i, 0)),
                   pl.BlockSpec((vw, tr), lambda i: (0, i))],
        out_shape=[jax.ShapeDtypeStruct((S, n_out), BF16),
                   jax.ShapeDtypeStruct((vw, S), BF16)],
        compiler_params=_cparams("parallel"),
        name="proj_b",
    )(h, w_qkv, cos_t, sin_t, gq, gk)


def _gqa_kernel(q_ref, k_ref, vt_ref, o_ref, qs_ref, m_ref, acc_ref, *p_refs, tk, depth):
    tq = q_ref.shape[0]
    S = k_ref.shape[0]
    nq = 4 * tq
    lo = _lane_lo()
    for t in range(2):
        qt = q_ref[:, t * LANES:(t + 1) * LANES].astype(F32)
        qs_ref[(2 * t) * tq:(2 * t + 1) * tq, :] = jnp.where(lo, qt, 0.0).astype(BF16)
        qs_ref[(2 * t + 1) * tq:(2 * t + 2) * tq, :] = jnp.where(lo, 0.0, qt).astype(BF16)
    nkv = S // tk
    nbuf = len(p_refs)

    def scores(j):
        k0 = pl.multiple_of(j * tk, tk)
        return _dot_nt(k_ref[pl.ds(k0, tk), :], qs_ref[...])

    def values(j, p):
        k0 = pl.multiple_of(j * tk, tk)
        return _dot(vt_ref[:, pl.ds(k0, tk)], p)

    def colmax(s):
        return jnp.max(s.reshape(tk // SUBLANES, SUBLANES, nq), axis=0)

    def all_rows(x):
        return jnp.broadcast_to(jnp.max(x, axis=0, keepdims=True), x.shape)

    def rows(x, n):
        return jnp.tile(x, (n // SUBLANES, 1))

    m_ref[...] = all_rows(colmax(scores(0)))
    acc_ref[...] = jnp.zeros(acc_ref.shape, F32)

    def trip(jj, carry):
        j0 = depth * jj
        m_old = m_ref[...]
        m_old_t = rows(m_old, tk)

        def probs(u):
            s = scores(j0 + u)
            p_refs[u % nbuf][...] = jnp.exp2(s - m_old_t).astype(BF16)
            return colmax(s)

        tmax = jnp.maximum(m_old, probs(0))
        pv = None
        for u in range(depth):
            if u + 1 < depth:
                tmax = jnp.maximum(tmax, probs(u + 1))
            d = values(j0 + u, p_refs[u % nbuf][...])
            pv = d if pv is None else pv + d
        m_new = all_rows(tmax)
        safe = jnp.max(m_new - m_old) <= MAX_REF_LAG

        @pl.when(safe)
        def _():
            acc_ref[...] = rows(jnp.exp2(m_old - m_new), VT_ROWS) * (acc_ref[...] + pv)
            m_ref[...] = m_new

        @pl.when(jnp.logical_not(safe))
        def _():
            def exact_tile(u, c):
                s = scores(j0 + u)
                m_prev = m_ref[...]
                m_cur = jnp.maximum(m_prev, all_rows(colmax(s)))
                d = values(j0 + u, jnp.exp2(s - rows(m_cur, tk)).astype(BF16))
                acc_ref[...] = rows(jnp.exp2(m_prev - m_cur), VT_ROWS) * acc_ref[...] + d
                m_ref[...] = m_cur
                return c
            lax.fori_loop(0, depth, exact_tile, 0)

        return carry

    lax.fori_loop(0, nkv // depth, trip, 0)
    acc = acc_ref[...]
    ot = acc[:HEAD_DIM] / acc[HEAD_DIM:HEAD_DIM + 1]
    for t in range(2):
        pair = jnp.concatenate([ot[:, (2 * t) * tq:(2 * t + 1) * tq],
                                ot[:, (2 * t + 1) * tq:(2 * t + 2) * tq]], axis=0)
        o_ref[:, t * LANES:(t + 1) * LANES] = pair.T


def _gqa_attention(qk, vt, tq=128, tk=512, depth=32, nbuf=2):
    S = qk.shape[0]
    qw = 4 * HEAD_DIM
    k_blk0 = GQA_Q_HEADS * HEAD_DIM // LANES
    tk = min(tk, S)
    depth = min(depth, S // tk)
    assert (S // tk) % depth == 0
    return pl.pallas_call(
        functools.partial(_gqa_kernel, tk=tk, depth=depth),
        grid=(GQA_KV_HEADS, S // tq),
        in_specs=[pl.BlockSpec((tq, qw), lambda h, i: (i, h)),
                  pl.BlockSpec((S, LANES), lambda h, i: (0, k_blk0 + h)),
                  pl.BlockSpec((VT_ROWS, S), lambda h, i: (h, 0))],
        out_specs=pl.BlockSpec((tq, qw), lambda h, i: (i, h)),
        out_shape=jax.ShapeDtypeStruct((S, D_MODEL), F32),
        scratch_shapes=[pltpu.VMEM((4 * tq, LANES), BF16),
                        pltpu.VMEM((SUBLANES, 4 * tq), F32),
                        pltpu.VMEM((VT_ROWS, 4 * tq), F32)]
        + [pltpu.VMEM((tk, 4 * tq), BF16)] * nbuf,
        compiler_params=_cparams("parallel", "parallel"),
        name="gqa_attention",
    )(qk, qk, vt)


def _proj_c_kernel(x_ref, w_ref, cos_ref, sin_ref, o_ref, xp_ref, slab_ref, *, dil):
    kind = pl.program_id(1)
    tm, d = x_ref.shape[0], o_ref.shape[-1]
    n = tm // dil
    nslab = d // LANES
    cb_w = 2 * LANES

    @pl.when(kind == 0)
    def _():
        if dil == 1:
            xp_ref[...] = x_ref[...].astype(BF16)
            return
        for s in range(nslab):
            slab_ref[s] = x_ref[:, s * LANES:(s + 1) * LANES]
        for c in range(dil):
            for s in range(nslab):
                xp_ref[c * n:(c + 1) * n, s * LANES:(s + 1) * LANES] = (
                    slab_ref[s, pl.ds(c, n, stride=dil), :].astype(BF16))

    def emit(c0, vals):
        o_ref[0, 0, :, :, c0:c0 + vals.shape[1]] = vals.astype(BF16).reshape(dil, n, vals.shape[1])

    @pl.when(kind < 2)
    def _():
        scale = jnp.where(kind == 0, ATTN_SCALE * LOG2_E, 1.0).astype(F32)
        cos_t = cos_ref[...] * scale
        sin_t = sin_ref[...] * scale
        ncb = d // cb_w
        r_next = _dot(xp_ref[...], w_ref[:, :cb_w])
        for cb in range(ncb):
            r = r_next
            if cb + 1 < ncb:
                r_next = _dot(xp_ref[...], w_ref[:, (cb + 1) * cb_w:(cb + 2) * cb_w])
            for s in range(cb_w // LANES):
                rt = r[:, s * LANES:(s + 1) * LANES]
                emit(cb * cb_w + s * LANES, rt * cos_t + _rotate_partner(rt, HEAD_DIM // 2) * sin_t)

    @pl.when(kind == 2)
    def _():
        for cb in range(d // cb_w):
            emit(cb * cb_w, _dot(xp_ref[...], w_ref[:, cb * cb_w:(cb + 1) * cb_w]))


def _regroup_rows(tab, dil):
    S, w = tab.shape
    return tab.reshape(S // DIL_CHUNK, DIL_CHUNK // dil, dil, w).transpose(0, 2, 1, 3).reshape(S, w)


def _proj_c(h, w, cos_np, sin_np, group, dil):
    S = h.shape[0]
    tm = DIL_CHUNK
    n = tm // dil
    ngroups = len(DIL_DILATIONS)
    cos_t = jnp.asarray(_regroup_rows(cos_np, dil), F32)
    sin_t = jnp.asarray(_regroup_rows(sin_np, dil), F32)
    return pl.pallas_call(
        functools.partial(_proj_c_kernel, dil=dil),
        grid=(S // tm, 3),
        in_specs=[pl.BlockSpec((tm, D_MODEL), lambda i, t: (i, 0)),
                  pl.BlockSpec((D_MODEL, D_MODEL), lambda i, t: (0, ngroups * t + group)),
                  pl.BlockSpec((tm, LANES), lambda i, t: (i, 0)),
                  pl.BlockSpec((tm, LANES), lambda i, t: (i, 0))],
        out_specs=pl.BlockSpec((1, 1, dil, n, D_MODEL), lambda i, t: (t, i, 0, 0, 0)),
        out_shape=jax.ShapeDtypeStruct((3, S // tm, dil, n, D_MODEL), BF16),
        scratch_shapes=[pltpu.VMEM((tm, D_MODEL), BF16),
                        pltpu.VMEM((D_MODEL // LANES, tm, LANES), F32)],
        compiler_params=_cparams("parallel", "arbitrary"),
        name=f"proj_c_dil{dil}",
    )(h, w, cos_t, sin_t)


def _dil_kernel(*refs, nchunks):
    ng = len(DIL_DILATIONS)
    o_ref, osc_ref, lsc_ref = refs[7 * ng:]
    i = pl.program_id(0)
    lo = _lane_lo()

    groups, blocks = [], []
    for g, dil in enumerate(DIL_DILATIONS):
        n = DIL_CHUNK // dil
        bq = min(n, 2 * DIL_SIDE)
        nk = bq + 2 * DIL_SIDE
        qi = lax.broadcasted_iota(jnp.int32, (2 * bq, nk), 0) % bq
        kj = lax.broadcasted_iota(jnp.int32, (2 * bq, nk), 1)
        band = jnp.where(jnp.abs(kj - DIL_SIDE - qi) <= DIL_SIDE, 0.0, MASK_VALUE).astype(F32)
        groups.append(dict(dil=dil, n=n, bq=bq, nk=nk, nsub=n // bq, kj=kj, band=band,
                           kpad=-(n + 2 * DIL_SIDE) % LANES, refs=refs[7 * g:7 * g + 7]))
        blocks += [(g, c, b) for c in range(dil) for b in range(n // bq)]

    class_kv = {}

    def keys_values(g, c):
        if (g, c) not in class_kv:
            G = groups[g]
            _, kc_ref, kp_ref, kn_ref, vc_ref, vp_ref, vn_ref = G["refs"]
            kparts = [kp_ref[0, 0, c], kc_ref[0, 0, c], kn_ref[0, 0, c]]
            if G["kpad"]:
                kparts.append(jnp.zeros((G["kpad"], LANES), BF16))
            kcat_t = jnp.concatenate(kparts, axis=0).astype(F32).T.astype(BF16)
            vcat = jnp.concatenate([vp_ref[0, 0, c], vc_ref[0, 0, c], vn_ref[0, 0, c]], axis=0)
            class_kv[(g, c)] = (kcat_t, vcat)
        return class_kv[(g, c)]

    def scores(blk):
        g, c, b = blk
        G = groups[g]
        bq, nk, kj = G["bq"], G["nk"], G["kj"]
        p0 = b * bq
        bias = G["band"]
        if b == 0:
            bias = jnp.where(kj >= jnp.where(i == 0, DIL_SIDE, 0), bias, MASK_VALUE)
        if b == G["nsub"] - 1:
            bias = jnp.where(kj < jnp.where(i == nchunks - 1, bq + DIL_SIDE, nk), bias, MASK_VALUE)
        qf = G["refs"][0][0, 0, c, p0:p0 + bq, :].astype(F32)
        qs = jnp.concatenate([jnp.where(lo, qf, 0.0), jnp.where(lo, 0.0, qf)], axis=0)
        return _dot(qs.astype(BF16), keys_values(g, c)[0][:, p0:p0 + nk]) + bias

    def finish(blk, s):
        g, c, b = blk
        G = groups[g]
        bq, nk, dil = G["bq"], G["nk"], G["dil"]
        p0 = b * bq
        m = jnp.max(s, axis=1, keepdims=True)
        p = jnp.exp2(s - m)
        l = jnp.sum(p, axis=1, keepdims=True)
        o2 = _dot(p.astype(BF16), keys_values(g, c)[1][p0:p0 + nk]) / l
        lse2 = m + jnp.log2(l)
        o_pair = jnp.where(lo, o2[:bq], o2[bq:])
        l_pair = jnp.where(lo, lse2[:bq], lse2[bq:])
        rows = pl.ds(p0, bq) if dil == 1 else pl.ds(p0 * dil + c, bq, stride=dil)
        osc_ref[g, rows, :] = o_pair
        lsc_ref[g, rows, :] = l_pair

    s_next = scores(blocks[0])
    for idx, blk in enumerate(blocks):
        s_cur = s_next
        if idx + 1 < len(blocks):
            s_next = scores(blocks[idx + 1])
        finish(blk, s_cur)

    ls = [lsc_ref[g] for g in range(ng)]
    mx = functools.reduce(jnp.maximum, ls)
    ws = [jnp.exp2(l - mx) for l in ls]
    num = sum(w * osc_ref[g] for g, w in enumerate(ws))
    o_ref[...] = num / sum(ws)


def _dilated_attention(qkvs):
    nchunks = qkvs[0].shape[1]
    S = nchunks * DIL_CHUNK
    in_specs, args = [], []
    for dil, a in zip(DIL_DILATIONS, qkvs):
        n = DIL_CHUNK // dil
        nblk = n // DIL_SIDE
        full = (1, 1, dil, n, LANES)
        halo = (1, 1, dil, DIL_SIDE, LANES)

        def cur(t):
            return lambda i, m: (t, i, 0, 0, m)

        def prev(t, nblk=nblk):
            return lambda i, m: (t, jnp.maximum(i - 1, 0), 0, nblk - 1, m)

        def nxt(t):
            return lambda i, m: (t, jnp.minimum(i + 1, nchunks - 1), 0, 0, m)

        in_specs += [pl.BlockSpec(full, cur(0)),
                     pl.BlockSpec(full, cur(1)), pl.BlockSpec(halo, prev(1)), pl.BlockSpec(halo, nxt(1)),
                     pl.BlockSpec(full, cur(2)), pl.BlockSpec(halo, prev(2)), pl.BlockSpec(halo, nxt(2))]
        args += [a] * 7
    ng = len(DIL_DILATIONS)
    return pl.pallas_call(
        functools.partial(_dil_kernel, nchunks=nchunks),
        grid=(nchunks, D_MODEL // LANES),
        in_specs=in_specs,
        out_specs=pl.BlockSpec((DIL_CHUNK, LANES), lambda i, m: (i, m)),
        out_shape=jax.ShapeDtypeStruct((S, D_MODEL), F32),
        scratch_shapes=[pltpu.VMEM((ng, DIL_CHUNK, LANES), F32),
                        pltpu.VMEM((ng, DIL_CHUNK, LANES), F32)],
        compiler_params=_cparams("parallel", "parallel"),
        name="dilated_attention",
    )(*args)


def _epilogue_kernel(br_ref, h_ref, wt_ref, mk_ref, mv_ref, wo_ref, g_ref, b_ref, o_ref):
    lo = _lane_lo()
    hb = h_ref[...].astype(BF16)
    gate_b = _dot(hb, wt_ref[:, :D_MODEL])
    yb = (br_ref[...] * (gate_b * jax.nn.sigmoid(gate_b))).astype(BF16)
    y = _dot(yb, wo_ref[:D_MODEL, :])
    tail_m = _dot(hb, wt_ref[:, D_MODEL:])
    for t in range(MEM_WIDTH // LANES):
        cols = slice(t * LANES, (t + 1) * LANES)
        qf = tail_m[:, MEM_WIDTH + t * LANES:MEM_WIDTH + (t + 1) * LANES] * ATTN_SCALE
        kt, vt = mk_ref[:, cols], mv_ref[:, cols]
        outs = []
        for qh in (jnp.where(lo, qf, 0.0), jnp.where(lo, 0.0, qf)):
            s = _dot_nt(qh.astype(BF16), kt)
            m = jnp.max(s, axis=1, keepdims=True)
            p = jnp.exp(s - m)
            l = jnp.sum(p, axis=1, keepdims=True)
            outs.append(_dot(p.astype(BF16), vt) / l)
        mem_out = jnp.where(lo, outs[0], outs[1])
        gm = tail_m[:, cols]
        ym = (mem_out * (gm * jax.nn.sigmoid(gm))).astype(BF16)
        y = y + _dot(ym, wo_ref[D_MODEL + t * LANES:D_MODEL + (t + 1) * LANES, :])
    z = DEEPNORM_ALPHA * h_ref[...] + y
    o_ref[...] = _layer_norm_rows(z, g_ref[...], b_ref[...])


def _epilogue(branch, h, w_tail, mk, mv, w_out, g, b, tr=512):
    S = h.shape[0]
    row = lambda i: (i, 0)
    fixed = lambda i: (0, 0)
    return pl.pallas_call(
        _epilogue_kernel,
        grid=(S // tr,),
        in_specs=[pl.BlockSpec((tr, D_MODEL), row),
                  pl.BlockSpec((tr, D_MODEL), row),
                  pl.BlockSpec((D_MODEL, TAIL), fixed),
                  pl.BlockSpec((N_MEM, MEM_WIDTH), fixed),
                  pl.BlockSpec((N_MEM, MEM_WIDTH), fixed),
                  pl.BlockSpec((INNER, D_MODEL), fixed),
                  pl.BlockSpec((1, D_MODEL), fixed),
                  pl.BlockSpec((1, D_MODEL), fixed)],
        out_specs=pl.BlockSpec((tr, D_MODEL), row),
        out_shape=jax.ShapeDtypeStruct((S, D_MODEL), F32),
        compiler_params=_cparams("parallel"),
        name="epilogue",
    )(branch, h, w_tail, mk, mv, w_out, g.reshape(1, -1), b.reshape(1, -1))


def _dup_kv_columns(w):
    d = w.shape[0]
    w4 = w.reshape(d, GQA_KV_HEADS, 1, HEAD_DIM)
    return jnp.broadcast_to(w4, (d, GQA_KV_HEADS, 2, HEAD_DIM)).reshape(d, 2 * GQA_KV_HEADS * HEAD_DIM)


def kernel(x, mem, ln_in_g, ln_in_b, w_mem_kv, w_in_a, w_in_b, q_norm_g, k_norm_g, w_in_c, w_out, ln_g, ln_b):
    B, S, D = x.shape
    assert B == 1 and D == D_MODEL and S % DIL_CHUNK == 0 and S % (FFT_N2 * SUBLANES) == 0

    mkv = _proj(mem[0], w_mem_kv.astype(BF16), name="proj_mem").astype(BF16)
    mk, mv = mkv[:, :MEM_WIDTH], mkv[:, MEM_WIDTH:]
    h = _ln_in(x[0], ln_in_g, ln_in_b)
    fft_tabs = None
    qw = GQA_Q_HEADS * HEAD_DIM
    kvw = GQA_KV_HEADS * HEAD_DIM

    for i in range(DEPTH):
        kind, j = i % 3, i // 3
        if kind == 0:
            if fft_tabs is None:
                fft_tabs = _fft_tables(S)
            fc, gk, hm = fft_tabs
            w_tail = w_in_a[j].astype(BF16)
            branch = _fft_stage2(_fft_stage1(_proj_a(h, fc), gk), hm)
        elif kind == 1:
            w = w_in_b[j]
            w_qkv = jnp.concatenate([w[:, :qw], _dup_kv_columns(w[:, qw:qw + kvw]),
                                     w[:, qw + kvw:qw + 2 * kvw]], axis=1).astype(BF16)
            cos_t, sin_t = (jnp.asarray(t, F32) for t in _rope_tables_axial(S))
            gq = jnp.tile(q_norm_g[j], 2).reshape(1, LANES)
            gkn = jnp.tile(k_norm_g[j], 2).reshape(1, LANES)
            qk, vt = _proj_b(h, w_qkv, cos_t, sin_t, gq, gkn)
            w_tail = w[:, qw + 2 * kvw:].astype(BF16)
            branch = _gqa_attention(qk, vt)
        else:
            w = w_in_c[j].astype(BF16)
            cos_t, sin_t = _rope_tables_1d(S)
            qkvs = [_proj_c(h, w, cos_t, sin_t, g, dil) for g, dil in enumerate(DIL_DILATIONS)]
            w_tail = w[:, w.shape[1] - TAIL:]
            branch = _dilated_attention(qkvs)
        h = _epilogue(branch, h, w_tail, mk, mv, w_out[i].astype(BF16), ln_g[i], ln_b[i])
    return h[None]
```

```python
import functools

import jax
import jax.numpy as jnp
import numpy as np
from jax import lax
from jax.experimental import pallas as pl
from jax.experimental.pallas import tpu as pltpu

F32 = jnp.float32
BF16 = jnp.bfloat16

D_MODEL = 1024
DEPTH = 4
N_MEM = 256
GRID_W = 64
HEAD_DIM = 64
ROPE_THETA = 10000.0
LN_EPS = 1e-5
RMS_EPS = 1e-6
MASK_VALUE = -1e30
FNET_GROUPS = 4
FNET_GROUP_W = D_MODEL // FNET_GROUPS
GQA_Q_HEADS = 16
GQA_KV_HEADS = 4
DIL_DILATIONS = (1, 4, 16)
DIL_SIDE = 64
MEM_WIDTH = 256
INNER = D_MODEL + MEM_WIDTH
TAIL = INNER + MEM_WIDTH
DEEPNORM_ALPHA = (2.0 * DEPTH) ** 0.25
ATTN_SCALE = HEAD_DIM ** -0.5
LOG2_E = float(np.log2(np.e))
VT_ROWS = HEAD_DIM + 16
LANES = 128
SUBLANES = 8
FFT_N2 = 256
DIL_CHUNK = 1024
VMEM_LIMIT = 48 << 20


def _cparams(*sem):
    return pltpu.CompilerParams(dimension_semantics=sem, vmem_limit_bytes=VMEM_LIMIT)


def _dot(a, b):
    return jnp.dot(a, b, preferred_element_type=F32)


def _dot_nt(a, b):
    return lax.dot_general(a, b, (((1,), (1,)), ((), ())), preferred_element_type=F32)


def _lane_lo(shape=(1, LANES)):
    return lax.broadcasted_iota(jnp.int32, shape, len(shape) - 1) % LANES < HEAD_DIM


def _layer_norm_rows(z, g, b):
    mu = jnp.mean(z, axis=-1, keepdims=True)
    zc = z - mu
    var = jnp.mean(zc * zc, axis=-1, keepdims=True)
    return zc * lax.rsqrt(var + LN_EPS) * g + b


def _ln_kernel(x_ref, g_ref, b_ref, o_ref):
    o_ref[...] = _layer_norm_rows(x_ref[...], g_ref[...], b_ref[...])


def _ln_in(x, g, b, tr=512):
    S = x.shape[0]
    return pl.pallas_call(
        _ln_kernel,
        grid=(S // tr,),
        in_specs=[pl.BlockSpec((tr, D_MODEL), lambda i: (i, 0)),
                  pl.BlockSpec((1, D_MODEL), lambda i: (0, 0)),
                  pl.BlockSpec((1, D_MODEL), lambda i: (0, 0))],
        out_specs=pl.BlockSpec((tr, D_MODEL), lambda i: (i, 0)),
        out_shape=jax.ShapeDtypeStruct((S, D_MODEL), F32),
        compiler_params=_cparams("parallel"),
        name="ln_in",
    )(x, g.reshape(1, -1), b.reshape(1, -1))


def _proj_kernel(x_ref, w_ref, o_ref):
    o_ref[...] = _dot(x_ref[...].astype(BF16), w_ref[...])


def _proj(x, w, tr=512, name="proj"):
    S, K = x.shape
    N = w.shape[1]
    tr = min(tr, S)
    return pl.pallas_call(
        _proj_kernel,
        grid=(S // tr,),
        in_specs=[pl.BlockSpec((tr, K), lambda i: (i, 0)),
                  pl.BlockSpec((K, N), lambda i: (0, 0))],
        out_specs=pl.BlockSpec((tr, N), lambda i: (i, 0)),
        out_shape=jax.ShapeDtypeStruct((S, N), F32),
        compiler_params=_cparams("parallel"),
        name=name,
    )(x, w)


def _proj_a_kernel(x_ref, fc_ref, z_ref):
    xb = x_ref[...].astype(BF16)
    for g in range(FNET_GROUPS):
        cols = slice(g * FNET_GROUP_W, (g + 1) * FNET_GROUP_W)
        zg = _dot(xb[:, cols], fc_ref[...])
        z_ref[0, :, cols] = zg[:, :FNET_GROUP_W]
        z_ref[1, :, cols] = zg[:, FNET_GROUP_W:]


def _proj_a(h, fc, tr=512):
    S = h.shape[0]
    return pl.pallas_call(
        _proj_a_kernel,
        grid=(S // tr,),
        in_specs=[pl.BlockSpec((tr, D_MODEL), lambda i: (i, 0)),
                  pl.BlockSpec((FNET_GROUP_W, 2 * FNET_GROUP_W), lambda i: (0, 0))],
        out_specs=pl.BlockSpec((2, tr, D_MODEL), lambda i: (0, i, 0)),
        out_shape=jax.ShapeDtypeStruct((2, S, D_MODEL), F32),
        compiler_params=_cparams("parallel"),
        name="proj_a",
    )(h, fc)


def _fft1_kernel(z_ref, g_ref, t_ref):
    two, n1, _, sub, tc = z_ref.shape
    x = z_ref[...].reshape(two * n1 * sub, tc).astype(BF16)
    t_ref[...] = _dot(g_ref[...], x).reshape(t_ref.shape)


def _fft_stage1(z, gk, tc=1024):
    _, S, D = z.shape
    n1 = S // FFT_N2
    nu = FFT_N2 // SUBLANES
    z5 = z.reshape(2, n1, nu, SUBLANES, D)
    rows = 2 * n1 * SUBLANES
    t5 = pl.pallas_call(
        _fft1_kernel,
        grid=(nu, D // tc),
        in_specs=[pl.BlockSpec((2, n1, 1, SUBLANES, tc), lambda u, c: (0, 0, u, 0, c)),
                  pl.BlockSpec((rows, rows), lambda u, c: (0, 0))],
        out_specs=pl.BlockSpec((n1, 2, 1, SUBLANES, tc), lambda u, c: (0, 0, u, 0, c)),
        out_shape=jax.ShapeDtypeStruct((n1, 2, nu, SUBLANES, D), F32),
        compiler_params=_cparams("parallel", "parallel"),
        name="fft_stage1",
    )(z5, gk)
    return t5.reshape(n1, 2, FFT_N2, D)


def _fft2_kernel(t_ref, h_ref, o_ref, slab_ref):
    nj, _, n2, d = t_ref.shape
    nslab = d // LANES
    for j in range(nj):
        tj = t_ref[j].reshape(2 * n2, d).astype(BF16)
        r = _dot(h_ref[j], tj)
        for s in range(nslab):
            slab_ref[s, pl.ds(j, n2, stride=nj), :] = r[:, s * LANES:(s + 1) * LANES]
    for s in range(nslab):
        o_ref[:, :, s * LANES:(s + 1) * LANES] = slab_ref[s].reshape(n2, nj, LANES)


def _fft_stage2(t, hmat, tc=512):
    n1, _, n2, D = t.shape
    nj = SUBLANES
    y3 = pl.pallas_call(
        _fft2_kernel,
        grid=(n1 // nj, D // tc),
        in_specs=[pl.BlockSpec((nj, 2, n2, tc), lambda a, c: (a, 0, 0, c)),
                  pl.BlockSpec((nj, n2, 2 * n2), lambda a, c: (a, 0, 0))],
        out_specs=pl.BlockSpec((n2, nj, tc), lambda a, c: (0, a, c)),
        out_shape=jax.ShapeDtypeStruct((n2, n1, D), F32),
        scratch_shapes=[pltpu.VMEM((tc // LANES, n2 * nj, LANES), F32)],
        compiler_params=_cparams("parallel", "parallel"),
        name="fft_stage2",
    )(t, hmat)
    return y3.reshape(n2 * n1, D)


def _fft_tables(S):
    n1, n2 = S // FFT_N2, FFT_N2
    c = np.arange(FNET_GROUP_W)
    ang = (2.0 * np.pi / FNET_GROUP_W) * ((c[:, None] * c[None, :]) % FNET_GROUP_W)
    scale = 1.0 / np.sqrt(float(S) * FNET_GROUP_W)
    fc = np.concatenate([np.cos(ang), -np.sin(ang)], axis=1) * scale
    k1 = np.arange(n1)
    th = (2.0 * np.pi / n1) * ((k1[:, None] * k1[None, :]) % n1)
    cs, sn = np.cos(th), np.sin(th)
    g = np.stack([np.stack([cs, sn], axis=1), np.stack([-sn, cs], axis=1)], axis=1)
    gk = np.kron(g.reshape(2 * n1, 2 * n1), np.eye(SUBLANES))
    k2 = np.arange(n2)
    kk = k1[:, None, None] + n1 * k2[None, :, None]
    ph = (2.0 * np.pi / S) * ((k2[None, None, :] * kk) % S)
    hm = np.concatenate([np.cos(ph), np.sin(ph)], axis=2)
    return tuple(jnp.asarray(t, F32).astype(BF16) for t in (fc, gk, hm))


def _rope_angles(pos, dim):
    inv_freq = ROPE_THETA ** (-(np.arange(0, dim, 2, dtype=np.float64) / dim))
    return pos.astype(np.float64)[:, None] * inv_freq[None, :]


def _rope_tables_axial(S):
    t = np.arange(S)
    ar = _rope_angles(t // GRID_W, HEAD_DIM // 2)
    ac = _rope_angles(t % GRID_W, HEAD_DIM // 2)
    cos = np.concatenate([np.cos(ar), np.cos(ar), np.cos(ac), np.cos(ac)], axis=1)
    sin = np.concatenate([-np.sin(ar), np.sin(ar), -np.sin(ac), np.sin(ac)], axis=1)
    return np.tile(cos, (1, 2)), np.tile(sin, (1, 2))


def _rope_tables_1d(S):
    a = _rope_angles(np.arange(S), HEAD_DIM)
    cos = np.concatenate([np.cos(a), np.cos(a)], axis=1)
    sin = np.concatenate([-np.sin(a), np.sin(a)], axis=1)
    return np.tile(cos, (1, 2)), np.tile(sin, (1, 2))


def _rotate_partner(x, half):
    lane = lax.broadcasted_iota(jnp.int32, (1, LANES), 1)
    first = lane % (2 * half) < half
    return jnp.where(first, pltpu.roll(x, LANES - half, 1), pltpu.roll(x, half, 1))


def _proj_b_kernel(x_ref, w_ref, cos_ref, sin_ref, gq_ref, gk_ref, o_ref, vt_ref):
    xb = x_ref[...].astype(BF16)
    cos_t, sin_t = cos_ref[...], sin_ref[...]
    lo = _lane_lo()
    nb_w = 4 * LANES
    vt = _dot(xb, w_ref[:, 3 * nb_w:]).T.astype(BF16)
    ones = jnp.ones((VT_ROWS - HEAD_DIM, vt.shape[1]), BF16)
    for hd in range(GQA_KV_HEADS):
        vt_ref[hd * VT_ROWS:hd * VT_ROWS + HEAD_DIM, :] = vt[hd * HEAD_DIM:(hd + 1) * HEAD_DIM]
        vt_ref[hd * VT_ROWS + HEAD_DIM:(hd + 1) * VT_ROWS, :] = ones
    for nb in range(3):
        r = _dot(xb, w_ref[:, nb * nb_w:(nb + 1) * nb_w])
        gain = gq_ref[...] if nb < 2 else gk_ref[...]
        scale = ATTN_SCALE * LOG2_E if nb < 2 else 1.0
        for t in range(4):
            rt = r[:, t * LANES:(t + 1) * LANES]
            r2 = rt * rt
            tot = jnp.sum(r2, axis=1, keepdims=True)
            low = jnp.sum(jnp.where(lo, r2, 0.0), axis=1, keepdims=True)
            ss = jnp.where(lo, low, tot - low)
            xn = rt * lax.rsqrt(ss * (1.0 / HEAD_DIM) + RMS_EPS) * gain
            out = (xn * cos_t + _rotate_partner(xn, HEAD_DIM // 4) * sin_t) * scale
            c0 = nb * nb_w + t * LANES
            o_ref[:, c0:c0 + LANES] = out.astype(BF16)


def _proj_b(h, w_qkv, cos_t, sin_t, gq, gk, tr=512):
    S = h.shape[0]
    n_in = w_qkv.shape[1]
    n_out = n_in - GQA_KV_HEADS * HEAD_DIM
    vw = GQA_KV_HEADS * VT_ROWS
    return pl.pallas_call(
        _proj_b_kernel,
        grid=(S // tr,),
        in_specs=[pl.BlockSpec((tr, D_MODEL), lambda i: (i, 0)),
                  pl.BlockSpec((D_MODEL, n_in), lambda i: (0, 0)),
                  pl.BlockSpec((tr, LANES), lambda i: (i, 0)),
                  pl.BlockSpec((tr, LANES), lambda i: (i, 0)),
                  pl.BlockSpec((1, LANES), lambda i: (0, 0)),
                  pl.BlockSpec((1, LANES), lambda i: (0, 0))],
        out_specs=[pl.BlockSpec((tr, n_out), lambda i: (i, 0)),
                   pl.BlockSpec((vw, tr), lambda i: (0, i))],
        out_shape=[jax.ShapeDtypeStruct((S, n_out), BF16),
                   jax.ShapeDtypeStruct((vw, S), BF16)],
        compiler_params=_cparams("parallel"),
        name="proj_b",
    )(h, w_qkv, cos_t, sin_t, gq, gk)


def _gqa_kernel(q_ref, k_ref, vt_ref, o_ref, qs_ref, m_ref, acc_ref, *p_refs, tk):
    tq = q_ref.shape[0]
    S = k_ref.shape[0]
    nq = 4 * tq
    nkv = S // tk
    nbuf = len(p_refs)
    lo = _lane_lo()
    for t in range(2):
        qt = q_ref[:, t * LANES:(t + 1) * LANES].astype(F32)
        qs_ref[(2 * t) * tq:(2 * t + 1) * tq, :] = jnp.where(lo, qt, 0.0).astype(BF16)
        qs_ref[(2 * t + 1) * tq:(2 * t + 2) * tq, :] = jnp.where(lo, 0.0, qt).astype(BF16)

    def scores(j):
        k0 = pl.multiple_of(j * tk, tk)
        return _dot_nt(k_ref[pl.ds(k0, tk), :], qs_ref[...])

    def values(j, p):
        return _dot(vt_ref[:, pl.ds(pl.multiple_of(j * tk, tk), tk)], p)

    def colmax(s):
        return jnp.max(jnp.max(s.reshape(tk // SUBLANES, SUBLANES, nq), axis=0), axis=0, keepdims=True)

    m0 = jnp.broadcast_to(colmax(scores(0)), (tk, nq))

    def probs(u):
        p_refs[u % nbuf][...] = jnp.exp2(scores(u) - m0).astype(BF16)

    probs(0)
    pv = None
    for u in range(nkv):
        if u + 1 < nkv:
            probs(u + 1)
        d = values(u, p_refs[u % nbuf][...])
        pv = d if pv is None else pv + d
    safe = jnp.min(jnp.where(jnp.isfinite(pv), 1.0, 0.0)) > 0.5

    @pl.when(safe)
    def _():
        acc_ref[...] = pv

    @pl.when(jnp.logical_not(safe))
    def _():
        m_ref[...] = jnp.full(m_ref.shape, -jnp.inf, F32)
        acc_ref[...] = jnp.zeros(acc_ref.shape, F32)

        def exact_tile(u, c):
            s = scores(u)
            m_prev = m_ref[...]
            m_cur = jnp.maximum(m_prev, colmax(s))
            d = values(u, jnp.exp2(s - m_cur).astype(BF16))
            acc_ref[...] = jnp.exp2(m_prev - m_cur) * acc_ref[...] + d
            m_ref[...] = m_cur
            return c

        lax.fori_loop(0, nkv, exact_tile, 0)

    acc = acc_ref[...]
    ot = acc[:HEAD_DIM] / acc[HEAD_DIM:HEAD_DIM + 1]
    for t in range(2):
        pair = jnp.concatenate([ot[:, (2 * t) * tq:(2 * t + 1) * tq],
                                ot[:, (2 * t + 1) * tq:(2 * t + 2) * tq]], axis=0)
        o_ref[:, t * LANES:(t + 1) * LANES] = pair.T


def _gqa_attention(qk, vt, tq=256, tk=256, nbuf=2):
    S = qk.shape[0]
    qw = 4 * HEAD_DIM
    k_blk0 = GQA_Q_HEADS * HEAD_DIM // LANES
    tk = min(tk, S)
    return pl.pallas_call(
        functools.partial(_gqa_kernel, tk=tk),
        grid=(GQA_KV_HEADS, S // tq),
        in_specs=[pl.BlockSpec((tq, qw), lambda h, i: (i, h)),
                  pl.BlockSpec((S, LANES), lambda h, i: (0, k_blk0 + h)),
                  pl.BlockSpec((VT_ROWS, S), lambda h, i: (h, 0))],
        out_specs=pl.BlockSpec((tq, qw), lambda h, i: (i, h)),
        out_shape=jax.ShapeDtypeStruct((S, D_MODEL), F32),
        scratch_shapes=[pltpu.VMEM((4 * tq, LANES), BF16),
                        pltpu.VMEM((1, 4 * tq), F32),
                        pltpu.VMEM((VT_ROWS, 4 * tq), F32)]
        + [pltpu.VMEM((tk, 4 * tq), BF16)] * nbuf,
        compiler_params=_cparams("parallel", "parallel"),
        name="gqa_attention",
    )(qk, qk, vt)


def _proj_c_kernel(x_ref, w_ref, cos_ref, sin_ref, o_ref, xp_ref, slab_ref, *, dil):
    kind = pl.program_id(1)
    tm, d = x_ref.shape[0], o_ref.shape[-1]
    n = tm // dil
    nslab = d // LANES
    cb_w = 2 * LANES

    @pl.when(kind == 0)
    def _():
        if dil == 1:
            xp_ref[...] = x_ref[...].astype(BF16)
            return
        for s in range(nslab):
            slab_ref[s] = x_ref[:, s * LANES:(s + 1) * LANES]
        for c in range(dil):
            for s in range(nslab):
                xp_ref[c * n:(c + 1) * n, s * LANES:(s + 1) * LANES] = (
                    slab_ref[s, pl.ds(c, n, stride=dil), :].astype(BF16))

    def emit(c0, vals):
        o_ref[0, 0, :, :, c0:c0 + vals.shape[1]] = vals.astype(BF16).reshape(dil, n, vals.shape[1])

    @pl.when(kind < 2)
    def _():
        scale = jnp.where(kind == 0, ATTN_SCALE * LOG2_E, 1.0).astype(F32)
        cos_t = cos_ref[...] * scale
        sin_t = sin_ref[...] * scale
        ncb = d // cb_w
        r_next = _dot(xp_ref[...], w_ref[:, :cb_w])
        for cb in range(ncb):
            r = r_next
            if cb + 1 < ncb:
                r_next = _dot(xp_ref[...], w_ref[:, (cb + 1) * cb_w:(cb + 2) * cb_w])
            for s in range(cb_w // LANES):
                rt = r[:, s * LANES:(s + 1) * LANES]
                emit(cb * cb_w + s * LANES, rt * cos_t + _rotate_partner(rt, HEAD_DIM // 2) * sin_t)

    @pl.when(kind == 2)
    def _():
        for cb in range(d // cb_w):
            emit(cb * cb_w, _dot(xp_ref[...], w_ref[:, cb * cb_w:(cb + 1) * cb_w]))


def _regroup_rows(tab, dil):
    S, w = tab.shape
    return tab.reshape(S // DIL_CHUNK, DIL_CHUNK // dil, dil, w).transpose(0, 2, 1, 3).reshape(S, w)


def _proj_c(h, w, cos_np, sin_np, group, dil):
    S = h.shape[0]
    tm = DIL_CHUNK
    n = tm // dil
    ngroups = len(DIL_DILATIONS)
    cos_t = jnp.asarray(_regroup_rows(cos_np, dil), F32)
    sin_t = jnp.asarray(_regroup_rows(sin_np, dil), F32)
    return pl.pallas_call(
        functools.partial(_proj_c_kernel, dil=dil),
        grid=(S // tm, 3),
        in_specs=[pl.BlockSpec((tm, D_MODEL), lambda i, t: (i, 0)),
                  pl.BlockSpec((D_MODEL, D_MODEL), lambda i, t: (0, ngroups * t + group)),
                  pl.BlockSpec((tm, LANES), lambda i, t: (i, 0)),
                  pl.BlockSpec((tm, LANES), lambda i, t: (i, 0))],
        out_specs=pl.BlockSpec((1, 1, dil, n, D_MODEL), lambda i, t: (t, i, 0, 0, 0)),
        out_shape=jax.ShapeDtypeStruct((3, S // tm, dil, n, D_MODEL), BF16),
        scratch_shapes=[pltpu.VMEM((tm, D_MODEL), BF16),
                        pltpu.VMEM((D_MODEL // LANES, tm, LANES), F32)],
        compiler_params=_cparams("parallel", "arbitrary"),
        name=f"proj_c_dil{dil}",
    )(h, w, cos_t, sin_t)


def _dil_kernel(*refs, nchunks):
    ng = len(DIL_DILATIONS)
    o_ref, osc_ref, lsc_ref = refs[7 * ng:]
    i = pl.program_id(0)
    lo = _lane_lo()

    groups, blocks = [], []
    for g, dil in enumerate(DIL_DILATIONS):
        n = DIL_CHUNK // dil
        bq = min(n, 2 * DIL_SIDE)
        nk = bq + 2 * DIL_SIDE
        qi = lax.broadcasted_iota(jnp.int32, (2 * bq, nk), 0) % bq
        kj = lax.broadcasted_iota(jnp.int32, (2 * bq, nk), 1)
        band = jnp.where(jnp.abs(kj - DIL_SIDE - qi) <= DIL_SIDE, 0.0, MASK_VALUE).astype(F32)
        groups.append(dict(dil=dil, n=n, bq=bq, nk=nk, nsub=n // bq, kj=kj, band=band,
                           kpad=-(n + 2 * DIL_SIDE) % LANES, refs=refs[7 * g:7 * g + 7]))
        blocks += [(g, c, b) for c in range(dil) for b in range(n // bq)]

    class_kv = {}

    def keys_values(g, c):
        if (g, c) not in class_kv:
            G = groups[g]
            _, kc_ref, kp_ref, kn_ref, vc_ref, vp_ref, vn_ref = G["refs"]
            kparts = [kp_ref[0, 0, c], kc_ref[0, 0, c], kn_ref[0, 0, c]]
            if G["kpad"]:
                kparts.append(jnp.zeros((G["kpad"], LANES), BF16))
            kcat_t = jnp.concatenate(kparts, axis=0).astype(F32).T.astype(BF16)
            vcat = jnp.concatenate([vp_ref[0, 0, c], vc_ref[0, 0, c], vn_ref[0, 0, c]], axis=0)
            class_kv[(g, c)] = (kcat_t, vcat)
        return class_kv[(g, c)]

    def scores(blk):
        g, c, b = blk
        G = groups[g]
        bq, nk, kj = G["bq"], G["nk"], G["kj"]
        p0 = b * bq
        bias = G["band"]
        if b == 0:
            bias = jnp.where(kj >= jnp.where(i == 0, DIL_SIDE, 0), bias, MASK_VALUE)
        if b == G["nsub"] - 1:
            bias = jnp.where(kj < jnp.where(i == nchunks - 1, bq + DIL_SIDE, nk), bias, MASK_VALUE)
        qf = G["refs"][0][0, 0, c, p0:p0 + bq, :].astype(F32)
        qs = jnp.concatenate([jnp.where(lo, qf, 0.0), jnp.where(lo, 0.0, qf)], axis=0)
        return _dot(qs.astype(BF16), keys_values(g, c)[0][:, p0:p0 + nk]) + bias

    def finish(blk, s):
        g, c, b = blk
        G = groups[g]
        bq, nk, dil = G["bq"], G["nk"], G["dil"]
        p0 = b * bq
        m = jnp.max(s, axis=1, keepdims=True)
        p = jnp.exp2(s - m)
        l = jnp.sum(p, axis=1, keepdims=True)
        o2 = _dot(p.astype(BF16), keys_values(g, c)[1][p0:p0 + nk]) / l
        lse2 = m + jnp.log2(l)
        o_pair = jnp.where(lo, o2[:bq], o2[bq:])
        l_pair = jnp.where(lo, lse2[:bq], lse2[bq:])
        rows = pl.ds(p0, bq) if dil == 1 else pl.ds(p0 * dil + c, bq, stride=dil)
        osc_ref[g, rows, :] = o_pair
        lsc_ref[g, rows, :] = l_pair

    s_next = scores(blocks[0])
    for idx, blk in enumerate(blocks):
        s_cur = s_next
        if idx + 1 < len(blocks):
            s_next = scores(blocks[idx + 1])
        finish(blk, s_cur)

    ls = [lsc_ref[g] for g in range(ng)]
    mx = functools.reduce(jnp.maximum, ls)
    ws = [jnp.exp2(l - mx) for l in ls]
    num = sum(w * osc_ref[g] for g, w in enumerate(ws))
    o_ref[...] = num / sum(ws)


def _dilated_attention(qkvs):
    nchunks = qkvs[0].shape[1]
    S = nchunks * DIL_CHUNK
    in_specs, args = [], []
    for dil, a in zip(DIL_DILATIONS, qkvs):
        n = DIL_CHUNK // dil
        nblk = n // DIL_SIDE
        full = (1, 1, dil, n, LANES)
        halo = (1, 1, dil, DIL_SIDE, LANES)

        def cur(t):
            return lambda i, m: (t, i, 0, 0, m)

        def prev(t, nblk=nblk):
            return lambda i, m: (t, jnp.maximum(i - 1, 0), 0, nblk - 1, m)

        def nxt(t):
            return lambda i, m: (t, jnp.minimum(i + 1, nchunks - 1), 0, 0, m)

        in_specs += [pl.BlockSpec(full, cur(0)),
                     pl.BlockSpec(full, cur(1)), pl.BlockSpec(halo, prev(1)), pl.BlockSpec(halo, nxt(1)),
                     pl.BlockSpec(full, cur(2)), pl.BlockSpec(halo, prev(2)), pl.BlockSpec(halo, nxt(2))]
        args += [a] * 7
    ng = len(DIL_DILATIONS)
    return pl.pallas_call(
        functools.partial(_dil_kernel, nchunks=nchunks),
        grid=(nchunks, D_MODEL // LANES),
        in_specs=in_specs,
        out_specs=pl.BlockSpec((DIL_CHUNK, LANES), lambda i, m: (i, m)),
        out_shape=jax.ShapeDtypeStruct((S, D_MODEL), F32),
        scratch_shapes=[pltpu.VMEM((ng, DIL_CHUNK, LANES), F32),
                        pltpu.VMEM((ng, DIL_CHUNK, LANES), F32)],
        compiler_params=_cparams("parallel", "parallel"),
        name="dilated_attention",
    )(*args)


def _epilogue_kernel(br_ref, h_ref, wt_ref, mk_ref, mv_ref, wo_ref, g_ref, b_ref, o_ref):
    lo = _lane_lo()
    hb = h_ref[...].astype(BF16)
    gate_b = _dot(hb, wt_ref[:, :D_MODEL])
    yb = (br_ref[...] * (gate_b * jax.nn.sigmoid(gate_b))).astype(BF16)
    y = _dot(yb, wo_ref[:D_MODEL, :])
    tail_m = _dot(hb, wt_ref[:, D_MODEL:])
    for t in range(MEM_WIDTH // LANES):
        cols = slice(t * LANES, (t + 1) * LANES)
        qf = tail_m[:, MEM_WIDTH + t * LANES:MEM_WIDTH + (t + 1) * LANES] * ATTN_SCALE
        kt, vt = mk_ref[:, cols], mv_ref[:, cols]
        outs = []
        for qh in (jnp.where(lo, qf, 0.0), jnp.where(lo, 0.0, qf)):
            s = _dot_nt(qh.astype(BF16), kt)
            m = jnp.max(s, axis=1, keepdims=True)
            p = jnp.exp(s - m)
            l = jnp.sum(p, axis=1, keepdims=True)
            outs.append(_dot(p.astype(BF16), vt) / l)
        mem_out = jnp.where(lo, outs[0], outs[1])
        gm = tail_m[:, cols]
        ym = (mem_out * (gm * jax.nn.sigmoid(gm))).astype(BF16)
        y = y + _dot(ym, wo_ref[D_MODEL + t * LANES:D_MODEL + (t + 1) * LANES, :])
    z = DEEPNORM_ALPHA * h_ref[...] + y
    o_ref[...] = _layer_norm_rows(z, g_ref[...], b_ref[...])


def _epilogue(branch, h, w_tail, mk, mv, w_out, g, b, tr=512):
    S = h.shape[0]
    row = lambda i: (i, 0)
    fixed = lambda i: (0, 0)
    return pl.pallas_call(
        _epilogue_kernel,
        grid=(S // tr,),
        in_specs=[pl.BlockSpec((tr, D_MODEL), row),
                  pl.BlockSpec((tr, D_MODEL), row),
                  pl.BlockSpec((D_MODEL, TAIL), fixed),
                  pl.BlockSpec((N_MEM, MEM_WIDTH), fixed),
                  pl.BlockSpec((N_MEM, MEM_WIDTH), fixed),
                  pl.BlockSpec((INNER, D_MODEL), fixed),
                  pl.BlockSpec((1, D_MODEL), fixed),
                  pl.BlockSpec((1, D_MODEL), fixed)],
        out_specs=pl.BlockSpec((tr, D_MODEL), row),
        out_shape=jax.ShapeDtypeStruct((S, D_MODEL), F32),
        compiler_params=_cparams("parallel"),
        name="epilogue",
    )(branch, h, w_tail, mk, mv, w_out, g.reshape(1, -1), b.reshape(1, -1))


def _dup_kv_columns(w):
    d = w.shape[0]
    w4 = w.reshape(d, GQA_KV_HEADS, 1, HEAD_DIM)
    return jnp.broadcast_to(w4, (d, GQA_KV_HEADS, 2, HEAD_DIM)).reshape(d, 2 * GQA_KV_HEADS * HEAD_DIM)


def kernel(x, mem, ln_in_g, ln_in_b, w_mem_kv, w_in_a, w_in_b, q_norm_g, k_norm_g, w_in_c, w_out, ln_g, ln_b):
    B, S, D = x.shape
    assert B == 1 and D == D_MODEL and S % DIL_CHUNK == 0 and S % (FFT_N2 * SUBLANES) == 0

    mkv = _proj(mem[0], w_mem_kv.astype(BF16), name="proj_mem").astype(BF16)
    mk, mv = mkv[:, :MEM_WIDTH], mkv[:, MEM_WIDTH:]
    h = _ln_in(x[0], ln_in_g, ln_in_b)
    fft_tabs = None
    qw = GQA_Q_HEADS * HEAD_DIM
    kvw = GQA_KV_HEADS * HEAD_DIM

    for i in range(DEPTH):
        kind, j = i % 3, i // 3
        if kind == 0:
            if fft_tabs is None:
                fft_tabs = _fft_tables(S)
            fc, gk, hm = fft_tabs
            w_tail = w_in_a[j].astype(BF16)
            branch = _fft_stage2(_fft_stage1(_proj_a(h, fc), gk), hm)
        elif kind == 1:
            w = w_in_b[j]
            w_qkv = jnp.concatenate([w[:, :qw], _dup_kv_columns(w[:, qw:qw + kvw]),
                                     w[:, qw + kvw:qw + 2 * kvw]], axis=1).astype(BF16)
            cos_t, sin_t = (jnp.asarray(t, F32) for t in _rope_tables_axial(S))
            gq = jnp.tile(q_norm_g[j], 2).reshape(1, LANES)
            gkn = jnp.tile(k_norm_g[j], 2).reshape(1, LANES)
            qk, vt = _proj_b(h, w_qkv, cos_t, sin_t, gq, gkn)
            w_tail = w[:, qw + 2 * kvw:].astype(BF16)
            branch = _gqa_attention(qk, vt)
        else:
            w = w_in_c[j].astype(BF16)
            cos_t, sin_t = _rope_tables_1d(S)
            qkvs = [_proj_c(h, w, cos_t, sin_t, g, dil) for g, dil in enumerate(DIL_DILATIONS)]
            w_tail = w[:, w.shape[1] - TAIL:]
            branch = _dilated_attention(qkvs)
        h = _epilogue(branch, h, w_tail, mk, mv, w_out[i].astype(BF16), ln_g[i], ln_b[i])
    return h[None]
```

```python
import functools

import jax
import jax.numpy as jnp
import numpy as np
from jax import lax
from jax.experimental import pallas as pl
from jax.experimental.pallas import tpu as pltpu

F32 = jnp.float32
BF16 = jnp.bfloat16

D_MODEL = 1024
DEPTH = 4
N_MEM = 256
GRID_W = 64
HEAD_DIM = 64
ROPE_THETA = 10000.0
LN_EPS = 1e-5
RMS_EPS = 1e-6
MASK_VALUE = -1e30
FNET_GROUPS = 4
FNET_GROUP_W = D_MODEL // FNET_GROUPS
GQA_Q_HEADS = 16
GQA_KV_HEADS = 4
DIL_DILATIONS = (1, 4, 16)
DIL_SIDE = 64
MEM_WIDTH = 256
INNER = D_MODEL + MEM_WIDTH
TAIL = INNER + MEM_WIDTH
DEEPNORM_ALPHA = (2.0 * DEPTH) ** 0.25
ATTN_SCALE = HEAD_DIM ** -0.5
LOG2_E = float(np.log2(np.e))
VT_ROWS = HEAD_DIM + 16
LANES = 128
SUBLANES = 8
FFT_N2 = 256
DIL_CHUNK = 1024
VMEM_LIMIT = 48 << 20


def _cparams(*sem):
    return pltpu.CompilerParams(dimension_semantics=sem, vmem_limit_bytes=VMEM_LIMIT)


def _dot(a, b):
    return jnp.dot(a, b, preferred_element_type=F32)


def _dot_nt(a, b):
    return lax.dot_general(a, b, (((1,), (1,)), ((), ())), preferred_element_type=F32)


def _lane_lo(shape=(1, LANES)):
    return lax.broadcasted_iota(jnp.int32, shape, len(shape) - 1) % LANES < HEAD_DIM


def _layer_norm_rows(z, g, b):
    mu = jnp.mean(z, axis=-1, keepdims=True)
    zc = z - mu
    var = jnp.mean(zc * zc, axis=-1, keepdims=True)
    return zc * lax.rsqrt(var + LN_EPS) * g + b


def _ln_kernel(x_ref, g_ref, b_ref, o_ref):
    o_ref[...] = _layer_norm_rows(x_ref[...], g_ref[...], b_ref[...])


def _ln_in(x, g, b, tr=512):
    S = x.shape[0]
    return pl.pallas_call(
        _ln_kernel,
        grid=(S // tr,),
        in_specs=[pl.BlockSpec((tr, D_MODEL), lambda i: (i, 0)),
                  pl.BlockSpec((1, D_MODEL), lambda i: (0, 0)),
                  pl.BlockSpec((1, D_MODEL), lambda i: (0, 0))],
        out_specs=pl.BlockSpec((tr, D_MODEL), lambda i: (i, 0)),
        out_shape=jax.ShapeDtypeStruct((S, D_MODEL), F32),
        compiler_params=_cparams("parallel"),
        name="ln_in",
    )(x, g.reshape(1, -1), b.reshape(1, -1))


def _proj_kernel(x_ref, w_ref, o_ref):
    o_ref[...] = _dot(x_ref[...].astype(BF16), w_ref[...])


def _proj(x, w, tr=512, name="proj"):
    S, K = x.shape
    N = w.shape[1]
    tr = min(tr, S)
    return pl.pallas_call(
        _proj_kernel,
        grid=(S // tr,),
        in_specs=[pl.BlockSpec((tr, K), lambda i: (i, 0)),
                  pl.BlockSpec((K, N), lambda i: (0, 0))],
        out_specs=pl.BlockSpec((tr, N), lambda i: (i, 0)),
        out_shape=jax.ShapeDtypeStruct((S, N), F32),
        compiler_params=_cparams("parallel"),
        name=name,
    )(x, w)


def _proj_a_kernel(x_ref, fc_ref, z_ref):
    xb = x_ref[...].astype(BF16)
    for g in range(FNET_GROUPS):
        cols = slice(g * FNET_GROUP_W, (g + 1) * FNET_GROUP_W)
        zg = _dot(xb[:, cols], fc_ref[...])
        z_ref[0, :, cols] = zg[:, :FNET_GROUP_W]
        z_ref[1, :, cols] = zg[:, FNET_GROUP_W:]


def _proj_a(h, fc, tr=512):
    S = h.shape[0]
    return pl.pallas_call(
        _proj_a_kernel,
        grid=(S // tr,),
        in_specs=[pl.BlockSpec((tr, D_MODEL), lambda i: (i, 0)),
                  pl.BlockSpec((FNET_GROUP_W, 2 * FNET_GROUP_W), lambda i: (0, 0))],
        out_specs=pl.BlockSpec((2, tr, D_MODEL), lambda i: (0, i, 0)),
        out_shape=jax.ShapeDtypeStruct((2, S, D_MODEL), F32),
        compiler_params=_cparams("parallel"),
        name="proj_a",
    )(h, fc)


def _fft1_kernel(z_ref, g_ref, t_ref):
    two, n1, _, sub, tc = z_ref.shape
    x = z_ref[...].reshape(two * n1 * sub, tc).astype(BF16)
    t_ref[...] = _dot(g_ref[...], x).reshape(t_ref.shape)


def _fft_stage1(z, gk, tc=1024):
    _, S, D = z.shape
    n1 = S // FFT_N2
    nu = FFT_N2 // SUBLANES
    z5 = z.reshape(2, n1, nu, SUBLANES, D)
    rows = 2 * n1 * SUBLANES
    t5 = pl.pallas_call(
        _fft1_kernel,
        grid=(nu, D // tc),
        in_specs=[pl.BlockSpec((2, n1, 1, SUBLANES, tc), lambda u, c: (0, 0, u, 0, c)),
                  pl.BlockSpec((rows, rows), lambda u, c: (0, 0))],
        out_specs=pl.BlockSpec((n1, 2, 1, SUBLANES, tc), lambda u, c: (0, 0, u, 0, c)),
        out_shape=jax.ShapeDtypeStruct((n1, 2, nu, SUBLANES, D), F32),
        compiler_params=_cparams("parallel", "parallel"),
        name="fft_stage1",
    )(z5, gk)
    return t5.reshape(n1, 2, FFT_N2, D)


def _fft2_kernel(t_ref, h_ref, o_ref, slab_ref):
    nj, _, n2, d = t_ref.shape
    nslab = d // LANES
    for j in range(nj):
        tj = t_ref[j].reshape(2 * n2, d).astype(BF16)
        r = _dot(h_ref[j], tj)
        for s in range(nslab):
            slab_ref[s, pl.ds(j, n2, stride=nj), :] = r[:, s * LANES:(s + 1) * LANES]
    for s in range(nslab):
        o_ref[:, :, s * LANES:(s + 1) * LANES] = slab_ref[s].reshape(n2, nj, LANES)


def _fft_stage2(t, hmat, tc=512):
    n1, _, n2, D = t.shape
    nj = SUBLANES
    y3 = pl.pallas_call(
        _fft2_kernel,
        grid=(n1 // nj, D // tc),
        in_specs=[pl.BlockSpec((nj, 2, n2, tc), lambda a, c: (a, 0, 0, c)),
                  pl.BlockSpec((nj, n2, 2 * n2), lambda a, c: (a, 0, 0))],
        out_specs=pl.BlockSpec((n2, nj, tc), lambda a, c: (0, a, c)),
        out_shape=jax.ShapeDtypeStruct((n2, n1, D), F32),
        scratch_shapes=[pltpu.VMEM((tc // LANES, n2 * nj, LANES), F32)],
        compiler_params=_cparams("parallel", "parallel"),
        name="fft_stage2",
    )(t, hmat)
    return y3.reshape(n2 * n1, D)


def _fft_tables(S):
    n1, n2 = S // FFT_N2, FFT_N2
    c = np.arange(FNET_GROUP_W)
    ang = (2.0 * np.pi / FNET_GROUP_W) * ((c[:, None] * c[None, :]) % FNET_GROUP_W)
    scale = 1.0 / np.sqrt(float(S) * FNET_GROUP_W)
    fc = np.concatenate([np.cos(ang), -np.sin(ang)], axis=1) * scale
    k1 = np.arange(n1)
    th = (2.0 * np.pi / n1) * ((k1[:, None] * k1[None, :]) % n1)
    cs, sn = np.cos(th), np.sin(th)
    g = np.stack([np.stack([cs, sn], axis=1), np.stack([-sn, cs], axis=1)], axis=1)
    gk = np.kron(g.reshape(2 * n1, 2 * n1), np.eye(SUBLANES))
    k2 = np.arange(n2)
    kk = k1[:, None, None] + n1 * k2[None, :, None]
    ph = (2.0 * np.pi / S) * ((k2[None, None, :] * kk) % S)
    hm = np.concatenate([np.cos(ph), np.sin(ph)], axis=2)
    return tuple(jnp.asarray(t, F32).astype(BF16) for t in (fc, gk, hm))


def _rope_angles(pos, dim):
    inv_freq = ROPE_THETA ** (-(np.arange(0, dim, 2, dtype=np.float64) / dim))
    return pos.astype(np.float64)[:, None] * inv_freq[None, :]


def _rope_tables_axial(S):
    t = np.arange(S)
    ar = _rope_angles(t // GRID_W, HEAD_DIM // 2)
    ac = _rope_angles(t % GRID_W, HEAD_DIM // 2)
    cos = np.concatenate([np.cos(ar), np.cos(ar), np.cos(ac), np.cos(ac)], axis=1)
    sin = np.concatenate([-np.sin(ar), np.sin(ar), -np.sin(ac), np.sin(ac)], axis=1)
    return np.tile(cos, (1, 2)), np.tile(sin, (1, 2))


def _rope_tables_1d(S):
    a = _rope_angles(np.arange(S), HEAD_DIM)
    cos = np.concatenate([np.cos(a), np.cos(a)], axis=1)
    sin = np.concatenate([-np.sin(a), np.sin(a)], axis=1)
    return np.tile(cos, (1, 2)), np.tile(sin, (1, 2))


def _qk_lane_order(half):
    d = np.arange(HEAD_DIM)
    first = d[d % (2 * half) < half]
    return np.concatenate([first, HEAD_DIM + first, first + half, HEAD_DIM + first + half])


def _permute_pairs(a, src):
    lead = a.shape[:-1]
    return a.reshape(*lead, a.shape[-1] // LANES, LANES)[..., src].reshape(*lead, a.shape[-1])


def _lane_qk():
    return lax.broadcasted_iota(jnp.int32, (1, LANES), 1) % HEAD_DIM < HEAD_DIM // 2


def _rotate_partner(x):
    return pltpu.roll(x, HEAD_DIM, 1)


def _proj_b_kernel(x_ref, w_ref, cos_ref, sin_ref, gq_ref, gk_ref, o_ref, vt_ref):
    xb = x_ref[...].astype(BF16)
    cos_t, sin_t = cos_ref[...], sin_ref[...]
    lo = _lane_qk()
    nb_w = 4 * LANES
    vt = _dot(xb, w_ref[:, 3 * nb_w:]).T.astype(BF16)
    ones = jnp.ones((VT_ROWS - HEAD_DIM, vt.shape[1]), BF16)
    for hd in range(GQA_KV_HEADS):
        vt_ref[hd * VT_ROWS:hd * VT_ROWS + HEAD_DIM, :] = vt[hd * HEAD_DIM:(hd + 1) * HEAD_DIM]
        vt_ref[hd * VT_ROWS + HEAD_DIM:(hd + 1) * VT_ROWS, :] = ones
    for nb in range(3):
        r = _dot(xb, w_ref[:, nb * nb_w:(nb + 1) * nb_w])
        gain = gq_ref[...] if nb < 2 else gk_ref[...]
        scale = ATTN_SCALE * LOG2_E if nb < 2 else 1.0
        for t in range(4):
            rt = r[:, t * LANES:(t + 1) * LANES]
            r2 = rt * rt
            tot = jnp.sum(r2, axis=1, keepdims=True)
            low = jnp.sum(jnp.where(lo, r2, 0.0), axis=1, keepdims=True)
            ss = jnp.where(lo, low, tot - low)
            xn = rt * lax.rsqrt(ss * (1.0 / HEAD_DIM) + RMS_EPS) * gain
            out = (xn * cos_t + _rotate_partner(xn) * sin_t) * scale
            c0 = nb * nb_w + t * LANES
            o_ref[:, c0:c0 + LANES] = out.astype(BF16)


def _proj_b(h, w_qkv, cos_t, sin_t, gq, gk, tr=512):
    S = h.shape[0]
    n_in = w_qkv.shape[1]
    n_out = n_in - GQA_KV_HEADS * HEAD_DIM
    vw = GQA_KV_HEADS * VT_ROWS
    return pl.pallas_call(
        _proj_b_kernel,
        grid=(S // tr,),
        in_specs=[pl.BlockSpec((tr, D_MODEL), lambda i: (i, 0)),
                  pl.BlockSpec((D_MODEL, n_in), lambda i: (0, 0)),
                  pl.BlockSpec((tr, LANES), lambda i: (i, 0)),
                  pl.BlockSpec((tr, LANES), lambda i: (i, 0)),
                  pl.BlockSpec((1, LANES), lambda i: (0, 0)),
                  pl.BlockSpec((1, LANES), lambda i: (0, 0))],
        out_specs=[pl.BlockSpec((tr, n_out), lambda i: (i, 0)),
                   pl.BlockSpec((vw, tr), lambda i: (0, i))],
        out_shape=[jax.ShapeDtypeStruct((S, n_out), BF16),
                   jax.ShapeDtypeStruct((vw, S), BF16)],
        compiler_params=_cparams("parallel"),
        name="proj_b",
    )(h, w_qkv, cos_t, sin_t, gq, gk)


def _gqa_kernel(q_ref, k_ref, vt_ref, o_ref, qs_ref, m_ref, acc_ref, *p_refs, tk):
    tq = q_ref.shape[0]
    S = k_ref.shape[0]
    nq = 4 * tq
    nkv = S // tk
    nbuf = len(p_refs)
    lo = _lane_qk()
    for t in range(2):
        qt = q_ref[:, t * LANES:(t + 1) * LANES].astype(F32)
        qs_ref[(2 * t) * tq:(2 * t + 1) * tq, :] = jnp.where(lo, qt, 0.0).astype(BF16)
        qs_ref[(2 * t + 1) * tq:(2 * t + 2) * tq, :] = jnp.where(lo, 0.0, qt).astype(BF16)

    def scores(j):
        k0 = pl.multiple_of(j * tk, tk)
        return _dot_nt(k_ref[pl.ds(k0, tk), :], qs_ref[...])

    def values(j, p):
        return _dot(vt_ref[:, pl.ds(pl.multiple_of(j * tk, tk), tk)], p)

    def colmax(s):
        return jnp.max(jnp.max(s.reshape(tk // SUBLANES, SUBLANES, nq), axis=0), axis=0, keepdims=True)

    m0 = jnp.broadcast_to(colmax(scores(0)), (tk, nq))

    def probs(u):
        p_refs[u % nbuf][...] = jnp.exp2(scores(u) - m0).astype(BF16)

    probs(0)
    pv = None
    for u in range(nkv):
        if u + 1 < nkv:
            probs(u + 1)
        d = values(u, p_refs[u % nbuf][...])
        pv = d if pv is None else pv + d
    safe = jnp.min(jnp.where(jnp.isfinite(pv), 1.0, 0.0)) > 0.5

    @pl.when(safe)
    def _():
        acc_ref[...] = pv

    @pl.when(jnp.logical_not(safe))
    def _():
        m_ref[...] = jnp.full(m_ref.shape, -jnp.inf, F32)
        acc_ref[...] = jnp.zeros(acc_ref.shape, F32)

        def exact_tile(u, c):
            s = scores(u)
            m_prev = m_ref[...]
            m_cur = jnp.maximum(m_prev, colmax(s))
            d = values(u, jnp.exp2(s - m_cur).astype(BF16))
            acc_ref[...] = jnp.exp2(m_prev - m_cur) * acc_ref[...] + d
            m_ref[...] = m_cur
            return c

        lax.fori_loop(0, nkv, exact_tile, 0)

    acc = acc_ref[...]
    ot = acc[:HEAD_DIM] / acc[HEAD_DIM:HEAD_DIM + 1]
    for t in range(2):
        pair = jnp.concatenate([ot[:, (2 * t) * tq:(2 * t + 1) * tq],
                                ot[:, (2 * t + 1) * tq:(2 * t + 2) * tq]], axis=0)
        o_ref[:, t * LANES:(t + 1) * LANES] = pair.T


def _gqa_attention(qk, vt, tq=256, tk=256, nbuf=2):
    S = qk.shape[0]
    qw = 4 * HEAD_DIM
    k_blk0 = GQA_Q_HEADS * HEAD_DIM // LANES
    tk = min(tk, S)
    return pl.pallas_call(
        functools.partial(_gqa_kernel, tk=tk),
        grid=(GQA_KV_HEADS, S // tq),
        in_specs=[pl.BlockSpec((tq, qw), lambda h, i: (i, h)),
                  pl.BlockSpec((S, LANES), lambda h, i: (0, k_blk0 + h)),
                  pl.BlockSpec((VT_ROWS, S), lambda h, i: (h, 0))],
        out_specs=pl.BlockSpec((tq, qw), lambda h, i: (i, h)),
        out_shape=jax.ShapeDtypeStruct((S, D_MODEL), F32),
        scratch_shapes=[pltpu.VMEM((4 * tq, LANES), BF16),
                        pltpu.VMEM((1, 4 * tq), F32),
                        pltpu.VMEM((VT_ROWS, 4 * tq), F32)]
        + [pltpu.VMEM((tk, 4 * tq), BF16)] * nbuf,
        compiler_params=_cparams("parallel", "parallel"),
        name="gqa_attention",
    )(qk, qk, vt)


def _proj_c_kernel(x_ref, w_ref, cos_ref, sin_ref, o_ref, xp_ref, slab_ref, *, dil):
    kind = pl.program_id(1)
    tm, d = x_ref.shape[0], o_ref.shape[-1]
    n = tm // dil
    nslab = d // LANES
    cb_w = 2 * LANES

    @pl.when(kind == 0)
    def _():
        if dil == 1:
            xp_ref[...] = x_ref[...].astype(BF16)
            return
        for s in range(nslab):
            slab_ref[s] = x_ref[:, s * LANES:(s + 1) * LANES]
        for c in range(dil):
            for s in range(nslab):
                xp_ref[c * n:(c + 1) * n, s * LANES:(s + 1) * LANES] = (
                    slab_ref[s, pl.ds(c, n, stride=dil), :].astype(BF16))

    def emit(c0, vals):
        o_ref[0, 0, :, :, c0:c0 + vals.shape[1]] = vals.astype(BF16).reshape(dil, n, vals.shape[1])

    @pl.when(kind < 2)
    def _():
        scale = jnp.where(kind == 0, ATTN_SCALE * LOG2_E, 1.0).astype(F32)
        cos_t = cos_ref[...] * scale
        sin_t = sin_ref[...] * scale
        ncb = d // cb_w
        r_next = _dot(xp_ref[...], w_ref[:, :cb_w])
        for cb in range(ncb):
            r = r_next
            if cb + 1 < ncb:
                r_next = _dot(xp_ref[...], w_ref[:, (cb + 1) * cb_w:(cb + 2) * cb_w])
            for s in range(cb_w // LANES):
                rt = r[:, s * LANES:(s + 1) * LANES]
                emit(cb * cb_w + s * LANES, rt * cos_t + _rotate_partner(rt) * sin_t)

    @pl.when(kind == 2)
    def _():
        for cb in range(d // cb_w):
            emit(cb * cb_w, _dot(xp_ref[...], w_ref[:, cb * cb_w:(cb + 1) * cb_w]))


def _regroup_rows(tab, dil):
    S, w = tab.shape
    return tab.reshape(S // DIL_CHUNK, DIL_CHUNK // dil, dil, w).transpose(0, 2, 1, 3).reshape(S, w)


def _proj_c(h, w, cos_np, sin_np, group, dil):
    S = h.shape[0]
    tm = DIL_CHUNK
    n = tm // dil
    ngroups = len(DIL_DILATIONS)
    cos_t = jnp.asarray(_regroup_rows(cos_np, dil), F32)
    sin_t = jnp.asarray(_regroup_rows(sin_np, dil), F32)
    return pl.pallas_call(
        functools.partial(_proj_c_kernel, dil=dil),
        grid=(S // tm, 3),
        in_specs=[pl.BlockSpec((tm, D_MODEL), lambda i, t: (i, 0)),
                  pl.BlockSpec((D_MODEL, D_MODEL), lambda i, t: (0, ngroups * t + group)),
                  pl.BlockSpec((tm, LANES), lambda i, t: (i, 0)),
                  pl.BlockSpec((tm, LANES), lambda i, t: (i, 0))],
        out_specs=pl.BlockSpec((1, 1, dil, n, D_MODEL), lambda i, t: (t, i, 0, 0, 0)),
        out_shape=jax.ShapeDtypeStruct((3, S // tm, dil, n, D_MODEL), BF16),
        scratch_shapes=[pltpu.VMEM((tm, D_MODEL), BF16),
                        pltpu.VMEM((D_MODEL // LANES, tm, LANES), F32)],
        compiler_params=_cparams("parallel", "arbitrary"),
        name=f"proj_c_dil{dil}",
    )(h, w, cos_t, sin_t)


def _dil_kernel(*refs, nchunks):
    ng = len(DIL_DILATIONS)
    o_ref, osc_ref, lsc_ref = refs[7 * ng:]
    i = pl.program_id(0)
    lo = _lane_lo()
    lo_qk = _lane_qk()

    groups, blocks = [], []
    for g, dil in enumerate(DIL_DILATIONS):
        n = DIL_CHUNK // dil
        bq = min(n, 2 * DIL_SIDE)
        nk = bq + 2 * DIL_SIDE
        qi = lax.broadcasted_iota(jnp.int32, (2 * bq, nk), 0) % bq
        kj = lax.broadcasted_iota(jnp.int32, (2 * bq, nk), 1)
        band = jnp.where(jnp.abs(kj - DIL_SIDE - qi) <= DIL_SIDE, 0.0, MASK_VALUE).astype(F32)
        groups.append(dict(dil=dil, n=n, bq=bq, nk=nk, nsub=n // bq, kj=kj, band=band,
                           kpad=-(n + 2 * DIL_SIDE) % LANES, refs=refs[7 * g:7 * g + 7]))
        blocks += [(g, c, b) for c in range(dil) for b in range(n // bq)]

    class_kv = {}

    def keys_values(g, c):
        if (g, c) not in class_kv:
            G = groups[g]
            _, kc_ref, kp_ref, kn_ref, vc_ref, vp_ref, vn_ref = G["refs"]
            kparts = [kp_ref[0, 0, c], kc_ref[0, 0, c], kn_ref[0, 0, c]]
            if G["kpad"]:
                kparts.append(jnp.zeros((G["kpad"], LANES), BF16))
            kcat_t = jnp.concatenate(kparts, axis=0).astype(F32).T.astype(BF16)
            vcat = jnp.concatenate([vp_ref[0, 0, c], vc_ref[0, 0, c], vn_ref[0, 0, c]], axis=0)
            class_kv[(g, c)] = (kcat_t, vcat)
        return class_kv[(g, c)]

    def scores(blk):
        g, c, b = blk
        G = groups[g]
        bq, nk, kj = G["bq"], G["nk"], G["kj"]
        p0 = b * bq
        bias = G["band"]
        if b == 0:
            bias = jnp.where(kj >= jnp.where(i == 0, DIL_SIDE, 0), bias, MASK_VALUE)
        if b == G["nsub"] - 1:
            bias = jnp.where(kj < jnp.where(i == nchunks - 1, bq + DIL_SIDE, nk), bias, MASK_VALUE)
        qf = G["refs"][0][0, 0, c, p0:p0 + bq, :].astype(F32)
        qs = jnp.concatenate([jnp.where(lo_qk, qf, 0.0), jnp.where(lo_qk, 0.0, qf)], axis=0)
        return _dot(qs.astype(BF16), keys_values(g, c)[0][:, p0:p0 + nk]) + bias

    def finish(blk, s):
        g, c, b = blk
        G = groups[g]
        bq, nk, dil = G["bq"], G["nk"], G["dil"]
        p0 = b * bq
        m = jnp.max(s, axis=1, keepdims=True)
        p = jnp.exp2(s - m)
        l = jnp.sum(p, axis=1, keepdims=True)
        o2 = _dot(p.astype(BF16), keys_values(g, c)[1][p0:p0 + nk]) / l
        lse2 = m + jnp.log2(l)
        o_pair = jnp.where(lo, o2[:bq], o2[bq:])
        l_pair = jnp.where(lo, lse2[:bq], lse2[bq:])
        rows = pl.ds(p0, bq) if dil == 1 else pl.ds(p0 * dil + c, bq, stride=dil)
        osc_ref[g, rows, :] = o_pair
        lsc_ref[g, rows, :] = l_pair

    s_next = scores(blocks[0])
    for idx, blk in enumerate(blocks):
        s_cur = s_next
        if idx + 1 < len(blocks):
            s_next = scores(blocks[idx + 1])
        finish(blk, s_cur)

    ls = [lsc_ref[g] for g in range(ng)]
    mx = functools.reduce(jnp.maximum, ls)
    ws = [jnp.exp2(l - mx) for l in ls]
    num = sum(w * osc_ref[g] for g, w in enumerate(ws))
    o_ref[...] = num / sum(ws)


def _dilated_attention(qkvs):
    nchunks = qkvs[0].shape[1]
    S = nchunks * DIL_CHUNK
    in_specs, args = [], []
    for dil, a in zip(DIL_DILATIONS, qkvs):
        n = DIL_CHUNK // dil
        nblk = n // DIL_SIDE
        full = (1, 1, dil, n, LANES)
        halo = (1, 1, dil, DIL_SIDE, LANES)

        def cur(t):
            return lambda i, m: (t, i, 0, 0, m)

        def prev(t, nblk=nblk):
            return lambda i, m: (t, jnp.maximum(i - 1, 0), 0, nblk - 1, m)

        def nxt(t):
            return lambda i, m: (t, jnp.minimum(i + 1, nchunks - 1), 0, 0, m)

        in_specs += [pl.BlockSpec(full, cur(0)),
                     pl.BlockSpec(full, cur(1)), pl.BlockSpec(halo, prev(1)), pl.BlockSpec(halo, nxt(1)),
                     pl.BlockSpec(full, cur(2)), pl.BlockSpec(halo, prev(2)), pl.BlockSpec(halo, nxt(2))]
        args += [a] * 7
    ng = len(DIL_DILATIONS)
    return pl.pallas_call(
        functools.partial(_dil_kernel, nchunks=nchunks),
        grid=(nchunks, D_MODEL // LANES),
        in_specs=in_specs,
        out_specs=pl.BlockSpec((DIL_CHUNK, LANES), lambda i, m: (i, m)),
        out_shape=jax.ShapeDtypeStruct((S, D_MODEL), F32),
        scratch_shapes=[pltpu.VMEM((ng, DIL_CHUNK, LANES), F32),
                        pltpu.VMEM((ng, DIL_CHUNK, LANES), F32)],
        compiler_params=_cparams("parallel", "parallel"),
        name="dilated_attention",
    )(*args)


def _epilogue_kernel(br_ref, h_ref, wt_ref, mk_ref, mv_ref, wo_ref, g_ref, b_ref, o_ref, *, nsplit):
    lo = _lane_lo()
    tr = h_ref.shape[0]
    rs = tr // nsplit

    def pre_norm(rows):
        h = h_ref[rows, :]
        hb = h.astype(BF16)
        gate_b = _dot(hb, wt_ref[:, :D_MODEL])
        yb = (br_ref[rows, :] * (gate_b * jax.nn.sigmoid(gate_b))).astype(BF16)
        y = _dot(yb, wo_ref[:D_MODEL, :])
        tail_m = _dot(hb, wt_ref[:, D_MODEL:])
        for t in range(MEM_WIDTH // LANES):
            cols = slice(t * LANES, (t + 1) * LANES)
            qf = tail_m[:, MEM_WIDTH + t * LANES:MEM_WIDTH + (t + 1) * LANES] * ATTN_SCALE
            kt, vt = mk_ref[:, cols], mv_ref[:, cols]
            outs = []
            for qh in (jnp.where(lo, qf, 0.0), jnp.where(lo, 0.0, qf)):
                s = _dot_nt(qh.astype(BF16), kt)
                m = jnp.max(s, axis=1, keepdims=True)
                p = jnp.exp(s - m)
                l = jnp.sum(p, axis=1, keepdims=True)
                outs.append(_dot(p.astype(BF16), vt) / l)
            mem_out = jnp.where(lo, outs[0], outs[1])
            gm = tail_m[:, cols]
            ym = (mem_out * (gm * jax.nn.sigmoid(gm))).astype(BF16)
            y = y + _dot(ym, wo_ref[D_MODEL + t * LANES:D_MODEL + (t + 1) * LANES, :])
        return DEEPNORM_ALPHA * h + y

    z_next = pre_norm(pl.ds(0, rs))
    for r in range(nsplit):
        z = z_next
        if r + 1 < nsplit:
            z_next = pre_norm(pl.ds((r + 1) * rs, rs))
        o_ref[pl.ds(r * rs, rs), :] = _layer_norm_rows(z, g_ref[...], b_ref[...])


def _epilogue(branch, h, w_tail, mk, mv, w_out, g, b, tr=1024, nsplit=2):
    S = h.shape[0]
    row = lambda i: (i, 0)
    fixed = lambda i: (0, 0)
    return pl.pallas_call(
        functools.partial(_epilogue_kernel, nsplit=nsplit),
        grid=(S // tr,),
        in_specs=[pl.BlockSpec((tr, D_MODEL), row),
                  pl.BlockSpec((tr, D_MODEL), row),
                  pl.BlockSpec((D_MODEL, TAIL), fixed),
                  pl.BlockSpec((N_MEM, MEM_WIDTH), fixed),
                  pl.BlockSpec((N_MEM, MEM_WIDTH), fixed),
                  pl.BlockSpec((INNER, D_MODEL), fixed),
                  pl.BlockSpec((1, D_MODEL), fixed),
                  pl.BlockSpec((1, D_MODEL), fixed)],
        out_specs=pl.BlockSpec((tr, D_MODEL), row),
        out_shape=jax.ShapeDtypeStruct((S, D_MODEL), F32),
        compiler_params=_cparams("parallel"),
        name="epilogue",
    )(branch, h, w_tail, mk, mv, w_out, g.reshape(1, -1), b.reshape(1, -1))


def _dup_kv_columns(w):
    d = w.shape[0]
    w4 = w.reshape(d, GQA_KV_HEADS, 1, HEAD_DIM)
    return jnp.broadcast_to(w4, (d, GQA_KV_HEADS, 2, HEAD_DIM)).reshape(d, 2 * GQA_KV_HEADS * HEAD_DIM)


def kernel(x, mem, ln_in_g, ln_in_b, w_mem_kv, w_in_a, w_in_b, q_norm_g, k_norm_g, w_in_c, w_out, ln_g, ln_b):
    B, S, D = x.shape
    assert B == 1 and D == D_MODEL and S % DIL_CHUNK == 0 and S % (FFT_N2 * SUBLANES) == 0

    mkv = _proj(mem[0], w_mem_kv.astype(BF16), name="proj_mem").astype(BF16)
    mk, mv = mkv[:, :MEM_WIDTH], mkv[:, MEM_WIDTH:]
    h = _ln_in(x[0], ln_in_g, ln_in_b)
    fft_tabs = None
    qw = GQA_Q_HEADS * HEAD_DIM
    kvw = GQA_KV_HEADS * HEAD_DIM

    for i in range(DEPTH):
        kind, j = i % 3, i // 3
        if kind == 0:
            if fft_tabs is None:
                fft_tabs = _fft_tables(S)
            fc, gk, hm = fft_tabs
            w_tail = w_in_a[j].astype(BF16)
            branch = _fft_stage2(_fft_stage1(_proj_a(h, fc), gk), hm)
        elif kind == 1:
            w = w_in_b[j]
            src = _qk_lane_order(HEAD_DIM // 4)
            w_qk = jnp.concatenate([w[:, :qw], _dup_kv_columns(w[:, qw:qw + kvw])], axis=1)
            w_qkv = jnp.concatenate([_permute_pairs(w_qk, src), w[:, qw + kvw:qw + 2 * kvw]], axis=1).astype(BF16)
            cos_t, sin_t = (jnp.asarray(_permute_pairs(t, src), F32) for t in _rope_tables_axial(S))
            gq = _permute_pairs(jnp.tile(q_norm_g[j], 2), src).reshape(1, LANES)
            gkn = _permute_pairs(jnp.tile(k_norm_g[j], 2), src).reshape(1, LANES)
            qk, vt = _proj_b(h, w_qkv, cos_t, sin_t, gq, gkn)
            w_tail = w[:, qw + 2 * kvw:].astype(BF16)
            branch = _gqa_attention(qk, vt)
        else:
            w = w_in_c[j].astype(BF16)
            src = _qk_lane_order(HEAD_DIM // 2)
            n_qk = 2 * len(DIL_DILATIONS) * D_MODEL
            w = jnp.concatenate([_permute_pairs(w[:, :n_qk], src), w[:, n_qk:]], axis=1)
            cos_t, sin_t = (_permute_pairs(t, src) for t in _rope_tables_1d(S))
            qkvs = [_proj_c(h, w, cos_t, sin_t, g, dil) for g, dil in enumerate(DIL_DILATIONS)]
            w_tail = w[:, w.shape[1] - TAIL:]
            branch = _dilated_attention(qkvs)
        h = _epilogue(branch, h, w_tail, mk, mv, w_out[i].astype(BF16), ln_g[i], ln_b[i])
    return h[None]
```

```python
import functools

import jax
import jax.numpy as jnp
import numpy as np
from jax import lax
from jax.experimental import pallas as pl
from jax.experimental.pallas import tpu as pltpu

F32 = jnp.float32
BF16 = jnp.bfloat16

D_MODEL = 1024
DEPTH = 4
N_MEM = 256
GRID_W = 64
HEAD_DIM = 64
ROPE_THETA = 10000.0
LN_EPS = 1e-5
RMS_EPS = 1e-6
MASK_VALUE = -1e30
FNET_GROUPS = 4
FNET_GROUP_W = D_MODEL // FNET_GROUPS
GQA_Q_HEADS = 16
GQA_KV_HEADS = 4
DIL_DILATIONS = (1, 4, 16)
DIL_SIDE = 64
MEM_WIDTH = 256
INNER = D_MODEL + MEM_WIDTH
TAIL = INNER + MEM_WIDTH
DEEPNORM_ALPHA = (2.0 * DEPTH) ** 0.25
ATTN_SCALE = HEAD_DIM ** -0.5
LOG2_E = float(np.log2(np.e))
VT_ROWS = HEAD_DIM + 16
LANES = 128
SUBLANES = 8
FFT_N2 = 256
DIL_CHUNK = 1024
VMEM_LIMIT = 48 << 20


def _cparams(*sem):
    return pltpu.CompilerParams(dimension_semantics=sem, vmem_limit_bytes=VMEM_LIMIT)


def _dot(a, b):
    return jnp.dot(a, b, preferred_element_type=F32)


def _dot_nt(a, b):
    return lax.dot_general(a, b, (((1,), (1,)), ((), ())), preferred_element_type=F32)


def _lane_lo(shape=(1, LANES)):
    return lax.broadcasted_iota(jnp.int32, shape, len(shape) - 1) % LANES < HEAD_DIM


def _layer_norm_rows(z, g, b):
    mu = jnp.mean(z, axis=-1, keepdims=True)
    zc = z - mu
    var = jnp.mean(zc * zc, axis=-1, keepdims=True)
    return zc * lax.rsqrt(var + LN_EPS) * g + b


def _ln_kernel(x_ref, g_ref, b_ref, o_ref):
    o_ref[...] = _layer_norm_rows(x_ref[...], g_ref[...], b_ref[...])


def _ln_in(x, g, b, tr=512):
    S = x.shape[0]
    return pl.pallas_call(
        _ln_kernel,
        grid=(S // tr,),
        in_specs=[pl.BlockSpec((tr, D_MODEL), lambda i: (i, 0)),
                  pl.BlockSpec((1, D_MODEL), lambda i: (0, 0)),
                  pl.BlockSpec((1, D_MODEL), lambda i: (0, 0))],
        out_specs=pl.BlockSpec((tr, D_MODEL), lambda i: (i, 0)),
        out_shape=jax.ShapeDtypeStruct((S, D_MODEL), F32),
        compiler_params=_cparams("parallel"),
        name="ln_in",
    )(x, g.reshape(1, -1), b.reshape(1, -1))


def _proj_kernel(x_ref, w_ref, o_ref):
    o_ref[...] = _dot(x_ref[...].astype(BF16), w_ref[...])


def _proj(x, w, tr=512, name="proj"):
    S, K = x.shape
    N = w.shape[1]
    tr = min(tr, S)
    return pl.pallas_call(
        _proj_kernel,
        grid=(S // tr,),
        in_specs=[pl.BlockSpec((tr, K), lambda i: (i, 0)),
                  pl.BlockSpec((K, N), lambda i: (0, 0))],
        out_specs=pl.BlockSpec((tr, N), lambda i: (i, 0)),
        out_shape=jax.ShapeDtypeStruct((S, N), F32),
        compiler_params=_cparams("parallel"),
        name=name,
    )(x, w)


def _proj_a_kernel(x_ref, fc_ref, z_ref):
    xb = x_ref[...].astype(BF16)
    for g in range(FNET_GROUPS):
        cols = slice(g * FNET_GROUP_W, (g + 1) * FNET_GROUP_W)
        zg = _dot(xb[:, cols], fc_ref[...])
        z_ref[0, :, cols] = zg[:, :FNET_GROUP_W]
        z_ref[1, :, cols] = zg[:, FNET_GROUP_W:]


def _proj_a(h, fc, tr=512):
    S = h.shape[0]
    return pl.pallas_call(
        _proj_a_kernel,
        grid=(S // tr,),
        in_specs=[pl.BlockSpec((tr, D_MODEL), lambda i: (i, 0)),
                  pl.BlockSpec((FNET_GROUP_W, 2 * FNET_GROUP_W), lambda i: (0, 0))],
        out_specs=pl.BlockSpec((2, tr, D_MODEL), lambda i: (0, i, 0)),
        out_shape=jax.ShapeDtypeStruct((2, S, D_MODEL), F32),
        compiler_params=_cparams("parallel"),
        name="proj_a",
    )(h, fc)


def _fft1_kernel(z_ref, g_ref, t_ref):
    two, n1, _, sub, tc = z_ref.shape
    x = z_ref[...].reshape(two * n1 * sub, tc).astype(BF16)
    t_ref[...] = _dot(g_ref[...], x).reshape(t_ref.shape)


def _fft_stage1(z, gk, tc=1024):
    _, S, D = z.shape
    n1 = S // FFT_N2
    nu = FFT_N2 // SUBLANES
    z5 = z.reshape(2, n1, nu, SUBLANES, D)
    rows = 2 * n1 * SUBLANES
    t5 = pl.pallas_call(
        _fft1_kernel,
        grid=(nu, D // tc),
        in_specs=[pl.BlockSpec((2, n1, 1, SUBLANES, tc), lambda u, c: (0, 0, u, 0, c)),
                  pl.BlockSpec((rows, rows), lambda u, c: (0, 0))],
        out_specs=pl.BlockSpec((n1, 2, 1, SUBLANES, tc), lambda u, c: (0, 0, u, 0, c)),
        out_shape=jax.ShapeDtypeStruct((n1, 2, nu, SUBLANES, D), F32),
        compiler_params=_cparams("parallel", "parallel"),
        name="fft_stage1",
    )(z5, gk)
    return t5.reshape(n1, 2, FFT_N2, D)


def _fft2_kernel(t_ref, h_ref, o_ref, slab_ref):
    nj, _, n2, d = t_ref.shape
    nslab = d // LANES
    for j in range(nj):
        tj = t_ref[j].reshape(2 * n2, d).astype(BF16)
        r = _dot(h_ref[j], tj)
        for s in range(nslab):
            slab_ref[s, pl.ds(j, n2, stride=nj), :] = r[:, s * LANES:(s + 1) * LANES]
    for s in range(nslab):
        o_ref[:, :, s * LANES:(s + 1) * LANES] = slab_ref[s].reshape(n2, nj, LANES)


def _fft_stage2(t, hmat, tc=512):
    n1, _, n2, D = t.shape
    nj = SUBLANES
    y3 = pl.pallas_call(
        _fft2_kernel,
        grid=(n1 // nj, D // tc),
        in_specs=[pl.BlockSpec((nj, 2, n2, tc), lambda a, c: (a, 0, 0, c)),
                  pl.BlockSpec((nj, n2, 2 * n2), lambda a, c: (a, 0, 0))],
        out_specs=pl.BlockSpec((n2, nj, tc), lambda a, c: (0, a, c)),
        out_shape=jax.ShapeDtypeStruct((n2, n1, D), F32),
        scratch_shapes=[pltpu.VMEM((tc // LANES, n2 * nj, LANES), F32)],
        compiler_params=_cparams("parallel", "parallel"),
        name="fft_stage2",
    )(t, hmat)
    return y3.reshape(n2 * n1, D)


def _fft_tables(S):
    n1, n2 = S // FFT_N2, FFT_N2
    c = np.arange(FNET_GROUP_W)
    ang = (2.0 * np.pi / FNET_GROUP_W) * ((c[:, None] * c[None, :]) % FNET_GROUP_W)
    scale = 1.0 / np.sqrt(float(S) * FNET_GROUP_W)
    fc = np.concatenate([np.cos(ang), -np.sin(ang)], axis=1) * scale
    k1 = np.arange(n1)
    th = (2.0 * np.pi / n1) * ((k1[:, None] * k1[None, :]) % n1)
    cs, sn = np.cos(th), np.sin(th)
    g = np.stack([np.stack([cs, sn], axis=1), np.stack([-sn, cs], axis=1)], axis=1)
    gk = np.kron(g.reshape(2 * n1, 2 * n1), np.eye(SUBLANES))
    k2 = np.arange(n2)
    kk = k1[:, None, None] + n1 * k2[None, :, None]
    ph = (2.0 * np.pi / S) * ((k2[None, None, :] * kk) % S)
    hm = np.concatenate([np.cos(ph), np.sin(ph)], axis=2)
    return tuple(jnp.asarray(t, F32).astype(BF16) for t in (fc, gk, hm))


def _rope_angles(pos, dim):
    inv_freq = ROPE_THETA ** (-(np.arange(0, dim, 2, dtype=np.float64) / dim))
    return pos.astype(np.float64)[:, None] * inv_freq[None, :]


def _rope_tables_axial(S):
    t = np.arange(S)
    ar = _rope_angles(t // GRID_W, HEAD_DIM // 2)
    ac = _rope_angles(t % GRID_W, HEAD_DIM // 2)
    cos = np.concatenate([np.cos(ar), np.cos(ar), np.cos(ac), np.cos(ac)], axis=1)
    sin = np.concatenate([-np.sin(ar), np.sin(ar), -np.sin(ac), np.sin(ac)], axis=1)
    return np.tile(cos, (1, 2)), np.tile(sin, (1, 2))


def _rope_tables_1d(S):
    a = _rope_angles(np.arange(S), HEAD_DIM)
    cos = np.concatenate([np.cos(a), np.cos(a)], axis=1)
    sin = np.concatenate([-np.sin(a), np.sin(a)], axis=1)
    return np.tile(cos, (1, 2)), np.tile(sin, (1, 2))


def _pair_rotary_layout(a, half):
    lead = a.shape[:-1]
    a6 = a.reshape(*lead, a.shape[-1] // LANES, 2, HEAD_DIM // (2 * half), 2, half)
    xp = jnp if isinstance(a, jax.Array) else np
    return xp.moveaxis(a6, -2, -4).reshape(*lead, a.shape[-1])


def _lane_qk():
    return lax.broadcasted_iota(jnp.int32, (1, LANES), 1) % HEAD_DIM < HEAD_DIM // 2


def _rotate_partner(x):
    return pltpu.roll(x, HEAD_DIM, 1)


def _proj_b_kernel(x_ref, w_ref, cos_ref, sin_ref, gq_ref, gk_ref, o_ref, vt_ref):
    xb = x_ref[...].astype(BF16)
    cos_t, sin_t = cos_ref[...], sin_ref[...]
    lo = _lane_qk()
    nb_w = 4 * LANES
    vt = _dot(xb, w_ref[:, 3 * nb_w:]).T.astype(BF16)
    ones = jnp.ones((VT_ROWS - HEAD_DIM, vt.shape[1]), BF16)
    for hd in range(GQA_KV_HEADS):
        vt_ref[hd * VT_ROWS:hd * VT_ROWS + HEAD_DIM, :] = vt[hd * HEAD_DIM:(hd + 1) * HEAD_DIM]
        vt_ref[hd * VT_ROWS + HEAD_DIM:(hd + 1) * VT_ROWS, :] = ones
    for nb in range(3):
        r = _dot(xb, w_ref[:, nb * nb_w:(nb + 1) * nb_w])
        gain = gq_ref[...] if nb < 2 else gk_ref[...]
        scale = ATTN_SCALE * LOG2_E if nb < 2 else 1.0
        for t in range(4):
            rt = r[:, t * LANES:(t + 1) * LANES]
            r2 = rt * rt
            tot = jnp.sum(r2, axis=1, keepdims=True)
            low = jnp.sum(jnp.where(lo, r2, 0.0), axis=1, keepdims=True)
            ss = jnp.where(lo, low, tot - low)
            xn = rt * lax.rsqrt(ss * (1.0 / HEAD_DIM) + RMS_EPS) * gain
            out = (xn * cos_t + _rotate_partner(xn) * sin_t) * scale
            c0 = nb * nb_w + t * LANES
            o_ref[:, c0:c0 + LANES] = out.astype(BF16)


def _proj_b(h, w_qkv, cos_t, sin_t, gq, gk, tr=512):
    S = h.shape[0]
    n_in = w_qkv.shape[1]
    n_out = n_in - GQA_KV_HEADS * HEAD_DIM
    vw = GQA_KV_HEADS * VT_ROWS
    return pl.pallas_call(
        _proj_b_kernel,
        grid=(S // tr,),
        in_specs=[pl.BlockSpec((tr, D_MODEL), lambda i: (i, 0)),
                  pl.BlockSpec((D_MODEL, n_in), lambda i: (0, 0)),
                  pl.BlockSpec((tr, LANES), lambda i: (i, 0)),
                  pl.BlockSpec((tr, LANES), lambda i: (i, 0)),
                  pl.BlockSpec((1, LANES), lambda i: (0, 0)),
                  pl.BlockSpec((1, LANES), lambda i: (0, 0))],
        out_specs=[pl.BlockSpec((tr, n_out), lambda i: (i, 0)),
                   pl.BlockSpec((vw, tr), lambda i: (0, i))],
        out_shape=[jax.ShapeDtypeStruct((S, n_out), BF16),
                   jax.ShapeDtypeStruct((vw, S), BF16)],
        compiler_params=_cparams("parallel"),
        name="proj_b",
    )(h, w_qkv, cos_t, sin_t, gq, gk)


def _gqa_kernel(q_ref, k_ref, vt_ref, o_ref, qs_ref, m_ref, acc_ref, *p_refs, tk):
    tq = q_ref.shape[0]
    S = k_ref.shape[0]
    nq = 4 * tq
    nkv = S // tk
    nbuf = len(p_refs)
    lo = _lane_qk()
    for t in range(2):
        qt = q_ref[:, t * LANES:(t + 1) * LANES].astype(F32)
        qs_ref[(2 * t) * tq:(2 * t + 1) * tq, :] = jnp.where(lo, qt, 0.0).astype(BF16)
        qs_ref[(2 * t + 1) * tq:(2 * t + 2) * tq, :] = jnp.where(lo, 0.0, qt).astype(BF16)

    def scores(j):
        k0 = pl.multiple_of(j * tk, tk)
        return _dot_nt(k_ref[pl.ds(k0, tk), :], qs_ref[...])

    def values(j, p):
        return _dot(vt_ref[:, pl.ds(pl.multiple_of(j * tk, tk), tk)], p)

    def colmax(s):
        return jnp.max(jnp.max(s.reshape(tk // SUBLANES, SUBLANES, nq), axis=0), axis=0, keepdims=True)

    m0 = jnp.broadcast_to(colmax(scores(0)), (tk, nq))

    def probs(u):
        p_refs[u % nbuf][...] = jnp.exp2(scores(u) - m0).astype(BF16)

    probs(0)
    pv = None
    for u in range(nkv):
        if u + 1 < nkv:
            probs(u + 1)
        d = values(u, p_refs[u % nbuf][...])
        pv = d if pv is None else pv + d
    safe = jnp.min(jnp.where(jnp.isfinite(pv), 1.0, 0.0)) > 0.5

    @pl.when(safe)
    def _():
        acc_ref[...] = pv

    @pl.when(jnp.logical_not(safe))
    def _():
        m_ref[...] = jnp.full(m_ref.shape, -jnp.inf, F32)
        acc_ref[...] = jnp.zeros(acc_ref.shape, F32)

        def exact_tile(u, c):
            s = scores(u)
            m_prev = m_ref[...]
            m_cur = jnp.maximum(m_prev, colmax(s))
            d = values(u, jnp.exp2(s - m_cur).astype(BF16))
            acc_ref[...] = jnp.exp2(m_prev - m_cur) * acc_ref[...] + d
            m_ref[...] = m_cur
            return c

        lax.fori_loop(0, nkv, exact_tile, 0)

    acc = acc_ref[...]
    ot = acc[:HEAD_DIM] / acc[HEAD_DIM:HEAD_DIM + 1]
    for t in range(2):
        pair = jnp.concatenate([ot[:, (2 * t) * tq:(2 * t + 1) * tq],
                                ot[:, (2 * t + 1) * tq:(2 * t + 2) * tq]], axis=0)
        o_ref[:, t * LANES:(t + 1) * LANES] = pair.T


def _gqa_attention(qk, vt, tq=256, tk=256, nbuf=2):
    S = qk.shape[0]
    qw = 4 * HEAD_DIM
    k_blk0 = GQA_Q_HEADS * HEAD_DIM // LANES
    tk = min(tk, S)
    return pl.pallas_call(
        functools.partial(_gqa_kernel, tk=tk),
        grid=(GQA_KV_HEADS, S // tq),
        in_specs=[pl.BlockSpec((tq, qw), lambda h, i: (i, h)),
                  pl.BlockSpec((S, LANES), lambda h, i: (0, k_blk0 + h)),
                  pl.BlockSpec((VT_ROWS, S), lambda h, i: (h, 0))],
        out_specs=pl.BlockSpec((tq, qw), lambda h, i: (i, h)),
        out_shape=jax.ShapeDtypeStruct((S, D_MODEL), F32),
        scratch_shapes=[pltpu.VMEM((4 * tq, LANES), BF16),
                        pltpu.VMEM((1, 4 * tq), F32),
                        pltpu.VMEM((VT_ROWS, 4 * tq), F32)]
        + [pltpu.VMEM((tk, 4 * tq), BF16)] * nbuf,
        compiler_params=_cparams("parallel", "parallel"),
        name="gqa_attention",
    )(qk, qk, vt)


def _proj_c_kernel(x_ref, w_ref, cos_ref, sin_ref, o_ref, xp_ref, slab_ref, *, dil):
    kind = pl.program_id(1)
    tm, d = x_ref.shape[0], o_ref.shape[-1]
    n = tm // dil
    nslab = d // LANES
    cb_w = 2 * LANES

    @pl.when(kind == 0)
    def _():
        if dil == 1:
            xp_ref[...] = x_ref[...].astype(BF16)
            return
        for s in range(nslab):
            slab_ref[s] = x_ref[:, s * LANES:(s + 1) * LANES]
        for c in range(dil):
            for s in range(nslab):
                xp_ref[c * n:(c + 1) * n, s * LANES:(s + 1) * LANES] = (
                    slab_ref[s, pl.ds(c, n, stride=dil), :].astype(BF16))

    def emit(c0, vals):
        o_ref[0, 0, :, :, c0:c0 + vals.shape[1]] = vals.astype(BF16).reshape(dil, n, vals.shape[1])

    @pl.when(kind < 2)
    def _():
        scale = jnp.where(kind == 0, ATTN_SCALE * LOG2_E, 1.0).astype(F32)
        cos_t = cos_ref[...] * scale
        sin_t = sin_ref[...] * scale
        ncb = d // cb_w
        r_next = _dot(xp_ref[...], w_ref[:, :cb_w])
        for cb in range(ncb):
            r = r_next
            if cb + 1 < ncb:
                r_next = _dot(xp_ref[...], w_ref[:, (cb + 1) * cb_w:(cb + 2) * cb_w])
            for s in range(cb_w // LANES):
                rt = r[:, s * LANES:(s + 1) * LANES]
                emit(cb * cb_w + s * LANES, rt * cos_t + _rotate_partner(rt) * sin_t)

    @pl.when(kind == 2)
    def _():
        for cb in range(d // cb_w):
            emit(cb * cb_w, _dot(xp_ref[...], w_ref[:, cb * cb_w:(cb + 1) * cb_w]))


def _regroup_rows(tab, dil):
    S, w = tab.shape
    return tab.reshape(S // DIL_CHUNK, DIL_CHUNK // dil, dil, w).transpose(0, 2, 1, 3).reshape(S, w)


def _proj_c(h, w, cos_np, sin_np, group, dil):
    S = h.shape[0]
    tm = DIL_CHUNK
    n = tm // dil
    ngroups = len(DIL_DILATIONS)
    cos_t = jnp.asarray(_regroup_rows(cos_np, dil), F32)
    sin_t = jnp.asarray(_regroup_rows(sin_np, dil), F32)
    return pl.pallas_call(
        functools.partial(_proj_c_kernel, dil=dil),
        grid=(S // tm, 3),
        in_specs=[pl.BlockSpec((tm, D_MODEL), lambda i, t: (i, 0)),
                  pl.BlockSpec((D_MODEL, D_MODEL), lambda i, t: (0, ngroups * t + group)),
                  pl.BlockSpec((tm, LANES), lambda i, t: (i, 0)),
                  pl.BlockSpec((tm, LANES), lambda i, t: (i, 0))],
        out_specs=pl.BlockSpec((1, 1, dil, n, D_MODEL), lambda i, t: (t, i, 0, 0, 0)),
        out_shape=jax.ShapeDtypeStruct((3, S // tm, dil, n, D_MODEL), BF16),
        scratch_shapes=[pltpu.VMEM((tm, D_MODEL), BF16),
                        pltpu.VMEM((D_MODEL // LANES, tm, LANES), F32)],
        compiler_params=_cparams("parallel", "arbitrary"),
        name=f"proj_c_dil{dil}",
    )(h, w, cos_t, sin_t)


def _dil_kernel(*refs, nchunks):
    ng = len(DIL_DILATIONS)
    o_ref, osc_ref, lsc_ref = refs[7 * ng:]
    i = pl.program_id(0)
    lo = _lane_lo()
    lo_qk = _lane_qk()

    groups, blocks = [], []
    for g, dil in enumerate(DIL_DILATIONS):
        n = DIL_CHUNK // dil
        bq = min(n, 2 * DIL_SIDE)
        nk = bq + 2 * DIL_SIDE
        qi = lax.broadcasted_iota(jnp.int32, (2 * bq, nk), 0) % bq
        kj = lax.broadcasted_iota(jnp.int32, (2 * bq, nk), 1)
        band = jnp.where(jnp.abs(kj - DIL_SIDE - qi) <= DIL_SIDE, 0.0, MASK_VALUE).astype(F32)
        groups.append(dict(dil=dil, n=n, bq=bq, nk=nk, nsub=n // bq, kj=kj, band=band,
                           kpad=-(n + 2 * DIL_SIDE) % LANES, refs=refs[7 * g:7 * g + 7]))
        blocks += [(g, c, b) for c in range(dil) for b in range(n // bq)]

    class_kv = {}

    def keys_values(g, c):
        if (g, c) not in class_kv:
            G = groups[g]
            _, kc_ref, kp_ref, kn_ref, vc_ref, vp_ref, vn_ref = G["refs"]
            kparts = [kp_ref[0, 0, c], kc_ref[0, 0, c], kn_ref[0, 0, c]]
            if G["kpad"]:
                kparts.append(jnp.zeros((G["kpad"], LANES), BF16))
            kcat_t = jnp.concatenate(kparts, axis=0).astype(F32).T.astype(BF16)
            vcat = jnp.concatenate([vp_ref[0, 0, c], vc_ref[0, 0, c], vn_ref[0, 0, c]], axis=0)
            class_kv[(g, c)] = (kcat_t, vcat)
        return class_kv[(g, c)]

    def scores(blk):
        g, c, b = blk
        G = groups[g]
        bq, nk, kj = G["bq"], G["nk"], G["kj"]
        p0 = b * bq
        bias = G["band"]
        if b == 0:
            bias = jnp.where(kj >= jnp.where(i == 0, DIL_SIDE, 0), bias, MASK_VALUE)
        if b == G["nsub"] - 1:
            bias = jnp.where(kj < jnp.where(i == nchunks - 1, bq + DIL_SIDE, nk), bias, MASK_VALUE)
        qf = G["refs"][0][0, 0, c, p0:p0 + bq, :].astype(F32)
        qs = jnp.concatenate([jnp.where(lo_qk, qf, 0.0), jnp.where(lo_qk, 0.0, qf)], axis=0)
        return _dot(qs.astype(BF16), keys_values(g, c)[0][:, p0:p0 + nk]) + bias

    def finish(blk, s):
        g, c, b = blk
        G = groups[g]
        bq, nk, dil = G["bq"], G["nk"], G["dil"]
        p0 = b * bq
        m = jnp.max(s, axis=1, keepdims=True)
        p = jnp.exp2(s - m)
        l = jnp.sum(p, axis=1, keepdims=True)
        o2 = _dot(p.astype(BF16), keys_values(g, c)[1][p0:p0 + nk]) / l
        lse2 = m + jnp.log2(l)
        o_pair = jnp.where(lo, o2[:bq], o2[bq:])
        l_pair = jnp.where(lo, lse2[:bq], lse2[bq:])
        rows = pl.ds(p0, bq) if dil == 1 else pl.ds(p0 * dil + c, bq, stride=dil)
        osc_ref[g, rows, :] = o_pair
        lsc_ref[g, rows, :] = l_pair

    s_next = scores(blocks[0])
    for idx, blk in enumerate(blocks):
        s_cur = s_next
        if idx + 1 < len(blocks):
            s_next = scores(blocks[idx + 1])
        finish(blk, s_cur)

    ls = [lsc_ref[g] for g in range(ng)]
    mx = functools.reduce(jnp.maximum, ls)
    ws = [jnp.exp2(l - mx) for l in ls]
    num = sum(w * osc_ref[g] for g, w in enumerate(ws))
    o_ref[...] = num / sum(ws)


def _dilated_attention(qkvs):
    nchunks = qkvs[0].shape[1]
    S = nchunks * DIL_CHUNK
    in_specs, args = [], []
    for dil, a in zip(DIL_DILATIONS, qkvs):
        n = DIL_CHUNK // dil
        nblk = n // DIL_SIDE
        full = (1, 1, dil, n, LANES)
        halo = (1, 1, dil, DIL_SIDE, LANES)

        def cur(t):
            return lambda i, m: (t, i, 0, 0, m)

        def prev(t, nblk=nblk):
            return lambda i, m: (t, jnp.maximum(i - 1, 0), 0, nblk - 1, m)

        def nxt(t):
            return lambda i, m: (t, jnp.minimum(i + 1, nchunks - 1), 0, 0, m)

        in_specs += [pl.BlockSpec(full, cur(0)),
                     pl.BlockSpec(full, cur(1)), pl.BlockSpec(halo, prev(1)), pl.BlockSpec(halo, nxt(1)),
                     pl.BlockSpec(full, cur(2)), pl.BlockSpec(halo, prev(2)), pl.BlockSpec(halo, nxt(2))]
        args += [a] * 7
    ng = len(DIL_DILATIONS)
    return pl.pallas_call(
        functools.partial(_dil_kernel, nchunks=nchunks),
        grid=(nchunks, D_MODEL // LANES),
        in_specs=in_specs,
        out_specs=pl.BlockSpec((DIL_CHUNK, LANES), lambda i, m: (i, m)),
        out_shape=jax.ShapeDtypeStruct((S, D_MODEL), F32),
        scratch_shapes=[pltpu.VMEM((ng, DIL_CHUNK, LANES), F32),
                        pltpu.VMEM((ng, DIL_CHUNK, LANES), F32)],
        compiler_params=_cparams("parallel", "parallel"),
        name="dilated_attention",
    )(*args)


def _epilogue_kernel(br_ref, h_ref, wt_ref, mk_ref, mv_ref, wo_ref, g_ref, b_ref, o_ref, *, nsplit):
    lo = _lane_lo()
    tr = h_ref.shape[0]
    rs = tr // nsplit

    def pre_norm(rows):
        h = h_ref[rows, :]
        hb = h.astype(BF16)
        gate_b = _dot(hb, wt_ref[:, :D_MODEL])
        yb = (br_ref[rows, :] * (gate_b * jax.nn.sigmoid(gate_b))).astype(BF16)
        y = _dot(yb, wo_ref[:D_MODEL, :])
        tail_m = _dot(hb, wt_ref[:, D_MODEL:])
        for t in range(MEM_WIDTH // LANES):
            cols = slice(t * LANES, (t + 1) * LANES)
            qf = tail_m[:, MEM_WIDTH + t * LANES:MEM_WIDTH + (t + 1) * LANES] * ATTN_SCALE
            kt, vt = mk_ref[:, cols], mv_ref[:, cols]
            outs = []
            for qh in (jnp.where(lo, qf, 0.0), jnp.where(lo, 0.0, qf)):
                s = _dot_nt(qh.astype(BF16), kt)
                m = jnp.max(s, axis=1, keepdims=True)
                p = jnp.exp(s - m)
                l = jnp.sum(p, axis=1, keepdims=True)
                outs.append(_dot(p.astype(BF16), vt) / l)
            mem_out = jnp.where(lo, outs[0], outs[1])
            gm = tail_m[:, cols]
            ym = (mem_out * (gm * jax.nn.sigmoid(gm))).astype(BF16)
            y = y + _dot(ym, wo_ref[D_MODEL + t * LANES:D_MODEL + (t + 1) * LANES, :])
        return DEEPNORM_ALPHA * h + y

    z_next = pre_norm(pl.ds(0, rs))
    for r in range(nsplit):
        z = z_next
        if r + 1 < nsplit:
            z_next = pre_norm(pl.ds((r + 1) * rs, rs))
        o_ref[pl.ds(r * rs, rs), :] = _layer_norm_rows(z, g_ref[...], b_ref[...])


def _epilogue(branch, h, w_tail, mk, mv, w_out, g, b, tr=1024, nsplit=2):
    S = h.shape[0]
    row = lambda i: (i, 0)
    fixed = lambda i: (0, 0)
    return pl.pallas_call(
        functools.partial(_epilogue_kernel, nsplit=nsplit),
        grid=(S // tr,),
        in_specs=[pl.BlockSpec((tr, D_MODEL), row),
                  pl.BlockSpec((tr, D_MODEL), row),
                  pl.BlockSpec((D_MODEL, TAIL), fixed),
                  pl.BlockSpec((N_MEM, MEM_WIDTH), fixed),
                  pl.BlockSpec((N_MEM, MEM_WIDTH), fixed),
                  pl.BlockSpec((INNER, D_MODEL), fixed),
                  pl.BlockSpec((1, D_MODEL), fixed),
                  pl.BlockSpec((1, D_MODEL), fixed)],
        out_specs=pl.BlockSpec((tr, D_MODEL), row),
        out_shape=jax.ShapeDtypeStruct((S, D_MODEL), F32),
        compiler_params=_cparams("parallel"),
        name="epilogue",
    )(branch, h, w_tail, mk, mv, w_out, g.reshape(1, -1), b.reshape(1, -1))


def _dup_kv_columns(w):
    d = w.shape[0]
    w4 = w.reshape(d, GQA_KV_HEADS, 1, HEAD_DIM)
    return jnp.broadcast_to(w4, (d, GQA_KV_HEADS, 2, HEAD_DIM)).reshape(d, 2 * GQA_KV_HEADS * HEAD_DIM)


def kernel(x, mem, ln_in_g, ln_in_b, w_mem_kv, w_in_a, w_in_b, q_norm_g, k_norm_g, w_in_c, w_out, ln_g, ln_b):
    B, S, D = x.shape
    assert B == 1 and D == D_MODEL and S % DIL_CHUNK == 0 and S % (FFT_N2 * SUBLANES) == 0

    mkv = _proj(mem[0], w_mem_kv.astype(BF16), name="proj_mem").astype(BF16)
    mk, mv = mkv[:, :MEM_WIDTH], mkv[:, MEM_WIDTH:]
    h = _ln_in(x[0], ln_in_g, ln_in_b)
    fft_tabs = None
    qw = GQA_Q_HEADS * HEAD_DIM
    kvw = GQA_KV_HEADS * HEAD_DIM

    for i in range(DEPTH):
        kind, j = i % 3, i // 3
        if kind == 0:
            if fft_tabs is None:
                fft_tabs = _fft_tables(S)
            fc, gk, hm = fft_tabs
            w_tail = w_in_a[j].astype(BF16)
            branch = _fft_stage2(_fft_stage1(_proj_a(h, fc), gk), hm)
        elif kind == 1:
            w = w_in_b[j]
            half = HEAD_DIM // 4
            w_qk = jnp.concatenate([w[:, :qw], _dup_kv_columns(w[:, qw:qw + kvw])], axis=1)
            w_qkv = jnp.concatenate([_pair_rotary_layout(w_qk, half), w[:, qw + kvw:qw + 2 * kvw]],
                                    axis=1).astype(BF16)
            cos_t, sin_t = (jnp.asarray(_pair_rotary_layout(t, half), F32) for t in _rope_tables_axial(S))
            gq = _pair_rotary_layout(jnp.tile(q_norm_g[j], 2), half).reshape(1, LANES)
            gkn = _pair_rotary_layout(jnp.tile(k_norm_g[j], 2), half).reshape(1, LANES)
            qk, vt = _proj_b(h, w_qkv, cos_t, sin_t, gq, gkn)
            w_tail = w[:, qw + 2 * kvw:].astype(BF16)
            branch = _gqa_attention(qk, vt)
        else:
            w = w_in_c[j].astype(BF16)
            half = HEAD_DIM // 2
            n_qk = 2 * len(DIL_DILATIONS) * D_MODEL
            w = jnp.concatenate([_pair_rotary_layout(w[:, :n_qk], half), w[:, n_qk:]], axis=1)
            cos_t, sin_t = (_pair_rotary_layout(t, half) for t in _rope_tables_1d(S))
            qkvs = [_proj_c(h, w, cos_t, sin_t, g, dil) for g, dil in enumerate(DIL_DILATIONS)]
            w_tail = w[:, w.shape[1] - TAIL:]
            branch = _dilated_attention(qkvs)
        h = _epilogue(branch, h, w_tail, mk, mv, w_out[i].astype(BF16), ln_g[i], ln_b[i])
    return h[None]
```

```python
import functools

import jax
import jax.numpy as jnp
import numpy as np
from jax import lax
from jax.experimental import pallas as pl
from jax.experimental.pallas import tpu as pltpu

F32 = jnp.float32
BF16 = jnp.bfloat16

D_MODEL = 1024
DEPTH = 4
N_MEM = 256
GRID_W = 64
HEAD_DIM = 64
ROPE_THETA = 10000.0
LN_EPS = 1e-5
RMS_EPS = 1e-6
MASK_VALUE = -1e30
FNET_GROUPS = 4
FNET_GROUP_W = D_MODEL // FNET_GROUPS
GQA_Q_HEADS = 16
GQA_KV_HEADS = 4
DIL_DILATIONS = (1, 4, 16)
DIL_SIDE = 64
MEM_WIDTH = 256
INNER = D_MODEL + MEM_WIDTH
TAIL = INNER + MEM_WIDTH
DEEPNORM_ALPHA = (2.0 * DEPTH) ** 0.25
ATTN_SCALE = HEAD_DIM ** -0.5
LOG2_E = float(np.log2(np.e))
VT_ROWS = HEAD_DIM + 16
LANES = 128
SUBLANES = 8
FFT_N2 = 256
DIL_CHUNK = 1024
VMEM_LIMIT = 48 << 20


def _cparams(*sem):
    return pltpu.CompilerParams(dimension_semantics=sem, vmem_limit_bytes=VMEM_LIMIT)


def _dot(a, b):
    return jnp.dot(a, b, preferred_element_type=F32)


def _dot_nt(a, b):
    return lax.dot_general(a, b, (((1,), (1,)), ((), ())), preferred_element_type=F32)


def _lane_lo(shape=(1, LANES)):
    return lax.broadcasted_iota(jnp.int32, shape, len(shape) - 1) % LANES < HEAD_DIM


def _layer_norm_rows(z, g, b):
    mu = jnp.mean(z, axis=-1, keepdims=True)
    zc = z - mu
    var = jnp.mean(zc * zc, axis=-1, keepdims=True)
    return zc * lax.rsqrt(var + LN_EPS) * g + b


def _ln_kernel(x_ref, g_ref, b_ref, o_ref):
    o_ref[...] = _layer_norm_rows(x_ref[...], g_ref[...], b_ref[...])


def _ln_in(x, g, b, tr=512):
    S = x.shape[0]
    return pl.pallas_call(
        _ln_kernel,
        grid=(S // tr,),
        in_specs=[pl.BlockSpec((tr, D_MODEL), lambda i: (i, 0)),
                  pl.BlockSpec((1, D_MODEL), lambda i: (0, 0)),
                  pl.BlockSpec((1, D_MODEL), lambda i: (0, 0))],
        out_specs=pl.BlockSpec((tr, D_MODEL), lambda i: (i, 0)),
        out_shape=jax.ShapeDtypeStruct((S, D_MODEL), F32),
        compiler_params=_cparams("parallel"),
        name="ln_in",
    )(x, g.reshape(1, -1), b.reshape(1, -1))


def _proj_kernel(x_ref, w_ref, o_ref):
    o_ref[...] = _dot(x_ref[...].astype(BF16), w_ref[...])


def _proj(x, w, tr=512, name="proj"):
    S, K = x.shape
    N = w.shape[1]
    tr = min(tr, S)
    return pl.pallas_call(
        _proj_kernel,
        grid=(S // tr,),
        in_specs=[pl.BlockSpec((tr, K), lambda i: (i, 0)),
                  pl.BlockSpec((K, N), lambda i: (0, 0))],
        out_specs=pl.BlockSpec((tr, N), lambda i: (i, 0)),
        out_shape=jax.ShapeDtypeStruct((S, N), F32),
        compiler_params=_cparams("parallel"),
        name=name,
    )(x, w)


def _proj_a_kernel(x_ref, fc_ref, z_ref):
    xb = x_ref[...].astype(BF16)
    for g in range(FNET_GROUPS):
        cols = slice(g * FNET_GROUP_W, (g + 1) * FNET_GROUP_W)
        zg = _dot(xb[:, cols], fc_ref[...])
        z_ref[0, :, cols] = zg[:, :FNET_GROUP_W]
        z_ref[1, :, cols] = zg[:, FNET_GROUP_W:]


def _proj_a(h, fc, tr=512):
    S = h.shape[0]
    return pl.pallas_call(
        _proj_a_kernel,
        grid=(S // tr,),
        in_specs=[pl.BlockSpec((tr, D_MODEL), lambda i: (i, 0)),
                  pl.BlockSpec((FNET_GROUP_W, 2 * FNET_GROUP_W), lambda i: (0, 0))],
        out_specs=pl.BlockSpec((2, tr, D_MODEL), lambda i: (0, i, 0)),
        out_shape=jax.ShapeDtypeStruct((2, S, D_MODEL), F32),
        compiler_params=_cparams("parallel"),
        name="proj_a",
    )(h, fc)


def _fft1_kernel(z_ref, g_ref, t_ref):
    two, n1, _, sub, tc = z_ref.shape
    x = z_ref[...].reshape(two * n1 * sub, tc).astype(BF16)
    t_ref[...] = _dot(g_ref[...], x).reshape(t_ref.shape)


def _fft_stage1(z, gk, tc=1024):
    _, S, D = z.shape
    n1 = S // FFT_N2
    nu = FFT_N2 // SUBLANES
    z5 = z.reshape(2, n1, nu, SUBLANES, D)
    rows = 2 * n1 * SUBLANES
    t5 = pl.pallas_call(
        _fft1_kernel,
        grid=(nu, D // tc),
        in_specs=[pl.BlockSpec((2, n1, 1, SUBLANES, tc), lambda u, c: (0, 0, u, 0, c)),
                  pl.BlockSpec((rows, rows), lambda u, c: (0, 0))],
        out_specs=pl.BlockSpec((n1, 2, 1, SUBLANES, tc), lambda u, c: (0, 0, u, 0, c)),
        out_shape=jax.ShapeDtypeStruct((n1, 2, nu, SUBLANES, D), F32),
        compiler_params=_cparams("parallel", "parallel"),
        name="fft_stage1",
    )(z5, gk)
    return t5.reshape(n1, 2, FFT_N2, D)


def _fft2_kernel(t_ref, h_ref, o_ref, slab_ref):
    nj, _, n2, d = t_ref.shape
    nslab = d // LANES
    for j in range(nj):
        tj = t_ref[j].reshape(2 * n2, d).astype(BF16)
        r = _dot(h_ref[j], tj)
        for s in range(nslab):
            slab_ref[s, pl.ds(j, n2, stride=nj), :] = r[:, s * LANES:(s + 1) * LANES]
    for s in range(nslab):
        o_ref[:, :, s * LANES:(s + 1) * LANES] = slab_ref[s].reshape(n2, nj, LANES)


def _fft_stage2(t, hmat, tc=512):
    n1, _, n2, D = t.shape
    nj = SUBLANES
    y3 = pl.pallas_call(
        _fft2_kernel,
        grid=(n1 // nj, D // tc),
        in_specs=[pl.BlockSpec((nj, 2, n2, tc), lambda a, c: (a, 0, 0, c)),
                  pl.BlockSpec((nj, n2, 2 * n2), lambda a, c: (a, 0, 0))],
        out_specs=pl.BlockSpec((n2, nj, tc), lambda a, c: (0, a, c)),
        out_shape=jax.ShapeDtypeStruct((n2, n1, D), F32),
        scratch_shapes=[pltpu.VMEM((tc // LANES, n2 * nj, LANES), F32)],
        compiler_params=_cparams("parallel", "parallel"),
        name="fft_stage2",
    )(t, hmat)
    return y3.reshape(n2 * n1, D)


def _fft_tables(S):
    n1, n2 = S // FFT_N2, FFT_N2
    c = np.arange(FNET_GROUP_W)
    ang = (2.0 * np.pi / FNET_GROUP_W) * ((c[:, None] * c[None, :]) % FNET_GROUP_W)
    scale = 1.0 / np.sqrt(float(S) * FNET_GROUP_W)
    fc = np.concatenate([np.cos(ang), -np.sin(ang)], axis=1) * scale
    k1 = np.arange(n1)
    th = (2.0 * np.pi / n1) * ((k1[:, None] * k1[None, :]) % n1)
    cs, sn = np.cos(th), np.sin(th)
    g = np.stack([np.stack([cs, sn], axis=1), np.stack([-sn, cs], axis=1)], axis=1)
    gk = np.kron(g.reshape(2 * n1, 2 * n1), np.eye(SUBLANES))
    k2 = np.arange(n2)
    kk = k1[:, None, None] + n1 * k2[None, :, None]
    ph = (2.0 * np.pi / S) * ((k2[None, None, :] * kk) % S)
    hm = np.concatenate([np.cos(ph), np.sin(ph)], axis=2)
    return tuple(jnp.asarray(t, F32).astype(BF16) for t in (fc, gk, hm))


def _rope_angles(pos, dim):
    inv_freq = ROPE_THETA ** (-(np.arange(0, dim, 2, dtype=np.float64) / dim))
    return pos.astype(np.float64)[:, None] * inv_freq[None, :]


def _rope_tables_axial(S):
    t = np.arange(S)
    ar = _rope_angles(t // GRID_W, HEAD_DIM // 2)
    ac = _rope_angles(t % GRID_W, HEAD_DIM // 2)
    cos = np.concatenate([np.cos(ar), np.cos(ar), np.cos(ac), np.cos(ac)], axis=1)
    sin = np.concatenate([-np.sin(ar), np.sin(ar), -np.sin(ac), np.sin(ac)], axis=1)
    return np.tile(cos, (1, 2)), np.tile(sin, (1, 2))


def _rope_tables_1d(S):
    a = _rope_angles(np.arange(S), HEAD_DIM)
    cos = np.concatenate([np.cos(a), np.cos(a)], axis=1)
    sin = np.concatenate([-np.sin(a), np.sin(a)], axis=1)
    return np.tile(cos, (1, 2)), np.tile(sin, (1, 2))


def _pair_rotary_layout(a, half):
    lead = a.shape[:-1]
    a6 = a.reshape(*lead, a.shape[-1] // LANES, 2, HEAD_DIM // (2 * half), 2, half)
    xp = jnp if isinstance(a, jax.Array) else np
    return xp.moveaxis(a6, -2, -4).reshape(*lead, a.shape[-1])


def _pair_rotary_lanes(x, half):
    nblk = LANES // half
    groups = HEAD_DIM // (2 * half)
    lane_blk = lax.broadcasted_iota(jnp.int32, (1, LANES), 1) // half
    moves = {}
    for j in range(nblk):
        fs, rem = divmod(j, nblk // 2)
        head, group = divmod(rem, groups)
        s = head * (HEAD_DIM // half) + 2 * group + fs
        moves.setdefault(((j - s) * half) % LANES, []).append(j)
    out = x
    for shift, dst in moves.items():
        if shift:
            mask = functools.reduce(jnp.logical_or, [lane_blk == j for j in dst])
            out = jnp.where(mask, pltpu.roll(x, shift, 1), out)
    return out


def _prep_qk_kernel(w_ref, o_ref, *, half):
    for t in range(o_ref.shape[1] // LANES):
        cols = slice(t * LANES, (t + 1) * LANES)
        o_ref[:, cols] = _pair_rotary_lanes(w_ref[:, cols], half).astype(BF16)


def _prep_qk_weights(w, ncols, half, tc=1024):
    d = w.shape[0]
    return pl.pallas_call(
        functools.partial(_prep_qk_kernel, half=half),
        grid=(ncols // tc,),
        in_specs=[pl.BlockSpec((d, tc), lambda c: (0, c))],
        out_specs=pl.BlockSpec((d, tc), lambda c: (0, c)),
        out_shape=jax.ShapeDtypeStruct((d, ncols), BF16),
        compiler_params=_cparams("parallel"),
        name="prep_qk_weights",
    )(w)


def _lane_qk():
    return lax.broadcasted_iota(jnp.int32, (1, LANES), 1) % HEAD_DIM < HEAD_DIM // 2


def _rotate_partner(x):
    return pltpu.roll(x, HEAD_DIM, 1)


def _proj_b_kernel(x_ref, w_ref, cos_ref, sin_ref, gq_ref, gk_ref, o_ref, vt_ref):
    xb = x_ref[...].astype(BF16)
    cos_t, sin_t = cos_ref[...], sin_ref[...]
    lo = _lane_qk()
    nb_w = 4 * LANES
    vt = _dot(xb, w_ref[:, 3 * nb_w:]).T.astype(BF16)
    ones = jnp.ones((VT_ROWS - HEAD_DIM, vt.shape[1]), BF16)
    for hd in range(GQA_KV_HEADS):
        vt_ref[hd * VT_ROWS:hd * VT_ROWS + HEAD_DIM, :] = vt[hd * HEAD_DIM:(hd + 1) * HEAD_DIM]
        vt_ref[hd * VT_ROWS + HEAD_DIM:(hd + 1) * VT_ROWS, :] = ones
    for nb in range(3):
        r = _dot(xb, w_ref[:, nb * nb_w:(nb + 1) * nb_w])
        gain = gq_ref[...] if nb < 2 else gk_ref[...]
        scale = ATTN_SCALE * LOG2_E if nb < 2 else 1.0
        for t in range(4):
            rt = r[:, t * LANES:(t + 1) * LANES]
            r2 = rt * rt
            tot = jnp.sum(r2, axis=1, keepdims=True)
            low = jnp.sum(jnp.where(lo, r2, 0.0), axis=1, keepdims=True)
            ss = jnp.where(lo, low, tot - low)
            xn = rt * lax.rsqrt(ss * (1.0 / HEAD_DIM) + RMS_EPS) * gain
            out = (xn * cos_t + _rotate_partner(xn) * sin_t) * scale
            c0 = nb * nb_w + t * LANES
            o_ref[:, c0:c0 + LANES] = out.astype(BF16)


def _proj_b(h, w_qkv, cos_t, sin_t, gq, gk, tr=512):
    S = h.shape[0]
    n_in = w_qkv.shape[1]
    n_out = n_in - GQA_KV_HEADS * HEAD_DIM
    vw = GQA_KV_HEADS * VT_ROWS
    return pl.pallas_call(
        _proj_b_kernel,
        grid=(S // tr,),
        in_specs=[pl.BlockSpec((tr, D_MODEL), lambda i: (i, 0)),
                  pl.BlockSpec((D_MODEL, n_in), lambda i: (0, 0)),
                  pl.BlockSpec((tr, LANES), lambda i: (i, 0)),
                  pl.BlockSpec((tr, LANES), lambda i: (i, 0)),
                  pl.BlockSpec((1, LANES), lambda i: (0, 0)),
                  pl.BlockSpec((1, LANES), lambda i: (0, 0))],
        out_specs=[pl.BlockSpec((tr, n_out), lambda i: (i, 0)),
                   pl.BlockSpec((vw, tr), lambda i: (0, i))],
        out_shape=[jax.ShapeDtypeStruct((S, n_out), BF16),
                   jax.ShapeDtypeStruct((vw, S), BF16)],
        compiler_params=_cparams("parallel"),
        name="proj_b",
    )(h, w_qkv, cos_t, sin_t, gq, gk)


def _gqa_kernel(q_ref, k_ref, vt_ref, o_ref, qs_ref, m_ref, acc_ref, *p_refs, tk):
    tq = q_ref.shape[0]
    S = k_ref.shape[0]
    nq = 4 * tq
    nkv = S // tk
    nbuf = len(p_refs)
    lo = _lane_qk()
    for t in range(2):
        qt = q_ref[:, t * LANES:(t + 1) * LANES].astype(F32)
        qs_ref[(2 * t) * tq:(2 * t + 1) * tq, :] = jnp.where(lo, qt, 0.0).astype(BF16)
        qs_ref[(2 * t + 1) * tq:(2 * t + 2) * tq, :] = jnp.where(lo, 0.0, qt).astype(BF16)

    def scores(j):
        k0 = pl.multiple_of(j * tk, tk)
        return _dot_nt(k_ref[pl.ds(k0, tk), :], qs_ref[...])

    def values(j, p):
        return _dot(vt_ref[:, pl.ds(pl.multiple_of(j * tk, tk), tk)], p)

    def colmax(s):
        return jnp.max(jnp.max(s.reshape(tk // SUBLANES, SUBLANES, nq), axis=0), axis=0, keepdims=True)

    m0 = jnp.broadcast_to(colmax(scores(0)), (tk, nq))

    def probs(u):
        p_refs[u % nbuf][...] = jnp.exp2(scores(u) - m0).astype(BF16)

    probs(0)
    pv = None
    for u in range(nkv):
        if u + 1 < nkv:
            probs(u + 1)
        d = values(u, p_refs[u % nbuf][...])
        pv = d if pv is None else pv + d
    safe = jnp.min(jnp.where(jnp.isfinite(pv), 1.0, 0.0)) > 0.5

    @pl.when(safe)
    def _():
        acc_ref[...] = pv

    @pl.when(jnp.logical_not(safe))
    def _():
        m_ref[...] = jnp.full(m_ref.shape, -jnp.inf, F32)
        acc_ref[...] = jnp.zeros(acc_ref.shape, F32)

        def exact_tile(u, c):
            s = scores(u)
            m_prev = m_ref[...]
            m_cur = jnp.maximum(m_prev, colmax(s))
            d = values(u, jnp.exp2(s - m_cur).astype(BF16))
            acc_ref[...] = jnp.exp2(m_prev - m_cur) * acc_ref[...] + d
            m_ref[...] = m_cur
            return c

        lax.fori_loop(0, nkv, exact_tile, 0)

    acc = acc_ref[...]
    ot = acc[:HEAD_DIM] / acc[HEAD_DIM:HEAD_DIM + 1]
    for t in range(2):
        pair = jnp.concatenate([ot[:, (2 * t) * tq:(2 * t + 1) * tq],
                                ot[:, (2 * t + 1) * tq:(2 * t + 2) * tq]], axis=0)
        o_ref[:, t * LANES:(t + 1) * LANES] = pair.T.astype(o_ref.dtype)


def _gqa_attention(qk, vt, tq=256, tk=256, nbuf=2):
    S = qk.shape[0]
    qw = 4 * HEAD_DIM
    k_blk0 = GQA_Q_HEADS * HEAD_DIM // LANES
    tk = min(tk, S)
    return pl.pallas_call(
        functools.partial(_gqa_kernel, tk=tk),
        grid=(GQA_KV_HEADS, S // tq),
        in_specs=[pl.BlockSpec((tq, qw), lambda h, i: (i, h)),
                  pl.BlockSpec((S, LANES), lambda h, i: (0, k_blk0 + h)),
                  pl.BlockSpec((VT_ROWS, S), lambda h, i: (h, 0))],
        out_specs=pl.BlockSpec((tq, qw), lambda h, i: (i, h)),
        out_shape=jax.ShapeDtypeStruct((S, D_MODEL), BF16),
        scratch_shapes=[pltpu.VMEM((4 * tq, LANES), BF16),
                        pltpu.VMEM((1, 4 * tq), F32),
                        pltpu.VMEM((VT_ROWS, 4 * tq), F32)]
        + [pltpu.VMEM((tk, 4 * tq), BF16)] * nbuf,
        compiler_params=_cparams("parallel", "parallel"),
        name="gqa_attention",
    )(qk, qk, vt)


def _proj_c_kernel(x_ref, wqk_ref, wv_ref, cos_ref, sin_ref, o_ref, xp_ref, slab_ref, *, dil):
    kind = pl.program_id(1)
    tm, d = x_ref.shape[0], o_ref.shape[-1]
    n = tm // dil
    nslab = d // LANES
    cb_w = 2 * LANES

    @pl.when(kind == 0)
    def _():
        if dil == 1:
            xp_ref[...] = x_ref[...].astype(BF16)
            return
        for s in range(nslab):
            slab_ref[s] = x_ref[:, s * LANES:(s + 1) * LANES]
        for c in range(dil):
            for s in range(nslab):
                xp_ref[c * n:(c + 1) * n, s * LANES:(s + 1) * LANES] = (
                    slab_ref[s, pl.ds(c, n, stride=dil), :].astype(BF16))

    def emit(c0, vals):
        o_ref[0, 0, :, :, c0:c0 + vals.shape[1]] = vals.astype(BF16).reshape(dil, n, vals.shape[1])

    @pl.when(kind < 2)
    def _():
        scale = jnp.where(kind == 0, ATTN_SCALE * LOG2_E, 1.0).astype(F32)
        cos_t = cos_ref[...] * scale
        sin_t = sin_ref[...] * scale
        ncb = d // cb_w
        r_next = _dot(xp_ref[...], wqk_ref[:, :cb_w])
        for cb in range(ncb):
            r = r_next
            if cb + 1 < ncb:
                r_next = _dot(xp_ref[...], wqk_ref[:, (cb + 1) * cb_w:(cb + 2) * cb_w])
            for s in range(cb_w // LANES):
                rt = r[:, s * LANES:(s + 1) * LANES]
                emit(cb * cb_w + s * LANES, rt * cos_t + _rotate_partner(rt) * sin_t)

    @pl.when(kind == 2)
    def _():
        for cb in range(d // cb_w):
            emit(cb * cb_w, _dot(xp_ref[...], wv_ref[:, cb * cb_w:(cb + 1) * cb_w]))


def _regroup_rows(tab, dil):
    S, w = tab.shape
    return tab.reshape(S // DIL_CHUNK, DIL_CHUNK // dil, dil, w).transpose(0, 2, 1, 3).reshape(S, w)


def _proj_c(h, w_qk, w_v, cos_np, sin_np, group, dil):
    S = h.shape[0]
    tm = DIL_CHUNK
    n = tm // dil
    ngroups = len(DIL_DILATIONS)
    cos_t = jnp.asarray(_regroup_rows(cos_np, dil), F32)
    sin_t = jnp.asarray(_regroup_rows(sin_np, dil), F32)
    return pl.pallas_call(
        functools.partial(_proj_c_kernel, dil=dil),
        grid=(S // tm, 3),
        in_specs=[pl.BlockSpec((tm, D_MODEL), lambda i, t: (i, 0)),
                  pl.BlockSpec((D_MODEL, D_MODEL), lambda i, t: (0, ngroups * jnp.minimum(t, 1) + group)),
                  pl.BlockSpec((D_MODEL, D_MODEL), lambda i, t: (0, group)),
                  pl.BlockSpec((tm, LANES), lambda i, t: (i, 0)),
                  pl.BlockSpec((tm, LANES), lambda i, t: (i, 0))],
        out_specs=pl.BlockSpec((1, 1, dil, n, D_MODEL), lambda i, t: (t, i, 0, 0, 0)),
        out_shape=jax.ShapeDtypeStruct((3, S // tm, dil, n, D_MODEL), BF16),
        scratch_shapes=[pltpu.VMEM((tm, D_MODEL), BF16),
                        pltpu.VMEM((D_MODEL // LANES, tm, LANES), F32)],
        compiler_params=_cparams("parallel", "arbitrary"),
        name=f"proj_c_dil{dil}",
    )(h, w_qk, w_v, cos_t, sin_t)


def _dil_kernel(*refs, nchunks):
    ng = len(DIL_DILATIONS)
    o_ref, osc_ref, lsc_ref = refs[7 * ng:]
    i = pl.program_id(0)
    lo = _lane_lo()
    lo_qk = _lane_qk()

    groups, blocks = [], []
    for g, dil in enumerate(DIL_DILATIONS):
        n = DIL_CHUNK // dil
        bq = min(n, 2 * DIL_SIDE)
        nk = bq + 2 * DIL_SIDE
        qi = lax.broadcasted_iota(jnp.int32, (2 * bq, nk), 0) % bq
        kj = lax.broadcasted_iota(jnp.int32, (2 * bq, nk), 1)
        band = jnp.where(jnp.abs(kj - DIL_SIDE - qi) <= DIL_SIDE, 0.0, MASK_VALUE).astype(F32)
        groups.append(dict(dil=dil, n=n, bq=bq, nk=nk, nsub=n // bq, kj=kj, band=band,
                           kpad=-(n + 2 * DIL_SIDE) % LANES, refs=refs[7 * g:7 * g + 7]))
        blocks += [(g, c, b) for c in range(dil) for b in range(n // bq)]

    class_kv = {}

    def keys_values(g, c):
        if (g, c) not in class_kv:
            G = groups[g]
            _, kc_ref, kp_ref, kn_ref, vc_ref, vp_ref, vn_ref = G["refs"]
            kparts = [kp_ref[0, 0, c], kc_ref[0, 0, c], kn_ref[0, 0, c]]
            if G["kpad"]:
                kparts.append(jnp.zeros((G["kpad"], LANES), BF16))
            kcat_t = jnp.concatenate(kparts, axis=0).astype(F32).T.astype(BF16)
            vcat = jnp.concatenate([vp_ref[0, 0, c], vc_ref[0, 0, c], vn_ref[0, 0, c]], axis=0)
            class_kv[(g, c)] = (kcat_t, vcat)
        return class_kv[(g, c)]

    def scores(blk):
        g, c, b = blk
        G = groups[g]
        bq, nk, kj = G["bq"], G["nk"], G["kj"]
        p0 = b * bq
        bias = G["band"]
        if b == 0:
            bias = jnp.where(kj >= jnp.where(i == 0, DIL_SIDE, 0), bias, MASK_VALUE)
        if b == G["nsub"] - 1:
            bias = jnp.where(kj < jnp.where(i == nchunks - 1, bq + DIL_SIDE, nk), bias, MASK_VALUE)
        qf = G["refs"][0][0, 0, c, p0:p0 + bq, :].astype(F32)
        qs = jnp.concatenate([jnp.where(lo_qk, qf, 0.0), jnp.where(lo_qk, 0.0, qf)], axis=0)
        return _dot(qs.astype(BF16), keys_values(g, c)[0][:, p0:p0 + nk]) + bias

    def finish(blk, s):
        g, c, b = blk
        G = groups[g]
        bq, nk, dil = G["bq"], G["nk"], G["dil"]
        p0 = b * bq
        m = jnp.max(s, axis=1, keepdims=True)
        p = jnp.exp2(s - m)
        l = jnp.sum(p, axis=1, keepdims=True)
        o2 = _dot(p.astype(BF16), keys_values(g, c)[1][p0:p0 + nk]) / l
        lse2 = m + jnp.log2(l)
        o_pair = jnp.where(lo, o2[:bq], o2[bq:])
        l_pair = jnp.where(lo, lse2[:bq], lse2[bq:])
        rows = pl.ds(p0, bq) if dil == 1 else pl.ds(p0 * dil + c, bq, stride=dil)
        osc_ref[g, rows, :] = o_pair
        lsc_ref[g, rows, :] = l_pair

    s_next = scores(blocks[0])
    for idx, blk in enumerate(blocks):
        s_cur = s_next
        if idx + 1 < len(blocks):
            s_next = scores(blocks[idx + 1])
        finish(blk, s_cur)

    ls = [lsc_ref[g] for g in range(ng)]
    mx = functools.reduce(jnp.maximum, ls)
    ws = [jnp.exp2(l - mx) for l in ls]
    num = sum(w * osc_ref[g] for g, w in enumerate(ws))
    o_ref[...] = (num / sum(ws)).astype(o_ref.dtype)


def _dilated_attention(qkvs):
    nchunks = qkvs[0].shape[1]
    S = nchunks * DIL_CHUNK
    in_specs, args = [], []
    for dil, a in zip(DIL_DILATIONS, qkvs):
        n = DIL_CHUNK // dil
        nblk = n // DIL_SIDE
        full = (1, 1, dil, n, LANES)
        halo = (1, 1, dil, DIL_SIDE, LANES)

        def cur(t):
            return lambda i, m: (t, i, 0, 0, m)

        def prev(t, nblk=nblk):
            return lambda i, m: (t, jnp.maximum(i - 1, 0), 0, nblk - 1, m)

        def nxt(t):
            return lambda i, m: (t, jnp.minimum(i + 1, nchunks - 1), 0, 0, m)

        in_specs += [pl.BlockSpec(full, cur(0)),
                     pl.BlockSpec(full, cur(1)), pl.BlockSpec(halo, prev(1)), pl.BlockSpec(halo, nxt(1)),
                     pl.BlockSpec(full, cur(2)), pl.BlockSpec(halo, prev(2)), pl.BlockSpec(halo, nxt(2))]
        args += [a] * 7
    ng = len(DIL_DILATIONS)
    return pl.pallas_call(
        functools.partial(_dil_kernel, nchunks=nchunks),
        grid=(nchunks, D_MODEL // LANES),
        in_specs=in_specs,
        out_specs=pl.BlockSpec((DIL_CHUNK, LANES), lambda i, m: (i, m)),
        out_shape=jax.ShapeDtypeStruct((S, D_MODEL), BF16),
        scratch_shapes=[pltpu.VMEM((ng, DIL_CHUNK, LANES), F32),
                        pltpu.VMEM((ng, DIL_CHUNK, LANES), F32)],
        compiler_params=_cparams("parallel", "parallel"),
        name="dilated_attention",
    )(*args)


def _epilogue_kernel(br_ref, h_ref, wt_ref, mk_ref, mv_ref, wo_ref, g_ref, b_ref, o_ref, *, nsplit):
    lo = _lane_lo()
    tr = h_ref.shape[0]
    rs = tr // nsplit

    def pre_norm(rows):
        h = h_ref[rows, :]
        hb = h.astype(BF16)
        gate_b = _dot(hb, wt_ref[:, :D_MODEL])
        yb = (br_ref[rows, :].astype(F32) * (gate_b * jax.nn.sigmoid(gate_b))).astype(BF16)
        y = _dot(yb, wo_ref[:D_MODEL, :])
        tail_m = _dot(hb, wt_ref[:, D_MODEL:])
        for t in range(MEM_WIDTH // LANES):
            cols = slice(t * LANES, (t + 1) * LANES)
            qf = tail_m[:, MEM_WIDTH + t * LANES:MEM_WIDTH + (t + 1) * LANES] * ATTN_SCALE
            kt, vt = mk_ref[:, cols], mv_ref[:, cols]
            outs = []
            for qh in (jnp.where(lo, qf, 0.0), jnp.where(lo, 0.0, qf)):
                s = _dot_nt(qh.astype(BF16), kt)
                m = jnp.max(s, axis=1, keepdims=True)
                p = jnp.exp(s - m)
                l = jnp.sum(p, axis=1, keepdims=True)
                outs.append(_dot(p.astype(BF16), vt) / l)
            mem_out = jnp.where(lo, outs[0], outs[1])
            gm = tail_m[:, cols]
            ym = (mem_out * (gm * jax.nn.sigmoid(gm))).astype(BF16)
            y = y + _dot(ym, wo_ref[D_MODEL + t * LANES:D_MODEL + (t + 1) * LANES, :])
        return DEEPNORM_ALPHA * h + y

    z_next = pre_norm(pl.ds(0, rs))
    for r in range(nsplit):
        z = z_next
        if r + 1 < nsplit:
            z_next = pre_norm(pl.ds((r + 1) * rs, rs))
        o_ref[pl.ds(r * rs, rs), :] = _layer_norm_rows(z, g_ref[...], b_ref[...])


def _epilogue(branch, h, w_tail, mk, mv, w_out, g, b, tail_block=0, tr=1024, nsplit=2):
    S = h.shape[0]
    row = lambda i: (i, 0)
    fixed = lambda i: (0, 0)
    return pl.pallas_call(
        functools.partial(_epilogue_kernel, nsplit=nsplit),
        grid=(S // tr,),
        in_specs=[pl.BlockSpec((tr, D_MODEL), row),
                  pl.BlockSpec((tr, D_MODEL), row),
                  pl.BlockSpec((D_MODEL, TAIL), lambda i: (0, tail_block)),
                  pl.BlockSpec((N_MEM, MEM_WIDTH), fixed),
                  pl.BlockSpec((N_MEM, MEM_WIDTH), fixed),
                  pl.BlockSpec((INNER, D_MODEL), fixed),
                  pl.BlockSpec((1, D_MODEL), fixed),
                  pl.BlockSpec((1, D_MODEL), fixed)],
        out_specs=pl.BlockSpec((tr, D_MODEL), row),
        out_shape=jax.ShapeDtypeStruct((S, D_MODEL), F32),
        compiler_params=_cparams("parallel"),
        name="epilogue",
    )(branch, h, w_tail, mk, mv, w_out, g.reshape(1, -1), b.reshape(1, -1))


def _dup_kv_columns(w):
    d = w.shape[0]
    w4 = w.reshape(d, GQA_KV_HEADS, 1, HEAD_DIM)
    return jnp.broadcast_to(w4, (d, GQA_KV_HEADS, 2, HEAD_DIM)).reshape(d, 2 * GQA_KV_HEADS * HEAD_DIM)


def kernel(x, mem, ln_in_g, ln_in_b, w_mem_kv, w_in_a, w_in_b, q_norm_g, k_norm_g, w_in_c, w_out, ln_g, ln_b):
    B, S, D = x.shape
    assert B == 1 and D == D_MODEL and S % DIL_CHUNK == 0 and S % (FFT_N2 * SUBLANES) == 0

    mkv = _proj(mem[0], w_mem_kv.astype(BF16), name="proj_mem").astype(BF16)
    mk, mv = mkv[:, :MEM_WIDTH], mkv[:, MEM_WIDTH:]
    h = _ln_in(x[0], ln_in_g, ln_in_b)
    fft_tabs = None
    qw = GQA_Q_HEADS * HEAD_DIM
    kvw = GQA_KV_HEADS * HEAD_DIM

    for i in range(DEPTH):
        kind, j = i % 3, i // 3
        tail_block = 0
        if kind == 0:
            if fft_tabs is None:
                fft_tabs = _fft_tables(S)
            fc, gk, hm = fft_tabs
            w_tail = w_in_a[j].astype(BF16)
            branch = _fft_stage2(_fft_stage1(_proj_a(h, fc), gk), hm)
        elif kind == 1:
            w = w_in_b[j]
            half = HEAD_DIM // 4
            w_qk = jnp.concatenate([w[:, :qw], _dup_kv_columns(w[:, qw:qw + kvw])], axis=1)
            w_qkv = jnp.concatenate([_pair_rotary_layout(w_qk, half), w[:, qw + kvw:qw + 2 * kvw]],
                                    axis=1).astype(BF16)
            cos_t, sin_t = (jnp.asarray(_pair_rotary_layout(t, half), F32) for t in _rope_tables_axial(S))
            gq = _pair_rotary_layout(jnp.tile(q_norm_g[j], 2), half).reshape(1, LANES)
            gkn = _pair_rotary_layout(jnp.tile(k_norm_g[j], 2), half).reshape(1, LANES)
            qk, vt = _proj_b(h, w_qkv, cos_t, sin_t, gq, gkn)
            w_tail = w[:, qw + 2 * kvw:].astype(BF16)
            branch = _gqa_attention(qk, vt)
        else:
            half = HEAD_DIM // 2
            n_qk = 2 * len(DIL_DILATIONS) * D_MODEL
            w_qk = _prep_qk_weights(w_in_c[j], n_qk, half)
            w_tail = w_in_c[j][:, n_qk:].astype(BF16)
            tail_block = (w_tail.shape[1] - TAIL) // TAIL
            cos_t, sin_t = (_pair_rotary_layout(t, half) for t in _rope_tables_1d(S))
            qkvs = [_proj_c(h, w_qk, w_tail, cos_t, sin_t, g, dil) for g, dil in enumerate(DIL_DILATIONS)]
            branch = _dilated_attention(qkvs)
        h = _epilogue(branch, h, w_tail, mk, mv, w_out[i].astype(BF16), ln_g[i], ln_b[i], tail_block)
    return h[None]
```

```python
import functools

import jax
import jax.numpy as jnp
import numpy as np
from jax import lax
from jax.experimental import pallas as pl
from jax.experimental.pallas import tpu as pltpu

F32 = jnp.float32
BF16 = jnp.bfloat16

D_MODEL = 1024
DEPTH = 4
N_MEM = 256
GRID_W = 64
HEAD_DIM = 64
ROPE_THETA = 10000.0
LN_EPS = 1e-5
RMS_EPS = 1e-6
MASK_VALUE = -1e30
FNET_GROUPS = 4
FNET_GROUP_W = D_MODEL // FNET_GROUPS
GQA_Q_HEADS = 16
GQA_KV_HEADS = 4
DIL_DILATIONS = (1, 4, 16)
DIL_SIDE = 64
MEM_WIDTH = 256
INNER = D_MODEL + MEM_WIDTH
TAIL = INNER + MEM_WIDTH
DEEPNORM_ALPHA = (2.0 * DEPTH) ** 0.25
ATTN_SCALE = HEAD_DIM ** -0.5
LOG2_E = float(np.log2(np.e))

BF16_SUBLANES = 16
VT_ROWS = HEAD_DIM + BF16_SUBLANES
LANES = 128
SUBLANES = 8
FFT_N2 = 256
DIL_CHUNK = 1024
VMEM_LIMIT = 48 << 20


def _cparams(*sem):
    return pltpu.CompilerParams(dimension_semantics=sem, vmem_limit_bytes=VMEM_LIMIT)


def _dot(a, b):
    return jnp.dot(a, b, preferred_element_type=F32)


def _dot_nt(a, b):
    return lax.dot_general(a, b, (((1,), (1,)), ((), ())), preferred_element_type=F32)


def _lane_lo(shape=(1, LANES)):
    return lax.broadcasted_iota(jnp.int32, shape, len(shape) - 1) % LANES < HEAD_DIM


def _layer_norm_rows(z, g, b):
    mu = jnp.mean(z, axis=-1, keepdims=True)
    zc = z - mu
    var = jnp.mean(zc * zc, axis=-1, keepdims=True)
    return zc * lax.rsqrt(var + LN_EPS) * g + b


def _ln_kernel(x_ref, g_ref, b_ref, o_ref):
    o_ref[...] = _layer_norm_rows(x_ref[...], g_ref[...], b_ref[...])


def _ln_in(x, g, b, tr=512):
    S = x.shape[0]
    return pl.pallas_call(
        _ln_kernel,
        grid=(S // tr,),
        in_specs=[pl.BlockSpec((tr, D_MODEL), lambda i: (i, 0)),
                  pl.BlockSpec((1, D_MODEL), lambda i: (0, 0)),
                  pl.BlockSpec((1, D_MODEL), lambda i: (0, 0))],
        out_specs=pl.BlockSpec((tr, D_MODEL), lambda i: (i, 0)),
        out_shape=jax.ShapeDtypeStruct((S, D_MODEL), F32),
        compiler_params=_cparams("parallel"),
        name="ln_in",
    )(x, g.reshape(1, -1), b.reshape(1, -1))


def _proj_kernel(x_ref, w_ref, o_ref):
    o_ref[...] = _dot(x_ref[...].astype(BF16), w_ref[...])


def _proj(x, w, tr=512, name="proj"):
    S, K = x.shape
    N = w.shape[1]
    tr = min(tr, S)
    return pl.pallas_call(
        _proj_kernel,
        grid=(S // tr,),
        in_specs=[pl.BlockSpec((tr, K), lambda i: (i, 0)),
                  pl.BlockSpec((K, N), lambda i: (0, 0))],
        out_specs=pl.BlockSpec((tr, N), lambda i: (i, 0)),
        out_shape=jax.ShapeDtypeStruct((S, N), F32),
        compiler_params=_cparams("parallel"),
        name=name,
    )(x, w)


def _proj_a_kernel(x_ref, fc_ref, z_ref):
    xb = x_ref[...].astype(BF16)
    for g in range(FNET_GROUPS):
        cols = slice(g * FNET_GROUP_W, (g + 1) * FNET_GROUP_W)
        zg = _dot(xb[:, cols], fc_ref[...])
        z_ref[0, :, cols] = zg[:, :FNET_GROUP_W]
        z_ref[1, :, cols] = zg[:, FNET_GROUP_W:]


def _proj_a(h, fc, tr=512):
    S = h.shape[0]
    return pl.pallas_call(
        _proj_a_kernel,
        grid=(S // tr,),
        in_specs=[pl.BlockSpec((tr, D_MODEL), lambda i: (i, 0)),
                  pl.BlockSpec((FNET_GROUP_W, 2 * FNET_GROUP_W), lambda i: (0, 0))],
        out_specs=pl.BlockSpec((2, tr, D_MODEL), lambda i: (0, i, 0)),
        out_shape=jax.ShapeDtypeStruct((2, S, D_MODEL), F32),
        compiler_params=_cparams("parallel"),
        name="proj_a",
    )(h, fc)


def _fft1_kernel(z_ref, g_ref, t_ref):
    two, n1, _, sub, tc = z_ref.shape
    x = z_ref[...].reshape(two * n1 * sub, tc).astype(BF16)
    t_ref[...] = _dot(g_ref[...], x).reshape(t_ref.shape)


def _fft_stage1(z, gk, tc=1024):
    _, S, D = z.shape
    n1 = S // FFT_N2
    nu = FFT_N2 // SUBLANES
    z5 = z.reshape(2, n1, nu, SUBLANES, D)
    rows = 2 * n1 * SUBLANES
    t5 = pl.pallas_call(
        _fft1_kernel,
        grid=(nu, D // tc),
        in_specs=[pl.BlockSpec((2, n1, 1, SUBLANES, tc), lambda u, c: (0, 0, u, 0, c)),
                  pl.BlockSpec((rows, rows), lambda u, c: (0, 0))],
        out_specs=pl.BlockSpec((n1, 2, 1, SUBLANES, tc), lambda u, c: (0, 0, u, 0, c)),
        out_shape=jax.ShapeDtypeStruct((n1, 2, nu, SUBLANES, D), F32),
        compiler_params=_cparams("parallel", "parallel"),
        name="fft_stage1",
    )(z5, gk)
    return t5.reshape(n1, 2, FFT_N2, D)


def _fft2_kernel(t_ref, h_ref, o_ref, slab_ref):
    nj, _, n2, d = t_ref.shape
    nslab = d // LANES
    for j in range(nj):
        tj = t_ref[j].reshape(2 * n2, d).astype(BF16)
        r = _dot(h_ref[j], tj)
        for s in range(nslab):
            slab_ref[s, pl.ds(j, n2, stride=nj), :] = r[:, s * LANES:(s + 1) * LANES]
    for s in range(nslab):
        o_ref[:, :, s * LANES:(s + 1) * LANES] = slab_ref[s].reshape(n2, nj, LANES)


def _fft_stage2(t, hmat, tc=512):
    n1, _, n2, D = t.shape
    nj = SUBLANES
    y3 = pl.pallas_call(
        _fft2_kernel,
        grid=(n1 // nj, D // tc),
        in_specs=[pl.BlockSpec((nj, 2, n2, tc), lambda a, c: (a, 0, 0, c)),
                  pl.BlockSpec((nj, n2, 2 * n2), lambda a, c: (a, 0, 0))],
        out_specs=pl.BlockSpec((n2, nj, tc), lambda a, c: (0, a, c)),
        out_shape=jax.ShapeDtypeStruct((n2, n1, D), F32),
        scratch_shapes=[pltpu.VMEM((tc // LANES, n2 * nj, LANES), F32)],
        compiler_params=_cparams("parallel", "parallel"),
        name="fft_stage2",
    )(t, hmat)
    return y3.reshape(n2 * n1, D)


def _fft_tables(S):
    n1, n2 = S // FFT_N2, FFT_N2
    c = np.arange(FNET_GROUP_W)
    ang = (2.0 * np.pi / FNET_GROUP_W) * ((c[:, None] * c[None, :]) % FNET_GROUP_W)
    scale = 1.0 / np.sqrt(float(S) * FNET_GROUP_W)
    fc = np.concatenate([np.cos(ang), -np.sin(ang)], axis=1) * scale
    k1 = np.arange(n1)
    th = (2.0 * np.pi / n1) * ((k1[:, None] * k1[None, :]) % n1)
    cs, sn = np.cos(th), np.sin(th)
    g = np.stack([np.stack([cs, sn], axis=1), np.stack([-sn, cs], axis=1)], axis=1)
    gk = np.kron(g.reshape(2 * n1, 2 * n1), np.eye(SUBLANES))
    k2 = np.arange(n2)
    kk = k1[:, None, None] + n1 * k2[None, :, None]
    ph = (2.0 * np.pi / S) * ((k2[None, None, :] * kk) % S)
    hm = np.concatenate([np.cos(ph), np.sin(ph)], axis=2)
    return tuple(jnp.asarray(t, F32).astype(BF16) for t in (fc, gk, hm))


def _rope_angles(pos, dim):
    inv_freq = ROPE_THETA ** (-(np.arange(0, dim, 2, dtype=np.float64) / dim))
    return pos.astype(np.float64)[:, None] * inv_freq[None, :]


def _rope_tables_axial(S):
    t = np.arange(S)
    ar = _rope_angles(t // GRID_W, HEAD_DIM // 2)
    ac = _rope_angles(t % GRID_W, HEAD_DIM // 2)
    cos = np.concatenate([np.cos(ar), np.cos(ar), np.cos(ac), np.cos(ac)], axis=1)
    sin = np.concatenate([-np.sin(ar), np.sin(ar), -np.sin(ac), np.sin(ac)], axis=1)
    return np.tile(cos, (1, 2)), np.tile(sin, (1, 2))


def _rope_tables_1d(S):
    a = _rope_angles(np.arange(S), HEAD_DIM)
    cos = np.concatenate([np.cos(a), np.cos(a)], axis=1)
    sin = np.concatenate([-np.sin(a), np.sin(a)], axis=1)
    return np.tile(cos, (1, 2)), np.tile(sin, (1, 2))


def _pair_rotary_layout(a, half):
    lead = a.shape[:-1]
    a6 = a.reshape(*lead, a.shape[-1] // LANES, 2, HEAD_DIM // (2 * half), 2, half)
    xp = jnp if isinstance(a, jax.Array) else np
    return xp.moveaxis(a6, -2, -4).reshape(*lead, a.shape[-1])


def _pair_rotary_lanes(x, half):
    nblk = LANES // half
    groups = HEAD_DIM // (2 * half)
    lane_blk = lax.broadcasted_iota(jnp.int32, (1, LANES), 1) // half
    moves = {}
    for j in range(nblk):
        fs, rem = divmod(j, nblk // 2)
        head, group = divmod(rem, groups)
        s = head * (HEAD_DIM // half) + 2 * group + fs
        moves.setdefault(((j - s) * half) % LANES, []).append(j)
    out = x
    for shift, dst in moves.items():
        if shift:
            mask = functools.reduce(jnp.logical_or, [lane_blk == j for j in dst])
            out = jnp.where(mask, pltpu.roll(x, shift, 1), out)
    return out


def _prep_qk_kernel(w_ref, o_ref, *, half):
    for t in range(o_ref.shape[1] // LANES):
        cols = slice(t * LANES, (t + 1) * LANES)
        o_ref[:, cols] = _pair_rotary_lanes(w_ref[:, cols], half).astype(BF16)


def _prep_qk_weights(w, ncols, half, tc=1024):
    d = w.shape[0]
    return pl.pallas_call(
        functools.partial(_prep_qk_kernel, half=half),
        grid=(ncols // tc,),
        in_specs=[pl.BlockSpec((d, tc), lambda c: (0, c))],
        out_specs=pl.BlockSpec((d, tc), lambda c: (0, c)),
        out_shape=jax.ShapeDtypeStruct((d, ncols), BF16),
        compiler_params=_cparams("parallel"),
        name="prep_qk_weights",
    )(w)


def _lane_qk():
    return lax.broadcasted_iota(jnp.int32, (1, LANES), 1) % HEAD_DIM < HEAD_DIM // 2


def _rotate_partner(x):
    return pltpu.roll(x, HEAD_DIM, 1)


def _proj_b_kernel(x_ref, w_ref, cos_ref, sin_ref, gq_ref, gk_ref, o_ref, vt_ref):
    xb = x_ref[...].astype(BF16)
    cos_t, sin_t = cos_ref[...], sin_ref[...]
    lo = _lane_qk()
    nb_w = 4 * LANES
    vt = _dot(xb, w_ref[:, 3 * nb_w:]).T.astype(BF16)
    ones = jnp.ones((VT_ROWS - HEAD_DIM, vt.shape[1]), BF16)
    for hd in range(GQA_KV_HEADS):
        vt_ref[hd * VT_ROWS:hd * VT_ROWS + HEAD_DIM, :] = vt[hd * HEAD_DIM:(hd + 1) * HEAD_DIM]
        vt_ref[hd * VT_ROWS + HEAD_DIM:(hd + 1) * VT_ROWS, :] = ones
    for nb in range(3):
        r = _dot(xb, w_ref[:, nb * nb_w:(nb + 1) * nb_w])
        gain = gq_ref[...] if nb < 2 else gk_ref[...]
        scale = ATTN_SCALE * LOG2_E if nb < 2 else 1.0
        for t in range(4):
            rt = r[:, t * LANES:(t + 1) * LANES]
            r2 = rt * rt
            tot = jnp.sum(r2, axis=1, keepdims=True)
            low = jnp.sum(jnp.where(lo, r2, 0.0), axis=1, keepdims=True)
            ss = jnp.where(lo, low, tot - low)
            xn = rt * lax.rsqrt(ss * (1.0 / HEAD_DIM) + RMS_EPS) * gain
            out = (xn * cos_t + _rotate_partner(xn) * sin_t) * scale
            c0 = nb * nb_w + t * LANES
            o_ref[:, c0:c0 + LANES] = out.astype(BF16)


def _proj_b(h, w_qkv, cos_t, sin_t, gq, gk, tr=512):
    S = h.shape[0]
    n_in = w_qkv.shape[1]
    n_out = n_in - GQA_KV_HEADS * HEAD_DIM
    vw = GQA_KV_HEADS * VT_ROWS
    return pl.pallas_call(
        _proj_b_kernel,
        grid=(S // tr,),
        in_specs=[pl.BlockSpec((tr, D_MODEL), lambda i: (i, 0)),
                  pl.BlockSpec((D_MODEL, n_in), lambda i: (0, 0)),
                  pl.BlockSpec((tr, LANES), lambda i: (i, 0)),
                  pl.BlockSpec((tr, LANES), lambda i: (i, 0)),
                  pl.BlockSpec((1, LANES), lambda i: (0, 0)),
                  pl.BlockSpec((1, LANES), lambda i: (0, 0))],
        out_specs=[pl.BlockSpec((tr, n_out), lambda i: (i, 0)),
                   pl.BlockSpec((vw, tr), lambda i: (0, i))],
        out_shape=[jax.ShapeDtypeStruct((S, n_out), BF16),
                   jax.ShapeDtypeStruct((vw, S), BF16)],
        compiler_params=_cparams("parallel"),
        name="proj_b",
    )(h, w_qkv, cos_t, sin_t, gq, gk)


def _gqa_kernel(q_ref, k_ref, vt_ref, o_ref, qs_ref, m_ref, acc_ref, *p_refs, tk):
    tq = q_ref.shape[0]
    S = k_ref.shape[0]
    nq = 4 * tq
    nkv = S // tk
    nbuf = len(p_refs)
    lo = _lane_qk()
    for t in range(2):
        qt = q_ref[:, t * LANES:(t + 1) * LANES].astype(F32)
        qs_ref[(2 * t) * tq:(2 * t + 1) * tq, :] = jnp.where(lo, qt, 0.0).astype(BF16)
        qs_ref[(2 * t + 1) * tq:(2 * t + 2) * tq, :] = jnp.where(lo, 0.0, qt).astype(BF16)

    def scores(j):
        k0 = pl.multiple_of(j * tk, tk)
        return _dot_nt(k_ref[pl.ds(k0, tk), :], qs_ref[...])

    def values(j, p):
        return _dot(vt_ref[:, pl.ds(pl.multiple_of(j * tk, tk), tk)], p)

    def colmax(s):
        return jnp.max(jnp.max(s.reshape(tk // SUBLANES, SUBLANES, nq), axis=0), axis=0, keepdims=True)

    m0 = jnp.broadcast_to(colmax(scores(0)), (tk, nq))

    def probs(u):
        p_refs[u % nbuf][...] = jnp.exp2(scores(u) - m0).astype(BF16)

    probs(0)
    pv = None
    for u in range(nkv):
        if u + 1 < nkv:
            probs(u + 1)
        d = values(u, p_refs[u % nbuf][...])
        pv = d if pv is None else pv + d
    safe = jnp.min(jnp.where(jnp.isfinite(pv), 1.0, 0.0)) > 0.5

    @pl.when(safe)
    def _():
        acc_ref[...] = pv

    @pl.when(jnp.logical_not(safe))
    def _():
        m_ref[...] = jnp.full(m_ref.shape, -jnp.inf, F32)
        acc_ref[...] = jnp.zeros(acc_ref.shape, F32)

        def exact_tile(u, c):
            s = scores(u)
            m_prev = m_ref[...]
            m_cur = jnp.maximum(m_prev, colmax(s))
            d = values(u, jnp.exp2(s - m_cur).astype(BF16))
            acc_ref[...] = jnp.exp2(m_prev - m_cur) * acc_ref[...] + d
            m_ref[...] = m_cur
            return c

        lax.fori_loop(0, nkv, exact_tile, 0)

    acc = acc_ref[...]
    ot = acc[:HEAD_DIM] / acc[HEAD_DIM:HEAD_DIM + 1]
    for t in range(2):
        pair = jnp.concatenate([ot[:, (2 * t) * tq:(2 * t + 1) * tq],
                                ot[:, (2 * t + 1) * tq:(2 * t + 2) * tq]], axis=0)
        o_ref[:, t * LANES:(t + 1) * LANES] = pair.T.astype(o_ref.dtype)


def _gqa_attention(qk, vt, tq=256, tk=256, nbuf=2):
    S = qk.shape[0]
    qw = 4 * HEAD_DIM
    k_blk0 = GQA_Q_HEADS * HEAD_DIM // LANES
    tk = min(tk, S)
    return pl.pallas_call(
        functools.partial(_gqa_kernel, tk=tk),
        grid=(GQA_KV_HEADS, S // tq),
        in_specs=[pl.BlockSpec((tq, qw), lambda h, i: (i, h)),
                  pl.BlockSpec((S, LANES), lambda h, i: (0, k_blk0 + h)),
                  pl.BlockSpec((VT_ROWS, S), lambda h, i: (h, 0))],
        out_specs=pl.BlockSpec((tq, qw), lambda h, i: (i, h)),
        out_shape=jax.ShapeDtypeStruct((S, D_MODEL), F32),
        scratch_shapes=[pltpu.VMEM((4 * tq, LANES), BF16),
                        pltpu.VMEM((1, 4 * tq), F32),
                        pltpu.VMEM((VT_ROWS, 4 * tq), F32)]
        + [pltpu.VMEM((tk, 4 * tq), BF16)] * nbuf,
        compiler_params=_cparams("parallel", "parallel"),
        name="gqa_attention",
    )(qk, qk, vt)


def _proj_c_kernel(x_ref, wqk_ref, wv_ref, cos_ref, sin_ref, o_ref, xp_ref, slab_ref, *, dil):
    kind = pl.program_id(1)
    tm, d = x_ref.shape[0], o_ref.shape[-1]
    n = tm // dil
    nslab = d // LANES
    cb_w = 2 * LANES

    @pl.when(kind == 0)
    def _():
        if dil == 1:
            xp_ref[...] = x_ref[...].astype(BF16)
            return
        for s in range(nslab):
            slab_ref[s] = x_ref[:, s * LANES:(s + 1) * LANES]
        for c in range(dil):
            for s in range(nslab):
                xp_ref[c * n:(c + 1) * n, s * LANES:(s + 1) * LANES] = (
                    slab_ref[s, pl.ds(c, n, stride=dil), :].astype(BF16))

    def emit(c0, vals):
        o_ref[0, 0, :, :, c0:c0 + vals.shape[1]] = vals.astype(BF16).reshape(dil, n, vals.shape[1])

    @pl.when(kind < 2)
    def _():
        scale = jnp.where(kind == 0, ATTN_SCALE * LOG2_E, 1.0).astype(F32)
        cos_t = cos_ref[...] * scale
        sin_t = sin_ref[...] * scale
        ncb = d // cb_w
        r_next = _dot(xp_ref[...], wqk_ref[:, :cb_w])
        for cb in range(ncb):
            r = r_next
            if cb + 1 < ncb:
                r_next = _dot(xp_ref[...], wqk_ref[:, (cb + 1) * cb_w:(cb + 2) * cb_w])
            for s in range(cb_w // LANES):
                rt = r[:, s * LANES:(s + 1) * LANES]
                emit(cb * cb_w + s * LANES, rt * cos_t + _rotate_partner(rt) * sin_t)

    @pl.when(kind == 2)
    def _():
        for cb in range(d // cb_w):
            emit(cb * cb_w, _dot(xp_ref[...], wv_ref[:, cb * cb_w:(cb + 1) * cb_w]))


def _regroup_rows(tab, dil):
    S, w = tab.shape
    return tab.reshape(S // DIL_CHUNK, DIL_CHUNK // dil, dil, w).transpose(0, 2, 1, 3).reshape(S, w)


def _proj_c(h, w_qk, w_v, cos_np, sin_np, group, dil):
    S = h.shape[0]
    tm = DIL_CHUNK
    n = tm // dil
    ngroups = len(DIL_DILATIONS)
    cos_t = jnp.asarray(_regroup_rows(cos_np, dil), F32)
    sin_t = jnp.asarray(_regroup_rows(sin_np, dil), F32)
    return pl.pallas_call(
        functools.partial(_proj_c_kernel, dil=dil),
        grid=(S // tm, 3),
        in_specs=[pl.BlockSpec((tm, D_MODEL), lambda i, t: (i, 0)),
                  pl.BlockSpec((D_MODEL, D_MODEL), lambda i, t: (0, ngroups * jnp.minimum(t, 1) + group)),
                  pl.BlockSpec((D_MODEL, D_MODEL), lambda i, t: (0, group)),
                  pl.BlockSpec((tm, LANES), lambda i, t: (i, 0)),
                  pl.BlockSpec((tm, LANES), lambda i, t: (i, 0))],
        out_specs=pl.BlockSpec((1, 1, dil, n, D_MODEL), lambda i, t: (t, i, 0, 0, 0)),
        out_shape=jax.ShapeDtypeStruct((3, S // tm, dil, n, D_MODEL), BF16),
        scratch_shapes=[pltpu.VMEM((tm, D_MODEL), BF16),
                        pltpu.VMEM((D_MODEL // LANES, tm, LANES), F32)],
        compiler_params=_cparams("parallel", "arbitrary"),
        name=f"proj_c_dil{dil}",
    )(h, w_qk, w_v, cos_t, sin_t)


def _dil_kernel(*refs, nchunks):
    ng = len(DIL_DILATIONS)
    o_ref, osc_ref, lsc_ref = refs[7 * ng:]
    i = pl.program_id(0)
    lo = _lane_lo()
    lo_qk = _lane_qk()

    groups, blocks = [], []
    for g, dil in enumerate(DIL_DILATIONS):
        n = DIL_CHUNK // dil
        bq = min(n, 2 * DIL_SIDE)
        nk = bq + 2 * DIL_SIDE
        qi = lax.broadcasted_iota(jnp.int32, (2 * bq, nk), 0) % bq
        kj = lax.broadcasted_iota(jnp.int32, (2 * bq, nk), 1)
        band = jnp.where(jnp.abs(kj - DIL_SIDE - qi) <= DIL_SIDE, 0.0, MASK_VALUE).astype(F32)
        groups.append(dict(dil=dil, n=n, bq=bq, nk=nk, nsub=n // bq, kj=kj, band=band,
                           kpad=-(n + 2 * DIL_SIDE) % LANES, refs=refs[7 * g:7 * g + 7]))
        blocks += [(g, c, b) for c in range(dil) for b in range(n // bq)]

    class_kv = {}

    def keys_values(g, c):
        if (g, c) not in class_kv:
            G = groups[g]
            _, kc_ref, kp_ref, kn_ref, vc_ref, vp_ref, vn_ref = G["refs"]
            kparts = [kp_ref[0, 0, c], kc_ref[0, 0, c], kn_ref[0, 0, c]]
            if G["kpad"]:
                kparts.append(jnp.zeros((G["kpad"], LANES), BF16))
            kcat_t = jnp.concatenate(kparts, axis=0).astype(F32).T.astype(BF16)
            vcat = jnp.concatenate([vp_ref[0, 0, c], vc_ref[0, 0, c], vn_ref[0, 0, c]], axis=0)
            class_kv[(g, c)] = (kcat_t, vcat)
        return class_kv[(g, c)]

    def scores(blk):
        g, c, b = blk
        G = groups[g]
        bq, nk, kj = G["bq"], G["nk"], G["kj"]
        p0 = b * bq
        bias = G["band"]
        if b == 0:
            bias = jnp.where(kj >= jnp.where(i == 0, DIL_SIDE, 0), bias, MASK_VALUE)
        if b == G["nsub"] - 1:
            bias = jnp.where(kj < jnp.where(i == nchunks - 1, bq + DIL_SIDE, nk), bias, MASK_VALUE)
        qf = G["refs"][0][0, 0, c, p0:p0 + bq, :].astype(F32)
        qs = jnp.concatenate([jnp.where(lo_qk, qf, 0.0), jnp.where(lo_qk, 0.0, qf)], axis=0)
        return _dot(qs.astype(BF16), keys_values(g, c)[0][:, p0:p0 + nk]) + bias

    def finish(blk, s):
        g, c, b = blk
        G = groups[g]
        bq, nk, dil = G["bq"], G["nk"], G["dil"]
        p0 = b * bq
        m = jnp.max(s, axis=1, keepdims=True)
        p = jnp.exp2(s - m)
        l = jnp.sum(p, axis=1, keepdims=True)
        o2 = _dot(p.astype(BF16), keys_values(g, c)[1][p0:p0 + nk]) / l
        lse2 = m + jnp.log2(l)
        o_pair = jnp.where(lo, o2[:bq], o2[bq:])
        l_pair = jnp.where(lo, lse2[:bq], lse2[bq:])
        rows = pl.ds(p0, bq) if dil == 1 else pl.ds(p0 * dil + c, bq, stride=dil)
        osc_ref[g, rows, :] = o_pair
        lsc_ref[g, rows, :] = l_pair

    s_next = scores(blocks[0])
    for idx, blk in enumerate(blocks):
        s_cur = s_next
        if idx + 1 < len(blocks):
            s_next = scores(blocks[idx + 1])
        finish(blk, s_cur)

    ls = [lsc_ref[g] for g in range(ng)]
    mx = functools.reduce(jnp.maximum, ls)
    ws = [jnp.exp2(l - mx) for l in ls]
    num = sum(w * osc_ref[g] for g, w in enumerate(ws))
    o_ref[...] = (num / sum(ws)).astype(o_ref.dtype)


def _dilated_attention(qkvs):
    nchunks = qkvs[0].shape[1]
    S = nchunks * DIL_CHUNK
    in_specs, args = [], []
    for dil, a in zip(DIL_DILATIONS, qkvs):
        n = DIL_CHUNK // dil
        nblk = n // DIL_SIDE
        full = (1, 1, dil, n, LANES)
        halo = (1, 1, dil, DIL_SIDE, LANES)

        def cur(t):
            return lambda i, m: (t, i, 0, 0, m)

        def prev(t, nblk=nblk):
            return lambda i, m: (t, jnp.maximum(i - 1, 0), 0, nblk - 1, m)

        def nxt(t):
            return lambda i, m: (t, jnp.minimum(i + 1, nchunks - 1), 0, 0, m)

        in_specs += [pl.BlockSpec(full, cur(0)),
                     pl.BlockSpec(full, cur(1)), pl.BlockSpec(halo, prev(1)), pl.BlockSpec(halo, nxt(1)),
                     pl.BlockSpec(full, cur(2)), pl.BlockSpec(halo, prev(2)), pl.BlockSpec(halo, nxt(2))]
        args += [a] * 7
    ng = len(DIL_DILATIONS)
    return pl.pallas_call(
        functools.partial(_dil_kernel, nchunks=nchunks),
        grid=(nchunks, D_MODEL // LANES),
        in_specs=in_specs,
        out_specs=pl.BlockSpec((DIL_CHUNK, LANES), lambda i, m: (i, m)),
        out_shape=jax.ShapeDtypeStruct((S, D_MODEL), F32),
        scratch_shapes=[pltpu.VMEM((ng, DIL_CHUNK, LANES), F32),
                        pltpu.VMEM((ng, DIL_CHUNK, LANES), F32)],
        compiler_params=_cparams("parallel", "parallel"),
        name="dilated_attention",
    )(*args)


def _epilogue_kernel(br_ref, h_ref, wt_ref, mk_ref, mv_ref, wo_ref, g_ref, b_ref, o_ref, *, nsplit):
    lo = _lane_lo()
    tr = h_ref.shape[0]
    rs = tr // nsplit

    def pre_norm(rows):
        h = h_ref[rows, :]
        hb = h.astype(BF16)
        gate_b = _dot(hb, wt_ref[:, :D_MODEL])
        yb = (br_ref[rows, :].astype(F32) * (gate_b * jax.nn.sigmoid(gate_b))).astype(BF16)
        y = _dot(yb, wo_ref[:D_MODEL, :])
        tail_m = _dot(hb, wt_ref[:, D_MODEL:])
        for t in range(MEM_WIDTH // LANES):
            cols = slice(t * LANES, (t + 1) * LANES)
            qf = tail_m[:, MEM_WIDTH + t * LANES:MEM_WIDTH + (t + 1) * LANES] * ATTN_SCALE
            kt, vt = mk_ref[:, cols], mv_ref[:, cols]
            outs = []
            for qh in (jnp.where(lo, qf, 0.0), jnp.where(lo, 0.0, qf)):
                s = _dot_nt(qh.astype(BF16), kt)
                m = jnp.max(s, axis=1, keepdims=True)
                p = jnp.exp(s - m)
                l = jnp.sum(p, axis=1, keepdims=True)
                outs.append(_dot(p.astype(BF16), vt) / l)
            mem_out = jnp.where(lo, outs[0], outs[1])
            gm = tail_m[:, cols]
            ym = (mem_out * (gm * jax.nn.sigmoid(gm))).astype(BF16)
            y = y + _dot(ym, wo_ref[D_MODEL + t * LANES:D_MODEL + (t + 1) * LANES, :])
        return DEEPNORM_ALPHA * h + y

    z_next = pre_norm(pl.ds(0, rs))
    for r in range(nsplit):
        z = z_next
        if r + 1 < nsplit:
            z_next = pre_norm(pl.ds((r + 1) * rs, rs))
        o_ref[pl.ds(r * rs, rs), :] = _layer_norm_rows(z, g_ref[...], b_ref[...])


def _epilogue(branch, h, w_tail, mk, mv, w_out, g, b, tail_block=0, tr=1024, nsplit=2):
    S = h.shape[0]
    row = lambda i: (i, 0)
    fixed = lambda i: (0, 0)
    return pl.pallas_call(
        functools.partial(_epilogue_kernel, nsplit=nsplit),
        grid=(S // tr,),
        in_specs=[pl.BlockSpec((tr, D_MODEL), row),
                  pl.BlockSpec((tr, D_MODEL), row),
                  pl.BlockSpec((D_MODEL, TAIL), lambda i: (0, tail_block)),
                  pl.BlockSpec((N_MEM, MEM_WIDTH), fixed),
                  pl.BlockSpec((N_MEM, MEM_WIDTH), fixed),
                  pl.BlockSpec((INNER, D_MODEL), fixed),
                  pl.BlockSpec((1, D_MODEL), fixed),
                  pl.BlockSpec((1, D_MODEL), fixed)],
        out_specs=pl.BlockSpec((tr, D_MODEL), row),
        out_shape=jax.ShapeDtypeStruct((S, D_MODEL), F32),
        compiler_params=_cparams("parallel"),
        name="epilogue",
    )(branch, h, w_tail, mk, mv, w_out, g.reshape(1, -1), b.reshape(1, -1))


def _dup_kv_columns(w):
    d = w.shape[0]
    w4 = w.reshape(d, GQA_KV_HEADS, 1, HEAD_DIM)
    return jnp.broadcast_to(w4, (d, GQA_KV_HEADS, 2, HEAD_DIM)).reshape(d, 2 * GQA_KV_HEADS * HEAD_DIM)


def kernel(x, mem, ln_in_g, ln_in_b, w_mem_kv, w_in_a, w_in_b, q_norm_g, k_norm_g, w_in_c, w_out, ln_g, ln_b):
    B, S, D = x.shape
    assert B == 1 and D == D_MODEL and S % DIL_CHUNK == 0 and S % (FFT_N2 * SUBLANES) == 0

    mkv = _proj(mem[0], w_mem_kv.astype(BF16), name="proj_mem").astype(BF16)
    mk, mv = mkv[:, :MEM_WIDTH], mkv[:, MEM_WIDTH:]
    h = _ln_in(x[0], ln_in_g, ln_in_b)
    fft_tabs = None
    qw = GQA_Q_HEADS * HEAD_DIM
    kvw = GQA_KV_HEADS * HEAD_DIM

    for i in range(DEPTH):
        kind, j = i % 3, i // 3
        tail_block = 0
        if kind == 0:
            if fft_tabs is None:
                fft_tabs = _fft_tables(S)
            fc, gk, hm = fft_tabs
            w_tail = w_in_a[j].astype(BF16)
            branch = _fft_stage2(_fft_stage1(_proj_a(h, fc), gk), hm)
        elif kind == 1:
            w = w_in_b[j]
            half = HEAD_DIM // 4
            w_qk = jnp.concatenate([w[:, :qw], _dup_kv_columns(w[:, qw:qw + kvw])], axis=1)
            w_qkv = jnp.concatenate([_pair_rotary_layout(w_qk, half), w[:, qw + kvw:qw + 2 * kvw]],
                                    axis=1).astype(BF16)
            cos_t, sin_t = (jnp.asarray(_pair_rotary_layout(t, half), F32) for t in _rope_tables_axial(S))
            gq = _pair_rotary_layout(jnp.tile(q_norm_g[j], 2), half).reshape(1, LANES)
            gkn = _pair_rotary_layout(jnp.tile(k_norm_g[j], 2), half).reshape(1, LANES)
            qk, vt = _proj_b(h, w_qkv, cos_t, sin_t, gq, gkn)
            w_tail = w[:, qw + 2 * kvw:].astype(BF16)
            branch = _gqa_attention(qk, vt)
        else:
            half = HEAD_DIM // 2
            n_qk = 2 * len(DIL_DILATIONS) * D_MODEL
            w_qk = _prep_qk_weights(w_in_c[j], n_qk, half)
            w_tail = w_in_c[j][:, n_qk:].astype(BF16)
            tail_block = (w_tail.shape[1] - TAIL) // TAIL
            cos_t, sin_t = (_pair_rotary_layout(t, half) for t in _rope_tables_1d(S))
            qkvs = [_proj_c(h, w_qk, w_tail, cos_t, sin_t, g, dil) for g, dil in enumerate(DIL_DILATIONS)]
            branch = _dilated_attention(qkvs)
        h = _epilogue(branch, h, w_tail, mk, mv, w_out[i].astype(BF16), ln_g[i], ln_b[i], tail_block)
    return h[None]
```

```python
import functools

import jax
import jax.numpy as jnp
import numpy as np
from jax import lax
from jax.experimental import pallas as pl
from jax.experimental.pallas import tpu as pltpu

F32 = jnp.float32
BF16 = jnp.bfloat16

D_MODEL = 1024
DEPTH = 4
N_MEM = 256
GRID_W = 64
HEAD_DIM = 64
ROPE_THETA = 10000.0
LN_EPS = 1e-5
RMS_EPS = 1e-6
MASK_VALUE = -1e30
FNET_GROUPS = 4
FNET_GROUP_W = D_MODEL // FNET_GROUPS
GQA_Q_HEADS = 16
GQA_KV_HEADS = 4
DIL_DILATIONS = (1, 4, 16)
DIL_SIDE = 64
MEM_WIDTH = 256
INNER = D_MODEL + MEM_WIDTH
TAIL = INNER + MEM_WIDTH
DEEPNORM_ALPHA = (2.0 * DEPTH) ** 0.25
ATTN_SCALE = HEAD_DIM ** -0.5
LOG2_E = float(np.log2(np.e))

BF16_SUBLANES = 16
VT_ROWS = HEAD_DIM + BF16_SUBLANES
LANES = 128
SUBLANES = 8
FFT_N2 = 256
DIL_CHUNK = 1024
VMEM_LIMIT = 48 << 20


def _cparams(*sem):
    return pltpu.CompilerParams(dimension_semantics=sem, vmem_limit_bytes=VMEM_LIMIT)


def _dot(a, b):
    return jnp.dot(a, b, preferred_element_type=F32)


def _dot_nt(a, b):
    return lax.dot_general(a, b, (((1,), (1,)), ((), ())), preferred_element_type=F32)


def _lane_lo(shape=(1, LANES)):
    return lax.broadcasted_iota(jnp.int32, shape, len(shape) - 1) % LANES < HEAD_DIM


def _layer_norm_rows(z, g, b):
    mu = jnp.mean(z, axis=-1, keepdims=True)
    zc = z - mu
    var = jnp.mean(zc * zc, axis=-1, keepdims=True)
    return zc * lax.rsqrt(var + LN_EPS) * g + b


def _ln_kernel(x_ref, g_ref, b_ref, o_ref):
    o_ref[...] = _layer_norm_rows(x_ref[...], g_ref[...], b_ref[...])


def _ln_in(x, g, b, tr=512):
    S = x.shape[0]
    return pl.pallas_call(
        _ln_kernel,
        grid=(S // tr,),
        in_specs=[pl.BlockSpec((tr, D_MODEL), lambda i: (i, 0)),
                  pl.BlockSpec((1, D_MODEL), lambda i: (0, 0)),
                  pl.BlockSpec((1, D_MODEL), lambda i: (0, 0))],
        out_specs=pl.BlockSpec((tr, D_MODEL), lambda i: (i, 0)),
        out_shape=jax.ShapeDtypeStruct((S, D_MODEL), F32),
        compiler_params=_cparams("parallel"),
        name="ln_in",
    )(x, g.reshape(1, -1), b.reshape(1, -1))


def _proj_kernel(x_ref, w_ref, o_ref):
    o_ref[...] = _dot(x_ref[...].astype(BF16), w_ref[...])


def _proj(x, w, tr=512, name="proj"):
    S, K = x.shape
    N = w.shape[1]
    tr = min(tr, S)
    return pl.pallas_call(
        _proj_kernel,
        grid=(S // tr,),
        in_specs=[pl.BlockSpec((tr, K), lambda i: (i, 0)),
                  pl.BlockSpec((K, N), lambda i: (0, 0))],
        out_specs=pl.BlockSpec((tr, N), lambda i: (i, 0)),
        out_shape=jax.ShapeDtypeStruct((S, N), F32),
        compiler_params=_cparams("parallel"),
        name=name,
    )(x, w)


def _proj_a_kernel(x_ref, fc_ref, z_ref):
    xb = x_ref[...].astype(BF16)
    for g in range(FNET_GROUPS):
        cols = slice(g * FNET_GROUP_W, (g + 1) * FNET_GROUP_W)
        zg = _dot(xb[:, cols], fc_ref[...])
        z_ref[0, :, cols] = zg[:, :FNET_GROUP_W]
        z_ref[1, :, cols] = zg[:, FNET_GROUP_W:]


def _proj_a(h, fc, tr=512):
    S = h.shape[0]
    return pl.pallas_call(
        _proj_a_kernel,
        grid=(S // tr,),
        in_specs=[pl.BlockSpec((tr, D_MODEL), lambda i: (i, 0)),
                  pl.BlockSpec((FNET_GROUP_W, 2 * FNET_GROUP_W), lambda i: (0, 0))],
        out_specs=pl.BlockSpec((2, tr, D_MODEL), lambda i: (0, i, 0)),
        out_shape=jax.ShapeDtypeStruct((2, S, D_MODEL), F32),
        compiler_params=_cparams("parallel"),
        name="proj_a",
    )(h, fc)


def _fft1_kernel(z_ref, g_ref, t_ref):
    two, n1, _, sub, tc = z_ref.shape
    x = z_ref[...].reshape(two * n1 * sub, tc).astype(BF16)
    t_ref[...] = _dot(g_ref[...], x).reshape(t_ref.shape)


def _fft_stage1(z, gk, tc=1024):
    _, S, D = z.shape
    n1 = S // FFT_N2
    nu = FFT_N2 // SUBLANES
    z5 = z.reshape(2, n1, nu, SUBLANES, D)
    rows = 2 * n1 * SUBLANES
    t5 = pl.pallas_call(
        _fft1_kernel,
        grid=(nu, D // tc),
        in_specs=[pl.BlockSpec((2, n1, 1, SUBLANES, tc), lambda u, c: (0, 0, u, 0, c)),
                  pl.BlockSpec((rows, rows), lambda u, c: (0, 0))],
        out_specs=pl.BlockSpec((n1, 2, 1, SUBLANES, tc), lambda u, c: (0, 0, u, 0, c)),
        out_shape=jax.ShapeDtypeStruct((n1, 2, nu, SUBLANES, D), F32),
        compiler_params=_cparams("parallel", "parallel"),
        name="fft_stage1",
    )(z5, gk)
    return t5.reshape(n1, 2, FFT_N2, D)


def _fft2_kernel(t_ref, h_ref, o_ref, slab_ref):
    nj, _, n2, d = t_ref.shape
    nslab = d // LANES
    for j in range(nj):
        tj = t_ref[j].reshape(2 * n2, d).astype(BF16)
        r = _dot(h_ref[j], tj)
        for s in range(nslab):
            slab_ref[s, pl.ds(j, n2, stride=nj), :] = r[:, s * LANES:(s + 1) * LANES]
    for s in range(nslab):
        o_ref[:, :, s * LANES:(s + 1) * LANES] = slab_ref[s].reshape(n2, nj, LANES)


def _fft_stage2(t, hmat, tc=512):
    n1, _, n2, D = t.shape
    nj = SUBLANES
    y3 = pl.pallas_call(
        _fft2_kernel,
        grid=(n1 // nj, D // tc),
        in_specs=[pl.BlockSpec((nj, 2, n2, tc), lambda a, c: (a, 0, 0, c)),
                  pl.BlockSpec((nj, n2, 2 * n2), lambda a, c: (a, 0, 0))],
        out_specs=pl.BlockSpec((n2, nj, tc), lambda a, c: (0, a, c)),
        out_shape=jax.ShapeDtypeStruct((n2, n1, D), F32),
        scratch_shapes=[pltpu.VMEM((tc // LANES, n2 * nj, LANES), F32)],
        compiler_params=_cparams("parallel", "parallel"),
        name="fft_stage2",
    )(t, hmat)
    return y3.reshape(n2 * n1, D)


def _fft_tables(S):
    n1, n2 = S // FFT_N2, FFT_N2
    c = np.arange(FNET_GROUP_W)
    ang = (2.0 * np.pi / FNET_GROUP_W) * ((c[:, None] * c[None, :]) % FNET_GROUP_W)
    scale = 1.0 / np.sqrt(float(S) * FNET_GROUP_W)
    fc = np.concatenate([np.cos(ang), -np.sin(ang)], axis=1) * scale
    k1 = np.arange(n1)
    th = (2.0 * np.pi / n1) * ((k1[:, None] * k1[None, :]) % n1)
    cs, sn = np.cos(th), np.sin(th)
    g = np.stack([np.stack([cs, sn], axis=1), np.stack([-sn, cs], axis=1)], axis=1)
    gk = np.kron(g.reshape(2 * n1, 2 * n1), np.eye(SUBLANES))
    k2 = np.arange(n2)
    kk = k1[:, None, None] + n1 * k2[None, :, None]
    ph = (2.0 * np.pi / S) * ((k2[None, None, :] * kk) % S)
    hm = np.concatenate([np.cos(ph), np.sin(ph)], axis=2)
    return tuple(jnp.asarray(t, F32).astype(BF16) for t in (fc, gk, hm))


def _rope_angles(pos, dim):
    inv_freq = ROPE_THETA ** (-(np.arange(0, dim, 2, dtype=np.float64) / dim))
    return pos.astype(np.float64)[:, None] * inv_freq[None, :]


def _rope_tables_axial(S):
    t = np.arange(S)
    ar = _rope_angles(t // GRID_W, HEAD_DIM // 2)
    ac = _rope_angles(t % GRID_W, HEAD_DIM // 2)
    cos = np.concatenate([np.cos(ar), np.cos(ar), np.cos(ac), np.cos(ac)], axis=1)
    sin = np.concatenate([-np.sin(ar), np.sin(ar), -np.sin(ac), np.sin(ac)], axis=1)
    return np.tile(cos, (1, 2)), np.tile(sin, (1, 2))


def _rope_tables_1d(S):
    a = _rope_angles(np.arange(S), HEAD_DIM)
    cos = np.concatenate([np.cos(a), np.cos(a)], axis=1)
    sin = np.concatenate([-np.sin(a), np.sin(a)], axis=1)
    return np.tile(cos, (1, 2)), np.tile(sin, (1, 2))


def _pair_rotary_layout(a, half):
    lead = a.shape[:-1]
    a6 = a.reshape(*lead, a.shape[-1] // LANES, 2, HEAD_DIM // (2 * half), 2, half)
    xp = jnp if isinstance(a, jax.Array) else np
    return xp.moveaxis(a6, -2, -4).reshape(*lead, a.shape[-1])


def _pair_rotary_lanes(x, half):
    nblk = LANES // half
    groups = HEAD_DIM // (2 * half)
    lane_blk = lax.broadcasted_iota(jnp.int32, (1, LANES), 1) // half
    moves = {}
    for j in range(nblk):
        fs, rem = divmod(j, nblk // 2)
        head, group = divmod(rem, groups)
        s = head * (HEAD_DIM // half) + 2 * group + fs
        moves.setdefault(((j - s) * half) % LANES, []).append(j)
    out = x
    for shift, dst in moves.items():
        if shift:
            mask = functools.reduce(jnp.logical_or, [lane_blk == j for j in dst])
            out = jnp.where(mask, pltpu.roll(x, shift, 1), out)
    return out


def _prep_qk_kernel(w_ref, o_ref, *, half):
    for t in range(o_ref.shape[1] // LANES):
        cols = slice(t * LANES, (t + 1) * LANES)
        o_ref[:, cols] = _pair_rotary_lanes(w_ref[:, cols], half).astype(BF16)


def _prep_qk_weights(w, ncols, half, tc=1024):
    d = w.shape[0]
    return pl.pallas_call(
        functools.partial(_prep_qk_kernel, half=half),
        grid=(ncols // tc,),
        in_specs=[pl.BlockSpec((d, tc), lambda c: (0, c))],
        out_specs=pl.BlockSpec((d, tc), lambda c: (0, c)),
        out_shape=jax.ShapeDtypeStruct((d, ncols), BF16),
        compiler_params=_cparams("parallel"),
        name="prep_qk_weights",
    )(w)


def _lane_qk():
    return lax.broadcasted_iota(jnp.int32, (1, LANES), 1) % HEAD_DIM < HEAD_DIM // 2


def _rotate_partner(x):
    return pltpu.roll(x, HEAD_DIM, 1)


def _proj_b_kernel(x_ref, w_ref, cos_ref, sin_ref, gq_ref, gk_ref, o_ref, vt_ref):
    xb = x_ref[...].astype(BF16)
    cos_t, sin_t = cos_ref[...], sin_ref[...]
    lo = _lane_qk()
    nb_w = 4 * LANES
    vt = _dot(xb, w_ref[:, 3 * nb_w:]).T.astype(BF16)
    ones = jnp.ones((VT_ROWS - HEAD_DIM, vt.shape[1]), BF16)
    for hd in range(GQA_KV_HEADS):
        vt_ref[hd * VT_ROWS:hd * VT_ROWS + HEAD_DIM, :] = vt[hd * HEAD_DIM:(hd + 1) * HEAD_DIM]
        vt_ref[hd * VT_ROWS + HEAD_DIM:(hd + 1) * VT_ROWS, :] = ones
    for nb in range(3):
        r = _dot(xb, w_ref[:, nb * nb_w:(nb + 1) * nb_w])
        gain = gq_ref[...] if nb < 2 else gk_ref[...]
        scale = ATTN_SCALE * LOG2_E if nb < 2 else 1.0
        for t in range(4):
            rt = r[:, t * LANES:(t + 1) * LANES]
            r2 = rt * rt
            tot = jnp.sum(r2, axis=1, keepdims=True)
            low = jnp.sum(jnp.where(lo, r2, 0.0), axis=1, keepdims=True)
            ss = jnp.where(lo, low, tot - low)
            xn = rt * lax.rsqrt(ss * (1.0 / HEAD_DIM) + RMS_EPS) * gain
            out = (xn * cos_t + _rotate_partner(xn) * sin_t) * scale
            c0 = nb * nb_w + t * LANES
            o_ref[:, c0:c0 + LANES] = out.astype(BF16)


def _proj_b(h, w_qkv, cos_t, sin_t, gq, gk, tr=512):
    S = h.shape[0]
    n_in = w_qkv.shape[1]
    n_out = n_in - GQA_KV_HEADS * HEAD_DIM
    vw = GQA_KV_HEADS * VT_ROWS
    return pl.pallas_call(
        _proj_b_kernel,
        grid=(S // tr,),
        in_specs=[pl.BlockSpec((tr, D_MODEL), lambda i: (i, 0)),
                  pl.BlockSpec((D_MODEL, n_in), lambda i: (0, 0)),
                  pl.BlockSpec((tr, LANES), lambda i: (i, 0)),
                  pl.BlockSpec((tr, LANES), lambda i: (i, 0)),
                  pl.BlockSpec((1, LANES), lambda i: (0, 0)),
                  pl.BlockSpec((1, LANES), lambda i: (0, 0))],
        out_specs=[pl.BlockSpec((tr, n_out), lambda i: (i, 0)),
                   pl.BlockSpec((vw, tr), lambda i: (0, i))],
        out_shape=[jax.ShapeDtypeStruct((S, n_out), BF16),
                   jax.ShapeDtypeStruct((vw, S), BF16)],
        compiler_params=_cparams("parallel"),
        name="proj_b",
    )(h, w_qkv, cos_t, sin_t, gq, gk)


def _gqa_kernel(q_ref, k_ref, vt_ref, o_ref, qs_ref, m_ref, acc_ref, *p_refs, tk):
    tq = q_ref.shape[0]
    S = k_ref.shape[0]
    nq = 4 * tq
    nkv = S // tk
    nbuf = len(p_refs)
    lo = _lane_qk()
    for t in range(2):
        qt = q_ref[:, t * LANES:(t + 1) * LANES].astype(F32)
        qs_ref[(2 * t) * tq:(2 * t + 1) * tq, :] = jnp.where(lo, qt, 0.0).astype(BF16)
        qs_ref[(2 * t + 1) * tq:(2 * t + 2) * tq, :] = jnp.where(lo, 0.0, qt).astype(BF16)

    def scores(j):
        k0 = pl.multiple_of(j * tk, tk)
        return _dot_nt(k_ref[pl.ds(k0, tk), :], qs_ref[...])

    def values(j, p):
        return _dot(vt_ref[:, pl.ds(pl.multiple_of(j * tk, tk), tk)], p)

    def colmax(s):
        return jnp.max(jnp.max(s.reshape(tk // SUBLANES, SUBLANES, nq), axis=0), axis=0, keepdims=True)

    m0 = jnp.broadcast_to(colmax(scores(0)), (tk, nq))

    def probs(u):
        p_refs[u % nbuf][...] = jnp.exp2(scores(u) - m0).astype(BF16)

    probs(0)
    pv = None
    for u in range(nkv):
        if u + 1 < nkv:
            probs(u + 1)
        d = values(u, p_refs[u % nbuf][...])
        pv = d if pv is None else pv + d
    safe = jnp.min(jnp.where(jnp.isfinite(pv), 1.0, 0.0)) > 0.5

    @pl.when(safe)
    def _():
        acc_ref[...] = pv

    @pl.when(jnp.logical_not(safe))
    def _():
        m_ref[...] = jnp.full(m_ref.shape, -jnp.inf, F32)
        acc_ref[...] = jnp.zeros(acc_ref.shape, F32)

        def exact_tile(u, c):
            s = scores(u)
            m_prev = m_ref[...]
            m_cur = jnp.maximum(m_prev, colmax(s))
            d = values(u, jnp.exp2(s - m_cur).astype(BF16))
            acc_ref[...] = jnp.exp2(m_prev - m_cur) * acc_ref[...] + d
            m_ref[...] = m_cur
            return c

        lax.fori_loop(0, nkv, exact_tile, 0)

    acc = acc_ref[...]
    ot = acc[:HEAD_DIM] / acc[HEAD_DIM:HEAD_DIM + 1]
    for t in range(2):
        pair = jnp.concatenate([ot[:, (2 * t) * tq:(2 * t + 1) * tq],
                                ot[:, (2 * t + 1) * tq:(2 * t + 2) * tq]], axis=0)
        o_ref[:, t * LANES:(t + 1) * LANES] = pair.T.astype(o_ref.dtype)


def _gqa_attention(qk, vt, tq=256, tk=256, nbuf=2):
    S = qk.shape[0]
    qw = 4 * HEAD_DIM
    k_blk0 = GQA_Q_HEADS * HEAD_DIM // LANES
    tk = min(tk, S)
    return pl.pallas_call(
        functools.partial(_gqa_kernel, tk=tk),
        grid=(GQA_KV_HEADS, S // tq),
        in_specs=[pl.BlockSpec((tq, qw), lambda h, i: (i, h)),
                  pl.BlockSpec((S, LANES), lambda h, i: (0, k_blk0 + h)),
                  pl.BlockSpec((VT_ROWS, S), lambda h, i: (h, 0))],
        out_specs=pl.BlockSpec((tq, qw), lambda h, i: (i, h)),
        out_shape=jax.ShapeDtypeStruct((S, D_MODEL), F32),
        scratch_shapes=[pltpu.VMEM((4 * tq, LANES), BF16),
                        pltpu.VMEM((1, 4 * tq), F32),
                        pltpu.VMEM((VT_ROWS, 4 * tq), F32)]
        + [pltpu.VMEM((tk, 4 * tq), BF16)] * nbuf,
        compiler_params=_cparams("parallel", "parallel"),
        name="gqa_attention",
    )(qk, qk, vt)


def _proj_c_kernel(x_ref, wqk_ref, wv_ref, cos_ref, sin_ref, o_ref, xp_ref, slab_ref, *, dil):
    kind = pl.program_id(1)
    tm, d = x_ref.shape[0], o_ref.shape[-1]
    n = tm // dil
    nslab = d // LANES
    cb_w = 2 * LANES

    @pl.when(kind == 0)
    def _():
        if dil == 1:
            xp_ref[...] = x_ref[...].astype(BF16)
            return
        for s in range(nslab):
            slab_ref[s] = x_ref[:, s * LANES:(s + 1) * LANES]
        for c in range(dil):
            for s in range(nslab):
                xp_ref[c * n:(c + 1) * n, s * LANES:(s + 1) * LANES] = (
                    slab_ref[s, pl.ds(c, n, stride=dil), :].astype(BF16))

    def emit(c0, vals):
        o_ref[0, 0, :, :, c0:c0 + vals.shape[1]] = vals.astype(BF16).reshape(dil, n, vals.shape[1])

    @pl.when(kind < 2)
    def _():
        scale = jnp.where(kind == 0, ATTN_SCALE * LOG2_E, 1.0).astype(F32)
        cos_t = cos_ref[...] * scale
        sin_t = sin_ref[...] * scale
        ncb = d // cb_w
        r_next = _dot(xp_ref[...], wqk_ref[:, :cb_w])
        for cb in range(ncb):
            r = r_next
            if cb + 1 < ncb:
                r_next = _dot(xp_ref[...], wqk_ref[:, (cb + 1) * cb_w:(cb + 2) * cb_w])
            for s in range(cb_w // LANES):
                rt = r[:, s * LANES:(s + 1) * LANES]
                emit(cb * cb_w + s * LANES, rt * cos_t + _rotate_partner(rt) * sin_t)

    @pl.when(kind == 2)
    def _():
        for cb in range(d // cb_w):
            emit(cb * cb_w, _dot(xp_ref[...], wv_ref[:, cb * cb_w:(cb + 1) * cb_w]))


def _regroup_rows(tab, dil):
    S, w = tab.shape
    return tab.reshape(S // DIL_CHUNK, DIL_CHUNK // dil, dil, w).transpose(0, 2, 1, 3).reshape(S, w)


def _proj_c(h, w_qk, w_v, cos_np, sin_np, group, dil):
    S = h.shape[0]
    tm = DIL_CHUNK
    n = tm // dil
    ngroups = len(DIL_DILATIONS)
    cos_t = jnp.asarray(_regroup_rows(cos_np, dil), F32)
    sin_t = jnp.asarray(_regroup_rows(sin_np, dil), F32)
    return pl.pallas_call(
        functools.partial(_proj_c_kernel, dil=dil),
        grid=(S // tm, 3),
        in_specs=[pl.BlockSpec((tm, D_MODEL), lambda i, t: (i, 0)),
                  pl.BlockSpec((D_MODEL, D_MODEL), lambda i, t: (0, ngroups * jnp.minimum(t, 1) + group)),
                  pl.BlockSpec((D_MODEL, D_MODEL), lambda i, t: (0, group)),
                  pl.BlockSpec((tm, LANES), lambda i, t: (i, 0)),
                  pl.BlockSpec((tm, LANES), lambda i, t: (i, 0))],
        out_specs=pl.BlockSpec((1, 1, dil, n, D_MODEL), lambda i, t: (t, i, 0, 0, 0)),
        out_shape=jax.ShapeDtypeStruct((3, S // tm, dil, n, D_MODEL), BF16),
        scratch_shapes=[pltpu.VMEM((tm, D_MODEL), BF16),
                        pltpu.VMEM((D_MODEL // LANES, tm, LANES), F32)],
        compiler_params=_cparams("parallel", "arbitrary"),
        name=f"proj_c_dil{dil}",
    )(h, w_qk, w_v, cos_t, sin_t)


def _dil_kernel(*refs, nchunks):
    ng = len(DIL_DILATIONS)
    o_ref, osc_ref, lsc_ref = refs[7 * ng:]
    i = pl.program_id(0)
    lo = _lane_lo()
    lo_qk = _lane_qk()

    groups, blocks = [], []
    for g, dil in enumerate(DIL_DILATIONS):
        n = DIL_CHUNK // dil
        bq = min(n, 2 * DIL_SIDE)
        nk = bq + 2 * DIL_SIDE
        qi = lax.broadcasted_iota(jnp.int32, (2 * bq, nk), 0) % bq
        kj = lax.broadcasted_iota(jnp.int32, (2 * bq, nk), 1)
        band = jnp.where(jnp.abs(kj - DIL_SIDE - qi) <= DIL_SIDE, 0.0, MASK_VALUE).astype(F32)
        groups.append(dict(dil=dil, n=n, bq=bq, nk=nk, nsub=n // bq, kj=kj, band=band,
                           kpad=-(n + 2 * DIL_SIDE) % LANES, refs=refs[7 * g:7 * g + 7]))
        blocks += [(g, c, b) for c in range(dil) for b in range(n // bq)]

    class_kv = {}

    def keys_values(g, c):
        if (g, c) not in class_kv:
            G = groups[g]
            _, kc_ref, kp_ref, kn_ref, vc_ref, vp_ref, vn_ref = G["refs"]
            kparts = [kp_ref[0, 0, c], kc_ref[0, 0, c], kn_ref[0, 0, c]]
            if G["kpad"]:
                kparts.append(jnp.zeros((G["kpad"], LANES), BF16))
            kcat_t = jnp.concatenate(kparts, axis=0).astype(F32).T.astype(BF16)
            vcat = jnp.concatenate([vp_ref[0, 0, c], vc_ref[0, 0, c], vn_ref[0, 0, c]], axis=0)
            class_kv[(g, c)] = (kcat_t, vcat)
        return class_kv[(g, c)]

    def scores(blk):
        g, c, b = blk
        G = groups[g]
        bq, nk, kj = G["bq"], G["nk"], G["kj"]
        p0 = b * bq
        bias = G["band"]
        if b == 0:
            bias = jnp.where(kj >= jnp.where(i == 0, DIL_SIDE, 0), bias, MASK_VALUE)
        if b == G["nsub"] - 1:
            bias = jnp.where(kj < jnp.where(i == nchunks - 1, bq + DIL_SIDE, nk), bias, MASK_VALUE)
        qf = G["refs"][0][0, 0, c, p0:p0 + bq, :].astype(F32)
        qs = jnp.concatenate([jnp.where(lo_qk, qf, 0.0), jnp.where(lo_qk, 0.0, qf)], axis=0)
        return _dot(qs.astype(BF16), keys_values(g, c)[0][:, p0:p0 + nk]) + bias

    def finish(blk, s):
        g, c, b = blk
        G = groups[g]
        bq, nk, dil = G["bq"], G["nk"], G["dil"]
        p0 = b * bq
        m = jnp.max(s, axis=1, keepdims=True)
        p = jnp.exp2(s - m)
        l = jnp.sum(p, axis=1, keepdims=True)
        o2 = _dot(p.astype(BF16), keys_values(g, c)[1][p0:p0 + nk]) / l
        lse2 = m + jnp.log2(l)
        o_pair = jnp.where(lo, o2[:bq], o2[bq:])
        l_pair = jnp.where(lo, lse2[:bq], lse2[bq:])
        rows = pl.ds(p0, bq) if dil == 1 else pl.ds(p0 * dil + c, bq, stride=dil)
        osc_ref[g, rows, :] = o_pair
        lsc_ref[g, rows, :] = l_pair

    s_next = scores(blocks[0])
    for idx, blk in enumerate(blocks):
        s_cur = s_next
        if idx + 1 < len(blocks):
            s_next = scores(blocks[idx + 1])
        finish(blk, s_cur)

    ls = [lsc_ref[g] for g in range(ng)]
    mx = functools.reduce(jnp.maximum, ls)
    ws = [jnp.exp2(l - mx) for l in ls]
    num = sum(w * osc_ref[g] for g, w in enumerate(ws))
    o_ref[...] = (num / sum(ws)).astype(o_ref.dtype)


def _dilated_attention(qkvs):
    nchunks = qkvs[0].shape[1]
    S = nchunks * DIL_CHUNK
    in_specs, args = [], []
    for dil, a in zip(DIL_DILATIONS, qkvs):
        n = DIL_CHUNK // dil
        nblk = n // DIL_SIDE
        full = (1, 1, dil, n, LANES)
        halo = (1, 1, dil, DIL_SIDE, LANES)

        def cur(t):
            return lambda i, m: (t, i, 0, 0, m)

        def prev(t, nblk=nblk):
            return lambda i, m: (t, jnp.maximum(i - 1, 0), 0, nblk - 1, m)

        def nxt(t):
            return lambda i, m: (t, jnp.minimum(i + 1, nchunks - 1), 0, 0, m)

        in_specs += [pl.BlockSpec(full, cur(0)),
                     pl.BlockSpec(full, cur(1)), pl.BlockSpec(halo, prev(1)), pl.BlockSpec(halo, nxt(1)),
                     pl.BlockSpec(full, cur(2)), pl.BlockSpec(halo, prev(2)), pl.BlockSpec(halo, nxt(2))]
        args += [a] * 7
    ng = len(DIL_DILATIONS)
    return pl.pallas_call(
        functools.partial(_dil_kernel, nchunks=nchunks),
        grid=(nchunks, D_MODEL // LANES),
        in_specs=in_specs,
        out_specs=pl.BlockSpec((DIL_CHUNK, LANES), lambda i, m: (i, m)),
        out_shape=jax.ShapeDtypeStruct((S, D_MODEL), F32),
        scratch_shapes=[pltpu.VMEM((ng, DIL_CHUNK, LANES), F32),
                        pltpu.VMEM((ng, DIL_CHUNK, LANES), F32)],
        compiler_params=_cparams("parallel", "parallel"),
        name="dilated_attention",
    )(*args)


def _epilogue_kernel(br_ref, h_ref, wt_ref, mk_ref, mv_ref, wo_ref, g_ref, b_ref, o_ref, *, nsplit):
    lo = _lane_lo()
    tr = h_ref.shape[0]
    rs = tr // nsplit

    def pre_norm(rows):
        h = h_ref[rows, :]
        hb = h.astype(BF16)
        gate_b = _dot(hb, wt_ref[:, :D_MODEL])
        yb = (br_ref[rows, :].astype(F32) * (gate_b * jax.nn.sigmoid(gate_b))).astype(BF16)
        y = _dot(yb, wo_ref[:D_MODEL, :])
        tail_m = _dot(hb, wt_ref[:, D_MODEL:])
        yms = []
        for t in range(MEM_WIDTH // LANES):
            cols = slice(t * LANES, (t + 1) * LANES)
            qf = tail_m[:, MEM_WIDTH + t * LANES:MEM_WIDTH + (t + 1) * LANES] * ATTN_SCALE
            kt, vt = mk_ref[:, cols], mv_ref[:, cols]
            outs = []
            for qh in (jnp.where(lo, qf, 0.0), jnp.where(lo, 0.0, qf)):
                s = _dot_nt(qh.astype(BF16), kt)
                m = jnp.max(s, axis=1, keepdims=True)
                p = jnp.exp(s - m)
                l = jnp.sum(p, axis=1, keepdims=True)
                outs.append(_dot(p.astype(BF16), vt) / l)
            mem_out = jnp.where(lo, outs[0], outs[1])
            gm = tail_m[:, cols]
            yms.append((mem_out * (gm * jax.nn.sigmoid(gm))).astype(BF16))
        y = y + _dot(jnp.concatenate(yms, axis=1), wo_ref[D_MODEL:, :])
        return DEEPNORM_ALPHA * h + y

    z_next = pre_norm(pl.ds(0, rs))
    for r in range(nsplit):
        z = z_next
        if r + 1 < nsplit:
            z_next = pre_norm(pl.ds((r + 1) * rs, rs))
        o_ref[pl.ds(r * rs, rs), :] = _layer_norm_rows(z, g_ref[...], b_ref[...])


def _epilogue(branch, h, w_tail, mk, mv, w_out, g, b, tail_block=0, tr=1024, nsplit=2):
    S = h.shape[0]
    row = lambda i: (i, 0)
    fixed = lambda i: (0, 0)
    return pl.pallas_call(
        functools.partial(_epilogue_kernel, nsplit=nsplit),
        grid=(S // tr,),
        in_specs=[pl.BlockSpec((tr, D_MODEL), row),
                  pl.BlockSpec((tr, D_MODEL), row),
                  pl.BlockSpec((D_MODEL, TAIL), lambda i: (0, tail_block)),
                  pl.BlockSpec((N_MEM, MEM_WIDTH), fixed),
                  pl.BlockSpec((N_MEM, MEM_WIDTH), fixed),
                  pl.BlockSpec((INNER, D_MODEL), fixed),
                  pl.BlockSpec((1, D_MODEL), fixed),
                  pl.BlockSpec((1, D_MODEL), fixed)],
        out_specs=pl.BlockSpec((tr, D_MODEL), row),
        out_shape=jax.ShapeDtypeStruct((S, D_MODEL), F32),
        compiler_params=_cparams("parallel"),
        name="epilogue",
    )(branch, h, w_tail, mk, mv, w_out, g.reshape(1, -1), b.reshape(1, -1))


def _dup_kv_columns(w):
    d = w.shape[0]
    w4 = w.reshape(d, GQA_KV_HEADS, 1, HEAD_DIM)
    return jnp.broadcast_to(w4, (d, GQA_KV_HEADS, 2, HEAD_DIM)).reshape(d, 2 * GQA_KV_HEADS * HEAD_DIM)


def kernel(x, mem, ln_in_g, ln_in_b, w_mem_kv, w_in_a, w_in_b, q_norm_g, k_norm_g, w_in_c, w_out, ln_g, ln_b):
    B, S, D = x.shape
    assert B == 1 and D == D_MODEL and S % DIL_CHUNK == 0 and S % (FFT_N2 * SUBLANES) == 0

    mkv = _proj(mem[0], w_mem_kv.astype(BF16), name="proj_mem").astype(BF16)
    mk, mv = mkv[:, :MEM_WIDTH], mkv[:, MEM_WIDTH:]
    h = _ln_in(x[0], ln_in_g, ln_in_b)
    fft_tabs = None
    qw = GQA_Q_HEADS * HEAD_DIM
    kvw = GQA_KV_HEADS * HEAD_DIM

    for i in range(DEPTH):
        kind, j = i % 3, i // 3
        tail_block = 0
        if kind == 0:
            if fft_tabs is None:
                fft_tabs = _fft_tables(S)
            fc, gk, hm = fft_tabs
            w_tail = w_in_a[j].astype(BF16)
            branch = _fft_stage2(_fft_stage1(_proj_a(h, fc), gk), hm)
        elif kind == 1:
            w = w_in_b[j]
            half = HEAD_DIM // 4
            w_qk = jnp.concatenate([w[:, :qw], _dup_kv_columns(w[:, qw:qw + kvw])], axis=1)
            w_qkv = jnp.concatenate([_pair_rotary_layout(w_qk, half), w[:, qw + kvw:qw + 2 * kvw]],
                                    axis=1).astype(BF16)
            cos_t, sin_t = (jnp.asarray(_pair_rotary_layout(t, half), F32) for t in _rope_tables_axial(S))
            gq = _pair_rotary_layout(jnp.tile(q_norm_g[j], 2), half).reshape(1, LANES)
            gkn = _pair_rotary_layout(jnp.tile(k_norm_g[j], 2), half).reshape(1, LANES)
            qk, vt = _proj_b(h, w_qkv, cos_t, sin_t, gq, gkn)
            w_tail = w[:, qw + 2 * kvw:].astype(BF16)
            branch = _gqa_attention(qk, vt)
        else:
            half = HEAD_DIM // 2
            n_qk = 2 * len(DIL_DILATIONS) * D_MODEL
            w_qk = _prep_qk_weights(w_in_c[j], n_qk, half)
            w_tail = w_in_c[j][:, n_qk:].astype(BF16)
            tail_block = (w_tail.shape[1] - TAIL) // TAIL
            cos_t, sin_t = (_pair_rotary_layout(t, half) for t in _rope_tables_1d(S))
            qkvs = [_proj_c(h, w_qk, w_tail, cos_t, sin_t, g, dil) for g, dil in enumerate(DIL_DILATIONS)]
            branch = _dilated_attention(qkvs)
        h = _epilogue(branch, h, w_tail, mk, mv, w_out[i].astype(BF16), ln_g[i], ln_b[i], tail_block)
    return h[None]
```

```python
import functools

import jax
import jax.numpy as jnp
import numpy as np
from jax import lax
from jax.experimental import pallas as pl
from jax.experimental.pallas import tpu as pltpu

F32 = jnp.float32
BF16 = jnp.bfloat16

D_MODEL = 1024
DEPTH = 4
N_MEM = 256
GRID_W = 64
HEAD_DIM = 64
ROPE_THETA = 10000.0
LN_EPS = 1e-5
RMS_EPS = 1e-6
MASK_VALUE = -1e30
FNET_GROUPS = 4
FNET_GROUP_W = D_MODEL // FNET_GROUPS
GQA_Q_HEADS = 16
GQA_KV_HEADS = 4
DIL_DILATIONS = (1, 4, 16)
DIL_SIDE = 64
MEM_WIDTH = 256
INNER = D_MODEL + MEM_WIDTH
TAIL = INNER + MEM_WIDTH
DEEPNORM_ALPHA = (2.0 * DEPTH) ** 0.25
ATTN_SCALE = HEAD_DIM ** -0.5
LOG2_E = float(np.log2(np.e))

BF16_SUBLANES = 16
VT_ROWS = HEAD_DIM + BF16_SUBLANES
LANES = 128
SUBLANES = 8
FFT_N2 = 256
DIL_CHUNK = 1024
VMEM_LIMIT = 48 << 20


def _cparams(*sem):
    return pltpu.CompilerParams(dimension_semantics=sem, vmem_limit_bytes=VMEM_LIMIT)


def _dot(a, b):
    return jnp.dot(a, b, preferred_element_type=F32)


def _dot_nt(a, b):
    return lax.dot_general(a, b, (((1,), (1,)), ((), ())), preferred_element_type=F32)


def _lane_lo(shape=(1, LANES)):
    return lax.broadcasted_iota(jnp.int32, shape, len(shape) - 1) % LANES < HEAD_DIM


def _layer_norm_rows(z, g, b):
    mu = jnp.mean(z, axis=-1, keepdims=True)
    zc = z - mu
    var = jnp.mean(zc * zc, axis=-1, keepdims=True)
    return zc * lax.rsqrt(var + LN_EPS) * g + b


def _ln_kernel(x_ref, g_ref, b_ref, o_ref):
    o_ref[...] = _layer_norm_rows(x_ref[...], g_ref[...], b_ref[...])


def _ln_in(x, g, b, tr=512):
    S = x.shape[0]
    return pl.pallas_call(
        _ln_kernel,
        grid=(S // tr,),
        in_specs=[pl.BlockSpec((tr, D_MODEL), lambda i: (i, 0)),
                  pl.BlockSpec((1, D_MODEL), lambda i: (0, 0)),
                  pl.BlockSpec((1, D_MODEL), lambda i: (0, 0))],
        out_specs=pl.BlockSpec((tr, D_MODEL), lambda i: (i, 0)),
        out_shape=jax.ShapeDtypeStruct((S, D_MODEL), F32),
        compiler_params=_cparams("parallel"),
        name="ln_in",
    )(x, g.reshape(1, -1), b.reshape(1, -1))


def _proj_kernel(x_ref, w_ref, o_ref):
    o_ref[...] = _dot(x_ref[...].astype(BF16), w_ref[...])


def _proj(x, w, tr=512, name="proj"):
    S, K = x.shape
    N = w.shape[1]
    tr = min(tr, S)
    return pl.pallas_call(
        _proj_kernel,
        grid=(S // tr,),
        in_specs=[pl.BlockSpec((tr, K), lambda i: (i, 0)),
                  pl.BlockSpec((K, N), lambda i: (0, 0))],
        out_specs=pl.BlockSpec((tr, N), lambda i: (i, 0)),
        out_shape=jax.ShapeDtypeStruct((S, N), F32),
        compiler_params=_cparams("parallel"),
        name=name,
    )(x, w)


def _proj_a_kernel(x_ref, fc_ref, z_ref):
    xb = x_ref[...].astype(BF16)
    for g in range(FNET_GROUPS):
        cols = slice(g * FNET_GROUP_W, (g + 1) * FNET_GROUP_W)
        zg = _dot(xb[:, cols], fc_ref[...])
        z_ref[0, :, cols] = zg[:, :FNET_GROUP_W]
        z_ref[1, :, cols] = zg[:, FNET_GROUP_W:]


def _proj_a(h, fc, tr=512):
    S = h.shape[0]
    return pl.pallas_call(
        _proj_a_kernel,
        grid=(S // tr,),
        in_specs=[pl.BlockSpec((tr, D_MODEL), lambda i: (i, 0)),
                  pl.BlockSpec((FNET_GROUP_W, 2 * FNET_GROUP_W), lambda i: (0, 0))],
        out_specs=pl.BlockSpec((2, tr, D_MODEL), lambda i: (0, i, 0)),
        out_shape=jax.ShapeDtypeStruct((2, S, D_MODEL), F32),
        compiler_params=_cparams("parallel"),
        name="proj_a",
    )(h, fc)


def _fft1_kernel(z_ref, g_ref, t_ref):
    two, n1, _, sub, tc = z_ref.shape
    x = z_ref[...].reshape(two * n1 * sub, tc).astype(BF16)
    t_ref[...] = _dot(g_ref[...], x).reshape(t_ref.shape)


def _fft_stage1(z, gk, tc=1024):
    _, S, D = z.shape
    n1 = S // FFT_N2
    nu = FFT_N2 // SUBLANES
    z5 = z.reshape(2, n1, nu, SUBLANES, D)
    rows = 2 * n1 * SUBLANES
    t5 = pl.pallas_call(
        _fft1_kernel,
        grid=(nu, D // tc),
        in_specs=[pl.BlockSpec((2, n1, 1, SUBLANES, tc), lambda u, c: (0, 0, u, 0, c)),
                  pl.BlockSpec((rows, rows), lambda u, c: (0, 0))],
        out_specs=pl.BlockSpec((n1, 2, 1, SUBLANES, tc), lambda u, c: (0, 0, u, 0, c)),
        out_shape=jax.ShapeDtypeStruct((n1, 2, nu, SUBLANES, D), F32),
        compiler_params=_cparams("parallel", "parallel"),
        name="fft_stage1",
    )(z5, gk)
    return t5.reshape(n1, 2, FFT_N2, D)


def _fft2_kernel(t_ref, h_ref, o_ref, slab_ref):
    nj, _, n2, d = t_ref.shape
    nslab = d // LANES
    for j in range(nj):
        tj = t_ref[j].reshape(2 * n2, d).astype(BF16)
        r = _dot(h_ref[j], tj)
        for s in range(nslab):
            slab_ref[s, pl.ds(j, n2, stride=nj), :] = r[:, s * LANES:(s + 1) * LANES]
    for s in range(nslab):
        o_ref[:, :, s * LANES:(s + 1) * LANES] = slab_ref[s].reshape(n2, nj, LANES)


def _fft_stage2(t, hmat, tc=512):
    n1, _, n2, D = t.shape
    nj = SUBLANES
    y3 = pl.pallas_call(
        _fft2_kernel,
        grid=(n1 // nj, D // tc),
        in_specs=[pl.BlockSpec((nj, 2, n2, tc), lambda a, c: (a, 0, 0, c)),
                  pl.BlockSpec((nj, n2, 2 * n2), lambda a, c: (a, 0, 0))],
        out_specs=pl.BlockSpec((n2, nj, tc), lambda a, c: (0, a, c)),
        out_shape=jax.ShapeDtypeStruct((n2, n1, D), F32),
        scratch_shapes=[pltpu.VMEM((tc // LANES, n2 * nj, LANES), F32)],
        compiler_params=_cparams("parallel", "parallel"),
        name="fft_stage2",
    )(t, hmat)
    return y3.reshape(n2 * n1, D)


def _fft_tables(S):
    n1, n2 = S // FFT_N2, FFT_N2
    c = np.arange(FNET_GROUP_W)
    ang = (2.0 * np.pi / FNET_GROUP_W) * ((c[:, None] * c[None, :]) % FNET_GROUP_W)
    scale = 1.0 / np.sqrt(float(S) * FNET_GROUP_W)
    fc = np.concatenate([np.cos(ang), -np.sin(ang)], axis=1) * scale
    k1 = np.arange(n1)
    th = (2.0 * np.pi / n1) * ((k1[:, None] * k1[None, :]) % n1)
    cs, sn = np.cos(th), np.sin(th)
    g = np.stack([np.stack([cs, sn], axis=1), np.stack([-sn, cs], axis=1)], axis=1)
    gk = np.kron(g.reshape(2 * n1, 2 * n1), np.eye(SUBLANES))
    k2 = np.arange(n2)
    kk = k1[:, None, None] + n1 * k2[None, :, None]
    ph = (2.0 * np.pi / S) * ((k2[None, None, :] * kk) % S)
    hm = np.concatenate([np.cos(ph), np.sin(ph)], axis=2)
    return tuple(jnp.asarray(t, F32).astype(BF16) for t in (fc, gk, hm))


def _rope_angles(pos, dim):
    inv_freq = ROPE_THETA ** (-(np.arange(0, dim, 2, dtype=np.float64) / dim))
    return pos.astype(np.float64)[:, None] * inv_freq[None, :]


def _rope_tables_axial(S):
    t = np.arange(S)
    ar = _rope_angles(t // GRID_W, HEAD_DIM // 2)
    ac = _rope_angles(t % GRID_W, HEAD_DIM // 2)
    cos = np.concatenate([np.cos(ar), np.cos(ar), np.cos(ac), np.cos(ac)], axis=1)
    sin = np.concatenate([-np.sin(ar), np.sin(ar), -np.sin(ac), np.sin(ac)], axis=1)
    return np.tile(cos, (1, 2)), np.tile(sin, (1, 2))


def _rope_tables_1d(S):
    a = _rope_angles(np.arange(S), HEAD_DIM)
    cos = np.concatenate([np.cos(a), np.cos(a)], axis=1)
    sin = np.concatenate([-np.sin(a), np.sin(a)], axis=1)
    return np.tile(cos, (1, 2)), np.tile(sin, (1, 2))


def _pair_rotary_layout(a, half):
    lead = a.shape[:-1]
    a6 = a.reshape(*lead, a.shape[-1] // LANES, 2, HEAD_DIM // (2 * half), 2, half)
    xp = jnp if isinstance(a, jax.Array) else np
    return xp.moveaxis(a6, -2, -4).reshape(*lead, a.shape[-1])


def _pair_rotary_lanes(x, half):
    nblk = LANES // half
    groups = HEAD_DIM // (2 * half)
    lane_blk = lax.broadcasted_iota(jnp.int32, (1, LANES), 1) // half
    moves = {}
    for j in range(nblk):
        fs, rem = divmod(j, nblk // 2)
        head, group = divmod(rem, groups)
        s = head * (HEAD_DIM // half) + 2 * group + fs
        moves.setdefault(((j - s) * half) % LANES, []).append(j)
    out = x
    for shift, dst in moves.items():
        if shift:
            mask = functools.reduce(jnp.logical_or, [lane_blk == j for j in dst])
            out = jnp.where(mask, pltpu.roll(x, shift, 1), out)
    return out


def _prep_qk_kernel(w_ref, o_ref, *, half):
    for t in range(o_ref.shape[1] // LANES):
        cols = slice(t * LANES, (t + 1) * LANES)
        o_ref[:, cols] = _pair_rotary_lanes(w_ref[:, cols], half).astype(BF16)


def _prep_qk_weights(w, ncols, half, tc=1024):
    d = w.shape[0]
    return pl.pallas_call(
        functools.partial(_prep_qk_kernel, half=half),
        grid=(ncols // tc,),
        in_specs=[pl.BlockSpec((d, tc), lambda c: (0, c))],
        out_specs=pl.BlockSpec((d, tc), lambda c: (0, c)),
        out_shape=jax.ShapeDtypeStruct((d, ncols), BF16),
        compiler_params=_cparams("parallel"),
        name="prep_qk_weights",
    )(w)


def _lane_qk():
    return lax.broadcasted_iota(jnp.int32, (1, LANES), 1) % HEAD_DIM < HEAD_DIM // 2


def _rotate_partner(x):
    return pltpu.roll(x, HEAD_DIM, 1)


def _proj_b_kernel(x_ref, w_ref, cos_ref, sin_ref, gq_ref, gk_ref, o_ref, vt_ref):
    xb = x_ref[...].astype(BF16)
    cos_t, sin_t = cos_ref[...], sin_ref[...]
    lo = _lane_qk()
    nb_w = 4 * LANES
    vt = _dot(xb, w_ref[:, 3 * nb_w:]).T.astype(BF16)
    ones = jnp.ones((VT_ROWS - HEAD_DIM, vt.shape[1]), BF16)
    for hd in range(GQA_KV_HEADS):
        vt_ref[hd * VT_ROWS:hd * VT_ROWS + HEAD_DIM, :] = vt[hd * HEAD_DIM:(hd + 1) * HEAD_DIM]
        vt_ref[hd * VT_ROWS + HEAD_DIM:(hd + 1) * VT_ROWS, :] = ones
    for nb in range(3):
        r = _dot(xb, w_ref[:, nb * nb_w:(nb + 1) * nb_w])
        gain = gq_ref[...] if nb < 2 else gk_ref[...]
        scale = ATTN_SCALE * LOG2_E if nb < 2 else 1.0
        for t in range(4):
            rt = r[:, t * LANES:(t + 1) * LANES]
            r2 = rt * rt
            tot = jnp.sum(r2, axis=1, keepdims=True)
            low = jnp.sum(jnp.where(lo, r2, 0.0), axis=1, keepdims=True)
            ss = jnp.where(lo, low, tot - low)
            xn = rt * lax.rsqrt(ss * (1.0 / HEAD_DIM) + RMS_EPS) * gain
            out = (xn * cos_t + _rotate_partner(xn) * sin_t) * scale
            c0 = nb * nb_w + t * LANES
            o_ref[:, c0:c0 + LANES] = out.astype(BF16)


def _proj_b(h, w_qkv, cos_t, sin_t, gq, gk, tr=512):
    S = h.shape[0]
    n_in = w_qkv.shape[1]
    n_out = n_in - GQA_KV_HEADS * HEAD_DIM
    vw = GQA_KV_HEADS * VT_ROWS
    return pl.pallas_call(
        _proj_b_kernel,
        grid=(S // tr,),
        in_specs=[pl.BlockSpec((tr, D_MODEL), lambda i: (i, 0)),
                  pl.BlockSpec((D_MODEL, n_in), lambda i: (0, 0)),
                  pl.BlockSpec((tr, LANES), lambda i: (i, 0)),
                  pl.BlockSpec((tr, LANES), lambda i: (i, 0)),
                  pl.BlockSpec((1, LANES), lambda i: (0, 0)),
                  pl.BlockSpec((1, LANES), lambda i: (0, 0))],
        out_specs=[pl.BlockSpec((tr, n_out), lambda i: (i, 0)),
                   pl.BlockSpec((vw, tr), lambda i: (0, i))],
        out_shape=[jax.ShapeDtypeStruct((S, n_out), BF16),
                   jax.ShapeDtypeStruct((vw, S), BF16)],
        compiler_params=_cparams("parallel"),
        name="proj_b",
    )(h, w_qkv, cos_t, sin_t, gq, gk)


def _gqa_kernel(q_ref, k_ref, vt_ref, o_ref, qs_ref, m_ref, acc_ref, *p_refs, tk):
    tq = q_ref.shape[0]
    S = k_ref.shape[0]
    nq = 4 * tq
    nkv = S // tk
    nbuf = len(p_refs)
    lo = _lane_qk()
    for t in range(2):
        qt = q_ref[:, t * LANES:(t + 1) * LANES].astype(F32)
        qs_ref[(2 * t) * tq:(2 * t + 1) * tq, :] = jnp.where(lo, qt, 0.0).astype(BF16)
        qs_ref[(2 * t + 1) * tq:(2 * t + 2) * tq, :] = jnp.where(lo, 0.0, qt).astype(BF16)

    def scores(j):
        k0 = pl.multiple_of(j * tk, tk)
        return _dot_nt(k_ref[pl.ds(k0, tk), :], qs_ref[...])

    def values(j, p):
        return _dot(vt_ref[:, pl.ds(pl.multiple_of(j * tk, tk), tk)], p)

    def colmax(s):
        return jnp.max(jnp.max(s.reshape(tk // SUBLANES, SUBLANES, nq), axis=0), axis=0, keepdims=True)

    m0 = jnp.broadcast_to(colmax(scores(0)), (tk, nq))

    def probs(u):
        p_refs[u % nbuf][...] = jnp.exp2(scores(u) - m0).astype(BF16)

    probs(0)
    pv = None
    for u in range(nkv):
        if u + 1 < nkv:
            probs(u + 1)
        d = values(u, p_refs[u % nbuf][...])
        pv = d if pv is None else pv + d
    safe = jnp.min(jnp.where(jnp.isfinite(pv), 1.0, 0.0)) > 0.5

    @pl.when(safe)
    def _():
        acc_ref[...] = pv

    @pl.when(jnp.logical_not(safe))
    def _():
        m_ref[...] = jnp.full(m_ref.shape, -jnp.inf, F32)
        acc_ref[...] = jnp.zeros(acc_ref.shape, F32)

        def exact_tile(u, c):
            s = scores(u)
            m_prev = m_ref[...]
            m_cur = jnp.maximum(m_prev, colmax(s))
            d = values(u, jnp.exp2(s - m_cur).astype(BF16))
            acc_ref[...] = jnp.exp2(m_prev - m_cur) * acc_ref[...] + d
            m_ref[...] = m_cur
            return c

        lax.fori_loop(0, nkv, exact_tile, 0)

    acc = acc_ref[...]
    ot = acc[:HEAD_DIM] / acc[HEAD_DIM:HEAD_DIM + 1]
    for t in range(2):
        pair = jnp.concatenate([ot[:, (2 * t) * tq:(2 * t + 1) * tq],
                                ot[:, (2 * t + 1) * tq:(2 * t + 2) * tq]], axis=0)
        o_ref[:, t * LANES:(t + 1) * LANES] = pair.T.astype(o_ref.dtype)


def _gqa_attention(qk, vt, tq=256, tk=256, nbuf=2):
    S = qk.shape[0]
    qw = 4 * HEAD_DIM
    k_blk0 = GQA_Q_HEADS * HEAD_DIM // LANES
    tk = min(tk, S)
    return pl.pallas_call(
        functools.partial(_gqa_kernel, tk=tk),
        grid=(GQA_KV_HEADS, S // tq),
        in_specs=[pl.BlockSpec((tq, qw), lambda h, i: (i, h)),
                  pl.BlockSpec((S, LANES), lambda h, i: (0, k_blk0 + h)),
                  pl.BlockSpec((VT_ROWS, S), lambda h, i: (h, 0))],
        out_specs=pl.BlockSpec((tq, qw), lambda h, i: (i, h)),
        out_shape=jax.ShapeDtypeStruct((S, D_MODEL), F32),
        scratch_shapes=[pltpu.VMEM((4 * tq, LANES), BF16),
                        pltpu.VMEM((1, 4 * tq), F32),
                        pltpu.VMEM((VT_ROWS, 4 * tq), F32)]
        + [pltpu.VMEM((tk, 4 * tq), BF16)] * nbuf,
        compiler_params=_cparams("parallel", "parallel"),
        name="gqa_attention",
    )(qk, qk, vt)


def _proj_c_kernel(x_ref, wq_ref, wk_ref, wv_ref, cos_ref, sin_ref, o_ref, xp_ref, slab_ref, *, dil):
    tm, d = x_ref.shape[0], o_ref.shape[-1]
    n = tm // dil
    nslab = d // LANES
    cb_w = 2 * LANES
    ncb = d // cb_w

    if dil == 1:
        xp_ref[...] = x_ref[...].astype(BF16)
    else:
        for s in range(nslab):
            slab_ref[s] = x_ref[:, s * LANES:(s + 1) * LANES]
        for c in range(dil):
            for s in range(nslab):
                xp_ref[c * n:(c + 1) * n, s * LANES:(s + 1) * LANES] = (
                    slab_ref[s, pl.ds(c, n, stride=dil), :].astype(BF16))

    w_refs = (wq_ref, wk_ref, wv_ref)
    cos_k, sin_k = cos_ref[...], sin_ref[...]
    cos_q, sin_q = cos_k * (ATTN_SCALE * LOG2_E), sin_k * (ATTN_SCALE * LOG2_E)

    def matmul(kind, cb):
        return _dot(xp_ref[...], w_refs[kind][:, cb * cb_w:(cb + 1) * cb_w])

    def emit(kind, c0, vals):
        o_ref[kind, 0, :, :, c0:c0 + vals.shape[1]] = vals.astype(BF16).reshape(dil, n, vals.shape[1])

    chunks = [(kind, cb) for kind in range(3) for cb in range(ncb)]
    r_next = matmul(*chunks[0])
    for idx, (kind, cb) in enumerate(chunks):
        r = r_next
        if idx + 1 < len(chunks):
            r_next = matmul(*chunks[idx + 1])
        if kind == 2:
            emit(kind, cb * cb_w, r)
            continue
        cos_t, sin_t = (cos_q, sin_q) if kind == 0 else (cos_k, sin_k)
        for s in range(cb_w // LANES):
            rt = r[:, s * LANES:(s + 1) * LANES]
            emit(kind, cb * cb_w + s * LANES, rt * cos_t + _rotate_partner(rt) * sin_t)


def _regroup_rows(tab, dil):
    S, w = tab.shape
    return tab.reshape(S // DIL_CHUNK, DIL_CHUNK // dil, dil, w).transpose(0, 2, 1, 3).reshape(S, w)


def _proj_c(h, w_qk, w_v, cos_np, sin_np, group, dil):
    S = h.shape[0]
    tm = DIL_CHUNK
    n = tm // dil
    ngroups = len(DIL_DILATIONS)
    cos_t = jnp.asarray(_regroup_rows(cos_np, dil), F32)
    sin_t = jnp.asarray(_regroup_rows(sin_np, dil), F32)
    return pl.pallas_call(
        functools.partial(_proj_c_kernel, dil=dil),
        grid=(S // tm,),
        in_specs=[pl.BlockSpec((tm, D_MODEL), lambda i: (i, 0)),
                  pl.BlockSpec((D_MODEL, D_MODEL), lambda i: (0, group)),
                  pl.BlockSpec((D_MODEL, D_MODEL), lambda i: (0, ngroups + group)),
                  pl.BlockSpec((D_MODEL, D_MODEL), lambda i: (0, group)),
                  pl.BlockSpec((tm, LANES), lambda i: (i, 0)),
                  pl.BlockSpec((tm, LANES), lambda i: (i, 0))],
        out_specs=pl.BlockSpec((3, 1, dil, n, D_MODEL), lambda i: (0, i, 0, 0, 0)),
        out_shape=jax.ShapeDtypeStruct((3, S // tm, dil, n, D_MODEL), BF16),
        scratch_shapes=[pltpu.VMEM((tm, D_MODEL), BF16),
                        pltpu.VMEM((D_MODEL // LANES, tm, LANES), F32)],
        compiler_params=_cparams("parallel"),
        name=f"proj_c_dil{dil}",
    )(h, w_qk, w_qk, w_v, cos_t, sin_t)


def _dil_kernel(*refs, nchunks):
    ng = len(DIL_DILATIONS)
    o_ref, osc_ref, lsc_ref = refs[7 * ng:]
    i = pl.program_id(0)
    lo = _lane_lo()
    lo_qk = _lane_qk()

    groups, blocks = [], []
    for g, dil in enumerate(DIL_DILATIONS):
        n = DIL_CHUNK // dil
        bq = min(n, 2 * DIL_SIDE)
        nk = bq + 2 * DIL_SIDE
        qi = lax.broadcasted_iota(jnp.int32, (2 * bq, nk), 0) % bq
        kj = lax.broadcasted_iota(jnp.int32, (2 * bq, nk), 1)
        band = jnp.where(jnp.abs(kj - DIL_SIDE - qi) <= DIL_SIDE, 0.0, MASK_VALUE).astype(F32)
        groups.append(dict(dil=dil, n=n, bq=bq, nk=nk, nsub=n // bq, kj=kj, band=band,
                           kpad=-(n + 2 * DIL_SIDE) % LANES, refs=refs[7 * g:7 * g + 7]))
        blocks += [(g, c, b) for c in range(dil) for b in range(n // bq)]

    class_kv = {}

    def keys_values(g, c):
        if (g, c) not in class_kv:
            G = groups[g]
            _, kc_ref, kp_ref, kn_ref, vc_ref, vp_ref, vn_ref = G["refs"]
            kparts = [kp_ref[0, 0, c], kc_ref[0, 0, c], kn_ref[0, 0, c]]
            if G["kpad"]:
                kparts.append(jnp.zeros((G["kpad"], LANES), BF16))
            kcat_t = jnp.concatenate(kparts, axis=0).astype(F32).T.astype(BF16)
            vcat = jnp.concatenate([vp_ref[0, 0, c], vc_ref[0, 0, c], vn_ref[0, 0, c]], axis=0)
            class_kv[(g, c)] = (kcat_t, vcat)
        return class_kv[(g, c)]

    def scores(blk):
        g, c, b = blk
        G = groups[g]
        bq, nk, kj = G["bq"], G["nk"], G["kj"]
        p0 = b * bq
        bias = G["band"]
        if b == 0:
            bias = jnp.where(kj >= jnp.where(i == 0, DIL_SIDE, 0), bias, MASK_VALUE)
        if b == G["nsub"] - 1:
            bias = jnp.where(kj < jnp.where(i == nchunks - 1, bq + DIL_SIDE, nk), bias, MASK_VALUE)
        qf = G["refs"][0][0, 0, c, p0:p0 + bq, :].astype(F32)
        qs = jnp.concatenate([jnp.where(lo_qk, qf, 0.0), jnp.where(lo_qk, 0.0, qf)], axis=0)
        return _dot(qs.astype(BF16), keys_values(g, c)[0][:, p0:p0 + nk]) + bias

    def finish(blk, s):
        g, c, b = blk
        G = groups[g]
        bq, nk, dil = G["bq"], G["nk"], G["dil"]
        p0 = b * bq
        m = jnp.max(s, axis=1, keepdims=True)
        p = jnp.exp2(s - m)
        l = jnp.sum(p, axis=1, keepdims=True)
        o2 = _dot(p.astype(BF16), keys_values(g, c)[1][p0:p0 + nk]) / l
        lse2 = m + jnp.log2(l)
        o_pair = jnp.where(lo, o2[:bq], o2[bq:])
        l_pair = jnp.where(lo, lse2[:bq], lse2[bq:])
        rows = pl.ds(p0, bq) if dil == 1 else pl.ds(p0 * dil + c, bq, stride=dil)
        osc_ref[g, rows, :] = o_pair
        lsc_ref[g, rows, :] = l_pair

    s_next = scores(blocks[0])
    for idx, blk in enumerate(blocks):
        s_cur = s_next
        if idx + 1 < len(blocks):
            s_next = scores(blocks[idx + 1])
        finish(blk, s_cur)

    ls = [lsc_ref[g] for g in range(ng)]
    mx = functools.reduce(jnp.maximum, ls)
    ws = [jnp.exp2(l - mx) for l in ls]
    num = sum(w * osc_ref[g] for g, w in enumerate(ws))
    o_ref[...] = (num / sum(ws)).astype(o_ref.dtype)


def _dilated_attention(qkvs):
    nchunks = qkvs[0].shape[1]
    S = nchunks * DIL_CHUNK
    in_specs, args = [], []
    for dil, a in zip(DIL_DILATIONS, qkvs):
        n = DIL_CHUNK // dil
        nblk = n // DIL_SIDE
        full = (1, 1, dil, n, LANES)
        halo = (1, 1, dil, DIL_SIDE, LANES)

        def cur(t):
            return lambda i, m: (t, i, 0, 0, m)

        def prev(t, nblk=nblk):
            return lambda i, m: (t, jnp.maximum(i - 1, 0), 0, nblk - 1, m)

        def nxt(t):
            return lambda i, m: (t, jnp.minimum(i + 1, nchunks - 1), 0, 0, m)

        in_specs += [pl.BlockSpec(full, cur(0)),
                     pl.BlockSpec(full, cur(1)), pl.BlockSpec(halo, prev(1)), pl.BlockSpec(halo, nxt(1)),
                     pl.BlockSpec(full, cur(2)), pl.BlockSpec(halo, prev(2)), pl.BlockSpec(halo, nxt(2))]
        args += [a] * 7
    ng = len(DIL_DILATIONS)
    return pl.pallas_call(
        functools.partial(_dil_kernel, nchunks=nchunks),
        grid=(nchunks, D_MODEL // LANES),
        in_specs=in_specs,
        out_specs=pl.BlockSpec((DIL_CHUNK, LANES), lambda i, m: (i, m)),
        out_shape=jax.ShapeDtypeStruct((S, D_MODEL), F32),
        scratch_shapes=[pltpu.VMEM((ng, DIL_CHUNK, LANES), F32),
                        pltpu.VMEM((ng, DIL_CHUNK, LANES), F32)],
        compiler_params=_cparams("parallel", "parallel"),
        name="dilated_attention",
    )(*args)


def _epilogue_kernel(br_ref, h_ref, wt_ref, mk_ref, mv_ref, wo_ref, g_ref, b_ref, o_ref, *, nsplit):
    lo = _lane_lo()
    tr = h_ref.shape[0]
    rs = tr // nsplit

    def pre_norm(rows):
        h = h_ref[rows, :]
        hb = h.astype(BF16)
        gate_b = _dot(hb, wt_ref[:, :D_MODEL])
        yb = (br_ref[rows, :].astype(F32) * (gate_b * jax.nn.sigmoid(gate_b))).astype(BF16)
        y = _dot(yb, wo_ref[:D_MODEL, :])
        tail_m = _dot(hb, wt_ref[:, D_MODEL:])
        yms = []
        for t in range(MEM_WIDTH // LANES):
            cols = slice(t * LANES, (t + 1) * LANES)
            qf = tail_m[:, MEM_WIDTH + t * LANES:MEM_WIDTH + (t + 1) * LANES] * ATTN_SCALE
            kt, vt = mk_ref[:, cols], mv_ref[:, cols]
            outs = []
            for qh in (jnp.where(lo, qf, 0.0), jnp.where(lo, 0.0, qf)):
                s = _dot_nt(qh.astype(BF16), kt)
                m = jnp.max(s, axis=1, keepdims=True)
                p = jnp.exp(s - m)
                l = jnp.sum(p, axis=1, keepdims=True)
                outs.append(_dot(p.astype(BF16), vt) / l)
            mem_out = jnp.where(lo, outs[0], outs[1])
            gm = tail_m[:, cols]
            yms.append((mem_out * (gm * jax.nn.sigmoid(gm))).astype(BF16))
        y = y + _dot(jnp.concatenate(yms, axis=1), wo_ref[D_MODEL:, :])
        return DEEPNORM_ALPHA * h + y

    z_next = pre_norm(pl.ds(0, rs))
    for r in range(nsplit):
        z = z_next
        if r + 1 < nsplit:
            z_next = pre_norm(pl.ds((r + 1) * rs, rs))
        o_ref[pl.ds(r * rs, rs), :] = _layer_norm_rows(z, g_ref[...], b_ref[...])


def _epilogue(branch, h, w_tail, mk, mv, w_out, g, b, tail_block=0, tr=1024, nsplit=2):
    S = h.shape[0]
    row = lambda i: (i, 0)
    fixed = lambda i: (0, 0)
    return pl.pallas_call(
        functools.partial(_epilogue_kernel, nsplit=nsplit),
        grid=(S // tr,),
        in_specs=[pl.BlockSpec((tr, D_MODEL), row),
                  pl.BlockSpec((tr, D_MODEL), row),
                  pl.BlockSpec((D_MODEL, TAIL), lambda i: (0, tail_block)),
                  pl.BlockSpec((N_MEM, MEM_WIDTH), fixed),
                  pl.BlockSpec((N_MEM, MEM_WIDTH), fixed),
                  pl.BlockSpec((INNER, D_MODEL), fixed),
                  pl.BlockSpec((1, D_MODEL), fixed),
                  pl.BlockSpec((1, D_MODEL), fixed)],
        out_specs=pl.BlockSpec((tr, D_MODEL), row),
        out_shape=jax.ShapeDtypeStruct((S, D_MODEL), F32),
        compiler_params=_cparams("parallel"),
        name="epilogue",
    )(branch, h, w_tail, mk, mv, w_out, g.reshape(1, -1), b.reshape(1, -1))


def _dup_kv_columns(w):
    d = w.shape[0]
    w4 = w.reshape(d, GQA_KV_HEADS, 1, HEAD_DIM)
    return jnp.broadcast_to(w4, (d, GQA_KV_HEADS, 2, HEAD_DIM)).reshape(d, 2 * GQA_KV_HEADS * HEAD_DIM)


def kernel(x, mem, ln_in_g, ln_in_b, w_mem_kv, w_in_a, w_in_b, q_norm_g, k_norm_g, w_in_c, w_out, ln_g, ln_b):
    B, S, D = x.shape
    assert B == 1 and D == D_MODEL and S % DIL_CHUNK == 0 and S % (FFT_N2 * SUBLANES) == 0

    mkv = _proj(mem[0], w_mem_kv.astype(BF16), name="proj_mem").astype(BF16)
    mk, mv = mkv[:, :MEM_WIDTH], mkv[:, MEM_WIDTH:]
    h = _ln_in(x[0], ln_in_g, ln_in_b)
    fft_tabs = None
    qw = GQA_Q_HEADS * HEAD_DIM
    kvw = GQA_KV_HEADS * HEAD_DIM

    for i in range(DEPTH):
        kind, j = i % 3, i // 3
        tail_block = 0
        if kind == 0:
            if fft_tabs is None:
                fft_tabs = _fft_tables(S)
            fc, gk, hm = fft_tabs
            w_tail = w_in_a[j].astype(BF16)
            branch = _fft_stage2(_fft_stage1(_proj_a(h, fc), gk), hm)
        elif kind == 1:
            w = w_in_b[j]
            half = HEAD_DIM // 4
            w_qk = jnp.concatenate([w[:, :qw], _dup_kv_columns(w[:, qw:qw + kvw])], axis=1)
            w_qkv = jnp.concatenate([_pair_rotary_layout(w_qk, half), w[:, qw + kvw:qw + 2 * kvw]],
                                    axis=1).astype(BF16)
            cos_t, sin_t = (jnp.asarray(_pair_rotary_layout(t, half), F32) for t in _rope_tables_axial(S))
            gq = _pair_rotary_layout(jnp.tile(q_norm_g[j], 2), half).reshape(1, LANES)
            gkn = _pair_rotary_layout(jnp.tile(k_norm_g[j], 2), half).reshape(1, LANES)
            qk, vt = _proj_b(h, w_qkv, cos_t, sin_t, gq, gkn)
            w_tail = w[:, qw + 2 * kvw:].astype(BF16)
            branch = _gqa_attention(qk, vt)
        else:
            half = HEAD_DIM // 2
            n_qk = 2 * len(DIL_DILATIONS) * D_MODEL
            w_qk = _prep_qk_weights(w_in_c[j], n_qk, half)
            w_tail = w_in_c[j][:, n_qk:].astype(BF16)
            tail_block = (w_tail.shape[1] - TAIL) // TAIL
            cos_t, sin_t = (_pair_rotary_layout(t, half) for t in _rope_tables_1d(S))
            qkvs = [_proj_c(h, w_qk, w_tail, cos_t, sin_t, g, dil) for g, dil in enumerate(DIL_DILATIONS)]
            branch = _dilated_attention(qkvs)
        h = _epilogue(branch, h, w_tail, mk, mv, w_out[i].astype(BF16), ln_g[i], ln_b[i], tail_block)
    return h[None]
```

```python
import functools

import jax
import jax.numpy as jnp
import numpy as np
from jax import lax
from jax.experimental import pallas as pl
from jax.experimental.pallas import tpu as pltpu

F32 = jnp.float32
BF16 = jnp.bfloat16

D_MODEL = 1024
DEPTH = 4
N_MEM = 256
GRID_W = 64
HEAD_DIM = 64
ROPE_THETA = 10000.0
LN_EPS = 1e-5
RMS_EPS = 1e-6
MASK_VALUE = -1e30
FNET_GROUPS = 4
FNET_GROUP_W = D_MODEL // FNET_GROUPS
GQA_Q_HEADS = 16
GQA_KV_HEADS = 4
DIL_DILATIONS = (1, 4, 16)
DIL_SIDE = 64
MEM_WIDTH = 256
INNER = D_MODEL + MEM_WIDTH
TAIL = INNER + MEM_WIDTH
DEEPNORM_ALPHA = (2.0 * DEPTH) ** 0.25
ATTN_SCALE = HEAD_DIM ** -0.5
LOG2_E = float(np.log2(np.e))

BF16_SUBLANES = 16
VT_ROWS = HEAD_DIM + BF16_SUBLANES
LANES = 128
SUBLANES = 8
FFT_N2 = 256
DIL_CHUNK = 1024
VMEM_LIMIT = 48 << 20


def _cparams(*sem):
    return pltpu.CompilerParams(dimension_semantics=sem, vmem_limit_bytes=VMEM_LIMIT)


def _dot(a, b):
    return jnp.dot(a, b, preferred_element_type=F32)


def _dot_nt(a, b):
    return lax.dot_general(a, b, (((1,), (1,)), ((), ())), preferred_element_type=F32)


def _lane_lo(shape=(1, LANES)):
    return lax.broadcasted_iota(jnp.int32, shape, len(shape) - 1) % LANES < HEAD_DIM


def _layer_norm_rows(z, g, b):
    mu = jnp.mean(z, axis=-1, keepdims=True)
    zc = z - mu
    var = jnp.mean(zc * zc, axis=-1, keepdims=True)
    return zc * lax.rsqrt(var + LN_EPS) * g + b


def _ln_kernel(x_ref, g_ref, b_ref, o_ref):
    o_ref[...] = _layer_norm_rows(x_ref[...], g_ref[...], b_ref[...])


def _ln_in(x, g, b, tr=512):
    S = x.shape[0]
    return pl.pallas_call(
        _ln_kernel,
        grid=(S // tr,),
        in_specs=[pl.BlockSpec((tr, D_MODEL), lambda i: (i, 0)),
                  pl.BlockSpec((1, D_MODEL), lambda i: (0, 0)),
                  pl.BlockSpec((1, D_MODEL), lambda i: (0, 0))],
        out_specs=pl.BlockSpec((tr, D_MODEL), lambda i: (i, 0)),
        out_shape=jax.ShapeDtypeStruct((S, D_MODEL), F32),
        compiler_params=_cparams("parallel"),
        name="ln_in",
    )(x, g.reshape(1, -1), b.reshape(1, -1))


def _proj_kernel(x_ref, w_ref, o_ref):
    o_ref[...] = _dot(x_ref[...].astype(BF16), w_ref[...])


def _proj(x, w, tr=512, name="proj"):
    S, K = x.shape
    N = w.shape[1]
    tr = min(tr, S)
    return pl.pallas_call(
        _proj_kernel,
        grid=(S // tr,),
        in_specs=[pl.BlockSpec((tr, K), lambda i: (i, 0)),
                  pl.BlockSpec((K, N), lambda i: (0, 0))],
        out_specs=pl.BlockSpec((tr, N), lambda i: (i, 0)),
        out_shape=jax.ShapeDtypeStruct((S, N), F32),
        compiler_params=_cparams("parallel"),
        name=name,
    )(x, w)


def _proj_a_kernel(x_ref, fc_ref, z_ref):
    xb = x_ref[...].astype(BF16)
    for g in range(FNET_GROUPS):
        cols = slice(g * FNET_GROUP_W, (g + 1) * FNET_GROUP_W)
        zg = _dot(xb[:, cols], fc_ref[...])
        z_ref[0, :, cols] = zg[:, :FNET_GROUP_W]
        z_ref[1, :, cols] = zg[:, FNET_GROUP_W:]


def _ln_proj_a_kernel(x_ref, g_ref, b_ref, fc_ref, h_ref, z_ref):
    h = _layer_norm_rows(x_ref[...], g_ref[...], b_ref[...])
    h_ref[...] = h
    hb = h.astype(BF16)
    for g in range(FNET_GROUPS):
        cols = slice(g * FNET_GROUP_W, (g + 1) * FNET_GROUP_W)
        zg = _dot(hb[:, cols], fc_ref[...])
        z_ref[0, :, cols] = zg[:, :FNET_GROUP_W]
        z_ref[1, :, cols] = zg[:, FNET_GROUP_W:]


def _ln_proj_a(x, g, b, fc, tr=512):
    S = x.shape[0]
    return pl.pallas_call(
        _ln_proj_a_kernel,
        grid=(S // tr,),
        in_specs=[pl.BlockSpec((tr, D_MODEL), lambda i: (i, 0)),
                  pl.BlockSpec((1, D_MODEL), lambda i: (0, 0)),
                  pl.BlockSpec((1, D_MODEL), lambda i: (0, 0)),
                  pl.BlockSpec((FNET_GROUP_W, 2 * FNET_GROUP_W), lambda i: (0, 0))],
        out_specs=[pl.BlockSpec((tr, D_MODEL), lambda i: (i, 0)),
                   pl.BlockSpec((2, tr, D_MODEL), lambda i: (0, i, 0))],
        out_shape=[jax.ShapeDtypeStruct((S, D_MODEL), F32),
                   jax.ShapeDtypeStruct((2, S, D_MODEL), F32)],
        compiler_params=_cparams("parallel"),
        name="ln_proj_a",
    )(x, g.reshape(1, -1), b.reshape(1, -1), fc)


def _proj_a(h, fc, tr=512):
    S = h.shape[0]
    return pl.pallas_call(
        _proj_a_kernel,
        grid=(S // tr,),
        in_specs=[pl.BlockSpec((tr, D_MODEL), lambda i: (i, 0)),
                  pl.BlockSpec((FNET_GROUP_W, 2 * FNET_GROUP_W), lambda i: (0, 0))],
        out_specs=pl.BlockSpec((2, tr, D_MODEL), lambda i: (0, i, 0)),
        out_shape=jax.ShapeDtypeStruct((2, S, D_MODEL), F32),
        compiler_params=_cparams("parallel"),
        name="proj_a",
    )(h, fc)


def _fft1_kernel(z_ref, g_ref, t_ref):
    two, n1, _, sub, tc = z_ref.shape
    x = z_ref[...].reshape(two * n1 * sub, tc).astype(BF16)
    t_ref[...] = _dot(g_ref[...], x).reshape(t_ref.shape)


def _fft_stage1(z, gk, tc=1024):
    _, S, D = z.shape
    n1 = S // FFT_N2
    nu = FFT_N2 // SUBLANES
    z5 = z.reshape(2, n1, nu, SUBLANES, D)
    rows = 2 * n1 * SUBLANES
    t5 = pl.pallas_call(
        _fft1_kernel,
        grid=(nu, D // tc),
        in_specs=[pl.BlockSpec((2, n1, 1, SUBLANES, tc), lambda u, c: (0, 0, u, 0, c)),
                  pl.BlockSpec((rows, rows), lambda u, c: (0, 0))],
        out_specs=pl.BlockSpec((n1, 2, 1, SUBLANES, tc), lambda u, c: (0, 0, u, 0, c)),
        out_shape=jax.ShapeDtypeStruct((n1, 2, nu, SUBLANES, D), F32),
        compiler_params=_cparams("parallel", "parallel"),
        name="fft_stage1",
    )(z5, gk)
    return t5.reshape(n1, 2, FFT_N2, D)


def _fft2_kernel(t_ref, h_ref, o_ref, slab_ref):
    nj, _, n2, d = t_ref.shape
    nslab = d // LANES
    for j in range(nj):
        tj = t_ref[j].reshape(2 * n2, d).astype(BF16)
        r = _dot(h_ref[j], tj)
        for s in range(nslab):
            slab_ref[s, pl.ds(j, n2, stride=nj), :] = r[:, s * LANES:(s + 1) * LANES]
    for s in range(nslab):
        o_ref[:, :, s * LANES:(s + 1) * LANES] = slab_ref[s].reshape(n2, nj, LANES)


def _fft_stage2(t, hmat, tc=512):
    n1, _, n2, D = t.shape
    nj = SUBLANES
    y3 = pl.pallas_call(
        _fft2_kernel,
        grid=(n1 // nj, D // tc),
        in_specs=[pl.BlockSpec((nj, 2, n2, tc), lambda a, c: (a, 0, 0, c)),
                  pl.BlockSpec((nj, n2, 2 * n2), lambda a, c: (a, 0, 0))],
        out_specs=pl.BlockSpec((n2, nj, tc), lambda a, c: (0, a, c)),
        out_shape=jax.ShapeDtypeStruct((n2, n1, D), F32),
        scratch_shapes=[pltpu.VMEM((tc // LANES, n2 * nj, LANES), F32)],
        compiler_params=_cparams("parallel", "parallel"),
        name="fft_stage2",
    )(t, hmat)
    return y3.reshape(n2 * n1, D)


def _fft_tables(S):
    n1, n2 = S // FFT_N2, FFT_N2
    c = np.arange(FNET_GROUP_W)
    ang = (2.0 * np.pi / FNET_GROUP_W) * ((c[:, None] * c[None, :]) % FNET_GROUP_W)
    scale = 1.0 / np.sqrt(float(S) * FNET_GROUP_W)
    fc = np.concatenate([np.cos(ang), -np.sin(ang)], axis=1) * scale
    k1 = np.arange(n1)
    th = (2.0 * np.pi / n1) * ((k1[:, None] * k1[None, :]) % n1)
    cs, sn = np.cos(th), np.sin(th)
    g = np.stack([np.stack([cs, sn], axis=1), np.stack([-sn, cs], axis=1)], axis=1)
    gk = np.kron(g.reshape(2 * n1, 2 * n1), np.eye(SUBLANES))
    k2 = np.arange(n2)
    kk = k1[:, None, None] + n1 * k2[None, :, None]
    ph = (2.0 * np.pi / S) * ((k2[None, None, :] * kk) % S)
    hm = np.concatenate([np.cos(ph), np.sin(ph)], axis=2)
    return tuple(jnp.asarray(t, F32).astype(BF16) for t in (fc, gk, hm))


def _rope_angles(pos, dim):
    inv_freq = ROPE_THETA ** (-(np.arange(0, dim, 2, dtype=np.float64) / dim))
    return pos.astype(np.float64)[:, None] * inv_freq[None, :]


def _rope_tables_axial(S):
    t = np.arange(S)
    ar = _rope_angles(t // GRID_W, HEAD_DIM // 2)
    ac = _rope_angles(t % GRID_W, HEAD_DIM // 2)
    cos = np.concatenate([np.cos(ar), np.cos(ar), np.cos(ac), np.cos(ac)], axis=1)
    sin = np.concatenate([-np.sin(ar), np.sin(ar), -np.sin(ac), np.sin(ac)], axis=1)
    return np.tile(cos, (1, 2)), np.tile(sin, (1, 2))


def _rope_tables_1d(S):
    a = _rope_angles(np.arange(S), HEAD_DIM)
    cos = np.concatenate([np.cos(a), np.cos(a)], axis=1)
    sin = np.concatenate([-np.sin(a), np.sin(a)], axis=1)
    return np.tile(cos, (1, 2)), np.tile(sin, (1, 2))


def _pair_rotary_layout(a, half):
    lead = a.shape[:-1]
    a6 = a.reshape(*lead, a.shape[-1] // LANES, 2, HEAD_DIM // (2 * half), 2, half)
    xp = jnp if isinstance(a, jax.Array) else np
    return xp.moveaxis(a6, -2, -4).reshape(*lead, a.shape[-1])


def _pair_rotary_lanes(x, half):
    nblk = LANES // half
    groups = HEAD_DIM // (2 * half)
    lane_blk = lax.broadcasted_iota(jnp.int32, (1, LANES), 1) // half
    moves = {}
    for j in range(nblk):
        fs, rem = divmod(j, nblk // 2)
        head, group = divmod(rem, groups)
        s = head * (HEAD_DIM // half) + 2 * group + fs
        moves.setdefault(((j - s) * half) % LANES, []).append(j)
    out = x
    for shift, dst in moves.items():
        if shift:
            mask = functools.reduce(jnp.logical_or, [lane_blk == j for j in dst])
            out = jnp.where(mask, pltpu.roll(x, shift, 1), out)
    return out


def _prep_qk_kernel(w_ref, o_ref, *, half):
    for t in range(o_ref.shape[1] // LANES):
        cols = slice(t * LANES, (t + 1) * LANES)
        o_ref[:, cols] = _pair_rotary_lanes(w_ref[:, cols], half).astype(BF16)


def _prep_qk_weights(w, ncols, half, tc=1024):
    d = w.shape[0]
    return pl.pallas_call(
        functools.partial(_prep_qk_kernel, half=half),
        grid=(ncols // tc,),
        in_specs=[pl.BlockSpec((d, tc), lambda c: (0, c))],
        out_specs=pl.BlockSpec((d, tc), lambda c: (0, c)),
        out_shape=jax.ShapeDtypeStruct((d, ncols), BF16),
        compiler_params=_cparams("parallel"),
        name="prep_qk_weights",
    )(w)


def _lane_qk():
    return lax.broadcasted_iota(jnp.int32, (1, LANES), 1) % HEAD_DIM < HEAD_DIM // 2


def _rotate_partner(x):
    return pltpu.roll(x, HEAD_DIM, 1)


def _proj_b_kernel(x_ref, w_ref, cos_ref, sin_ref, gq_ref, gk_ref, o_ref, vt_ref):
    xb = x_ref[...].astype(BF16)
    cos_t, sin_t = cos_ref[...], sin_ref[...]
    lo = _lane_qk()
    nb_w = 4 * LANES
    vt = _dot(xb, w_ref[:, 3 * nb_w:]).T.astype(BF16)
    ones = jnp.ones((VT_ROWS - HEAD_DIM, vt.shape[1]), BF16)
    for hd in range(GQA_KV_HEADS):
        vt_ref[hd * VT_ROWS:hd * VT_ROWS + HEAD_DIM, :] = vt[hd * HEAD_DIM:(hd + 1) * HEAD_DIM]
        vt_ref[hd * VT_ROWS + HEAD_DIM:(hd + 1) * VT_ROWS, :] = ones
    for nb in range(3):
        r = _dot(xb, w_ref[:, nb * nb_w:(nb + 1) * nb_w])
        gain = gq_ref[...] if nb < 2 else gk_ref[...]
        scale = ATTN_SCALE * LOG2_E if nb < 2 else 1.0
        for t in range(4):
            rt = r[:, t * LANES:(t + 1) * LANES]
            r2 = rt * rt
            tot = jnp.sum(r2, axis=1, keepdims=True)
            low = jnp.sum(jnp.where(lo, r2, 0.0), axis=1, keepdims=True)
            ss = jnp.where(lo, low, tot - low)
            xn = rt * lax.rsqrt(ss * (1.0 / HEAD_DIM) + RMS_EPS) * gain
            out = (xn * cos_t + _rotate_partner(xn) * sin_t) * scale
            c0 = nb * nb_w + t * LANES
            o_ref[:, c0:c0 + LANES] = out.astype(BF16)


def _proj_b(h, w_qkv, cos_t, sin_t, gq, gk, tr=512):
    S = h.shape[0]
    n_in = w_qkv.shape[1]
    n_out = n_in - GQA_KV_HEADS * HEAD_DIM
    vw = GQA_KV_HEADS * VT_ROWS
    return pl.pallas_call(
        _proj_b_kernel,
        grid=(S // tr,),
        in_specs=[pl.BlockSpec((tr, D_MODEL), lambda i: (i, 0)),
                  pl.BlockSpec((D_MODEL, n_in), lambda i: (0, 0)),
                  pl.BlockSpec((tr, LANES), lambda i: (i, 0)),
                  pl.BlockSpec((tr, LANES), lambda i: (i, 0)),
                  pl.BlockSpec((1, LANES), lambda i: (0, 0)),
                  pl.BlockSpec((1, LANES), lambda i: (0, 0))],
        out_specs=[pl.BlockSpec((tr, n_out), lambda i: (i, 0)),
                   pl.BlockSpec((vw, tr), lambda i: (0, i))],
        out_shape=[jax.ShapeDtypeStruct((S, n_out), BF16),
                   jax.ShapeDtypeStruct((vw, S), BF16)],
        compiler_params=_cparams("parallel"),
        name="proj_b",
    )(h, w_qkv, cos_t, sin_t, gq, gk)


def _gqa_kernel(q_ref, k_ref, vt_ref, o_ref, qs_ref, m_ref, acc_ref, *p_refs, tk):
    tq = q_ref.shape[0]
    S = k_ref.shape[0]
    nq = 4 * tq
    nkv = S // tk
    nbuf = len(p_refs)
    lo = _lane_qk()
    for t in range(2):
        qt = q_ref[:, t * LANES:(t + 1) * LANES].astype(F32)
        qs_ref[(2 * t) * tq:(2 * t + 1) * tq, :] = jnp.where(lo, qt, 0.0).astype(BF16)
        qs_ref[(2 * t + 1) * tq:(2 * t + 2) * tq, :] = jnp.where(lo, 0.0, qt).astype(BF16)

    def scores(j):
        k0 = pl.multiple_of(j * tk, tk)
        return _dot_nt(k_ref[pl.ds(k0, tk), :], qs_ref[...])

    def values(j, p):
        return _dot(vt_ref[:, pl.ds(pl.multiple_of(j * tk, tk), tk)], p)

    def colmax(s):
        return jnp.max(jnp.max(s.reshape(tk // SUBLANES, SUBLANES, nq), axis=0), axis=0, keepdims=True)

    m0 = jnp.broadcast_to(colmax(scores(0)), (tk, nq))

    def probs(u):
        p_refs[u % nbuf][...] = jnp.exp2(scores(u) - m0).astype(BF16)

    probs(0)
    pv = None
    for u in range(nkv):
        if u + 1 < nkv:
            probs(u + 1)
        d = values(u, p_refs[u % nbuf][...])
        pv = d if pv is None else pv + d
    safe = jnp.min(jnp.where(jnp.isfinite(pv), 1.0, 0.0)) > 0.5

    @pl.when(safe)
    def _():
        acc_ref[...] = pv

    @pl.when(jnp.logical_not(safe))
    def _():
        m_ref[...] = jnp.full(m_ref.shape, -jnp.inf, F32)
        acc_ref[...] = jnp.zeros(acc_ref.shape, F32)

        def exact_tile(u, c):
            s = scores(u)
            m_prev = m_ref[...]
            m_cur = jnp.maximum(m_prev, colmax(s))
            d = values(u, jnp.exp2(s - m_cur).astype(BF16))
            acc_ref[...] = jnp.exp2(m_prev - m_cur) * acc_ref[...] + d
            m_ref[...] = m_cur
            return c

        lax.fori_loop(0, nkv, exact_tile, 0)

    acc = acc_ref[...]
    ot = acc[:HEAD_DIM] / acc[HEAD_DIM:HEAD_DIM + 1]
    for t in range(2):
        pair = jnp.concatenate([ot[:, (2 * t) * tq:(2 * t + 1) * tq],
                                ot[:, (2 * t + 1) * tq:(2 * t + 2) * tq]], axis=0)
        o_ref[:, t * LANES:(t + 1) * LANES] = pair.T.astype(o_ref.dtype)


def _gqa_attention(qk, vt, tq=256, tk=256, nbuf=2):
    S = qk.shape[0]
    qw = 4 * HEAD_DIM
    k_blk0 = GQA_Q_HEADS * HEAD_DIM // LANES
    tk = min(tk, S)
    return pl.pallas_call(
        functools.partial(_gqa_kernel, tk=tk),
        grid=(GQA_KV_HEADS, S // tq),
        in_specs=[pl.BlockSpec((tq, qw), lambda h, i: (i, h)),
                  pl.BlockSpec((S, LANES), lambda h, i: (0, k_blk0 + h)),
                  pl.BlockSpec((VT_ROWS, S), lambda h, i: (h, 0))],
        out_specs=pl.BlockSpec((tq, qw), lambda h, i: (i, h)),
        out_shape=jax.ShapeDtypeStruct((S, D_MODEL), F32),
        scratch_shapes=[pltpu.VMEM((4 * tq, LANES), BF16),
                        pltpu.VMEM((1, 4 * tq), F32),
                        pltpu.VMEM((VT_ROWS, 4 * tq), F32)]
        + [pltpu.VMEM((tk, 4 * tq), BF16)] * nbuf,
        compiler_params=_cparams("parallel", "parallel"),
        name="gqa_attention",
    )(qk, qk, vt)


def _proj_c_kernel(x_ref, wq_ref, wk_ref, wv_ref, cos_ref, sin_ref, o_ref, xp_ref, slab_ref, *, dil):
    tm, d = x_ref.shape[0], o_ref.shape[-1]
    n = tm // dil
    nslab = d // LANES
    cb_w = 2 * LANES
    ncb = d // cb_w

    if dil == 1:
        xp_ref[...] = x_ref[...].astype(BF16)
    else:
        for s in range(nslab):
            slab_ref[s] = x_ref[:, s * LANES:(s + 1) * LANES]
        for c in range(dil):
            for s in range(nslab):
                xp_ref[c * n:(c + 1) * n, s * LANES:(s + 1) * LANES] = (
                    slab_ref[s, pl.ds(c, n, stride=dil), :].astype(BF16))

    w_refs = (wq_ref, wk_ref, wv_ref)
    cos_k, sin_k = cos_ref[...], sin_ref[...]
    cos_q, sin_q = cos_k * (ATTN_SCALE * LOG2_E), sin_k * (ATTN_SCALE * LOG2_E)

    def matmul(kind, cb):
        return _dot(xp_ref[...], w_refs[kind][:, cb * cb_w:(cb + 1) * cb_w])

    def emit(kind, c0, vals):
        o_ref[kind, 0, :, :, c0:c0 + vals.shape[1]] = vals.astype(BF16).reshape(dil, n, vals.shape[1])

    chunks = [(kind, cb) for kind in range(3) for cb in range(ncb)]
    r_next = matmul(*chunks[0])
    for idx, (kind, cb) in enumerate(chunks):
        r = r_next
        if idx + 1 < len(chunks):
            r_next = matmul(*chunks[idx + 1])
        if kind == 2:
            emit(kind, cb * cb_w, r)
            continue
        cos_t, sin_t = (cos_q, sin_q) if kind == 0 else (cos_k, sin_k)
        for s in range(cb_w // LANES):
            rt = r[:, s * LANES:(s + 1) * LANES]
            emit(kind, cb * cb_w + s * LANES, rt * cos_t + _rotate_partner(rt) * sin_t)


def _regroup_rows(tab, dil):
    S, w = tab.shape
    return tab.reshape(S // DIL_CHUNK, DIL_CHUNK // dil, dil, w).transpose(0, 2, 1, 3).reshape(S, w)


def _proj_c(h, w_qk, w_v, cos_np, sin_np, group, dil):
    S = h.shape[0]
    tm = DIL_CHUNK
    n = tm // dil
    ngroups = len(DIL_DILATIONS)
    cos_t = jnp.asarray(_regroup_rows(cos_np, dil), F32)
    sin_t = jnp.asarray(_regroup_rows(sin_np, dil), F32)
    return pl.pallas_call(
        functools.partial(_proj_c_kernel, dil=dil),
        grid=(S // tm,),
        in_specs=[pl.BlockSpec((tm, D_MODEL), lambda i: (i, 0)),
                  pl.BlockSpec((D_MODEL, D_MODEL), lambda i: (0, group)),
                  pl.BlockSpec((D_MODEL, D_MODEL), lambda i: (0, ngroups + group)),
                  pl.BlockSpec((D_MODEL, D_MODEL), lambda i: (0, group)),
                  pl.BlockSpec((tm, LANES), lambda i: (i, 0)),
                  pl.BlockSpec((tm, LANES), lambda i: (i, 0))],
        out_specs=pl.BlockSpec((3, 1, dil, n, D_MODEL), lambda i: (0, i, 0, 0, 0)),
        out_shape=jax.ShapeDtypeStruct((3, S // tm, dil, n, D_MODEL), BF16),
        scratch_shapes=[pltpu.VMEM((tm, D_MODEL), BF16),
                        pltpu.VMEM((D_MODEL // LANES, tm, LANES), F32)],
        compiler_params=_cparams("parallel"),
        name=f"proj_c_dil{dil}",
    )(h, w_qk, w_qk, w_v, cos_t, sin_t)


def _dil_kernel(*refs, nchunks):
    ng = len(DIL_DILATIONS)
    o_ref, osc_ref, lsc_ref = refs[7 * ng:]
    i = pl.program_id(0)
    lo = _lane_lo()
    lo_qk = _lane_qk()

    groups, blocks = [], []
    for g, dil in enumerate(DIL_DILATIONS):
        n = DIL_CHUNK // dil
        bq = min(n, 2 * DIL_SIDE)
        nk = bq + 2 * DIL_SIDE
        qi = lax.broadcasted_iota(jnp.int32, (2 * bq, nk), 0) % bq
        kj = lax.broadcasted_iota(jnp.int32, (2 * bq, nk), 1)
        band = jnp.where(jnp.abs(kj - DIL_SIDE - qi) <= DIL_SIDE, 0.0, MASK_VALUE).astype(F32)
        groups.append(dict(dil=dil, n=n, bq=bq, nk=nk, nsub=n // bq, kj=kj, band=band,
                           kpad=-(n + 2 * DIL_SIDE) % LANES, refs=refs[7 * g:7 * g + 7]))
        blocks += [(g, c, b) for c in range(dil) for b in range(n // bq)]

    class_kv = {}

    def keys_values(g, c):
        if (g, c) not in class_kv:
            G = groups[g]
            _, kc_ref, kp_ref, kn_ref, vc_ref, vp_ref, vn_ref = G["refs"]
            kparts = [kp_ref[0, 0, c], kc_ref[0, 0, c], kn_ref[0, 0, c]]
            if G["kpad"]:
                kparts.append(jnp.zeros((G["kpad"], LANES), BF16))
            kcat_t = jnp.concatenate(kparts, axis=0).astype(F32).T.astype(BF16)
            vcat = jnp.concatenate([vp_ref[0, 0, c], vc_ref[0, 0, c], vn_ref[0, 0, c]], axis=0)
            class_kv[(g, c)] = (kcat_t, vcat)
        return class_kv[(g, c)]

    def scores(blk):
        g, c, b = blk
        G = groups[g]
        bq, nk, kj = G["bq"], G["nk"], G["kj"]
        p0 = b * bq
        bias = G["band"]
        if b == 0:
            bias = jnp.where(kj >= jnp.where(i == 0, DIL_SIDE, 0), bias, MASK_VALUE)
        if b == G["nsub"] - 1:
            bias = jnp.where(kj < jnp.where(i == nchunks - 1, bq + DIL_SIDE, nk), bias, MASK_VALUE)
        qf = G["refs"][0][0, 0, c, p0:p0 + bq, :].astype(F32)
        qs = jnp.concatenate([jnp.where(lo_qk, qf, 0.0), jnp.where(lo_qk, 0.0, qf)], axis=0)
        return _dot(qs.astype(BF16), keys_values(g, c)[0][:, p0:p0 + nk]) + bias

    def finish(blk, s):
        g, c, b = blk
        G = groups[g]
        bq, nk, dil = G["bq"], G["nk"], G["dil"]
        p0 = b * bq
        m = jnp.max(s, axis=1, keepdims=True)
        p = jnp.exp2(s - m)
        l = jnp.sum(p, axis=1, keepdims=True)
        o2 = _dot(p.astype(BF16), keys_values(g, c)[1][p0:p0 + nk]) / l
        lse2 = m + jnp.log2(l)
        o_pair = jnp.where(lo, o2[:bq], o2[bq:])
        l_pair = jnp.where(lo, lse2[:bq], lse2[bq:])
        rows = pl.ds(p0, bq) if dil == 1 else pl.ds(p0 * dil + c, bq, stride=dil)
        osc_ref[g, rows, :] = o_pair
        lsc_ref[g, rows, :] = l_pair

    s_next = scores(blocks[0])
    for idx, blk in enumerate(blocks):
        s_cur = s_next
        if idx + 1 < len(blocks):
            s_next = scores(blocks[idx + 1])
        finish(blk, s_cur)

    ls = [lsc_ref[g] for g in range(ng)]
    mx = functools.reduce(jnp.maximum, ls)
    ws = [jnp.exp2(l - mx) for l in ls]
    num = sum(w * osc_ref[g] for g, w in enumerate(ws))
    o_ref[...] = (num / sum(ws)).astype(o_ref.dtype)


def _dilated_attention(qkvs):
    nchunks = qkvs[0].shape[1]
    S = nchunks * DIL_CHUNK
    in_specs, args = [], []
    for dil, a in zip(DIL_DILATIONS, qkvs):
        n = DIL_CHUNK // dil
        nblk = n // DIL_SIDE
        full = (1, 1, dil, n, LANES)
        halo = (1, 1, dil, DIL_SIDE, LANES)

        def cur(t):
            return lambda i, m: (t, i, 0, 0, m)

        def prev(t, nblk=nblk):
            return lambda i, m: (t, jnp.maximum(i - 1, 0), 0, nblk - 1, m)

        def nxt(t):
            return lambda i, m: (t, jnp.minimum(i + 1, nchunks - 1), 0, 0, m)

        in_specs += [pl.BlockSpec(full, cur(0)),
                     pl.BlockSpec(full, cur(1)), pl.BlockSpec(halo, prev(1)), pl.BlockSpec(halo, nxt(1)),
                     pl.BlockSpec(full, cur(2)), pl.BlockSpec(halo, prev(2)), pl.BlockSpec(halo, nxt(2))]
        args += [a] * 7
    ng = len(DIL_DILATIONS)
    return pl.pallas_call(
        functools.partial(_dil_kernel, nchunks=nchunks),
        grid=(nchunks, D_MODEL // LANES),
        in_specs=in_specs,
        out_specs=pl.BlockSpec((DIL_CHUNK, LANES), lambda i, m: (i, m)),
        out_shape=jax.ShapeDtypeStruct((S, D_MODEL), F32),
        scratch_shapes=[pltpu.VMEM((ng, DIL_CHUNK, LANES), F32),
                        pltpu.VMEM((ng, DIL_CHUNK, LANES), F32)],
        compiler_params=_cparams("parallel", "parallel"),
        name="dilated_attention",
    )(*args)


def _epilogue_kernel(br_ref, h_ref, wt_ref, mk_ref, mv_ref, wo_ref, g_ref, b_ref, o_ref, *, nsplit):
    lo = _lane_lo()
    tr = h_ref.shape[0]
    rs = tr // nsplit

    def pre_norm(rows):
        h = h_ref[rows, :]
        hb = h.astype(BF16)
        gate_b = _dot(hb, wt_ref[:, :D_MODEL])
        yb = (br_ref[rows, :].astype(F32) * (gate_b * jax.nn.sigmoid(gate_b))).astype(BF16)
        y = _dot(yb, wo_ref[:D_MODEL, :])
        tail_m = _dot(hb, wt_ref[:, D_MODEL:])
        yms = []
        for t in range(MEM_WIDTH // LANES):
            cols = slice(t * LANES, (t + 1) * LANES)
            qf = tail_m[:, MEM_WIDTH + t * LANES:MEM_WIDTH + (t + 1) * LANES] * ATTN_SCALE
            kt, vt = mk_ref[:, cols], mv_ref[:, cols]
            outs = []
            for qh in (jnp.where(lo, qf, 0.0), jnp.where(lo, 0.0, qf)):
                s = _dot_nt(qh.astype(BF16), kt)
                m = jnp.max(s, axis=1, keepdims=True)
                p = jnp.exp(s - m)
                l = jnp.sum(p, axis=1, keepdims=True)
                outs.append(_dot(p.astype(BF16), vt) / l)
            mem_out = jnp.where(lo, outs[0], outs[1])
            gm = tail_m[:, cols]
            yms.append((mem_out * (gm * jax.nn.sigmoid(gm))).astype(BF16))
        y = y + _dot(jnp.concatenate(yms, axis=1), wo_ref[D_MODEL:, :])
        return DEEPNORM_ALPHA * h + y

    z_next = pre_norm(pl.ds(0, rs))
    for r in range(nsplit):
        z = z_next
        if r + 1 < nsplit:
            z_next = pre_norm(pl.ds((r + 1) * rs, rs))
        o_ref[pl.ds(r * rs, rs), :] = _layer_norm_rows(z, g_ref[...], b_ref[...])


def _epilogue(branch, h, w_tail, mk, mv, w_out, g, b, tail_block=0, tr=1024, nsplit=2):
    S = h.shape[0]
    row = lambda i: (i, 0)
    fixed = lambda i: (0, 0)
    return pl.pallas_call(
        functools.partial(_epilogue_kernel, nsplit=nsplit),
        grid=(S // tr,),
        in_specs=[pl.BlockSpec((tr, D_MODEL), row),
                  pl.BlockSpec((tr, D_MODEL), row),
                  pl.BlockSpec((D_MODEL, TAIL), lambda i: (0, tail_block)),
                  pl.BlockSpec((N_MEM, MEM_WIDTH), fixed),
                  pl.BlockSpec((N_MEM, MEM_WIDTH), fixed),
                  pl.BlockSpec((INNER, D_MODEL), fixed),
                  pl.BlockSpec((1, D_MODEL), fixed),
                  pl.BlockSpec((1, D_MODEL), fixed)],
        out_specs=pl.BlockSpec((tr, D_MODEL), row),
        out_shape=jax.ShapeDtypeStruct((S, D_MODEL), F32),
        compiler_params=_cparams("parallel"),
        name="epilogue",
    )(branch, h, w_tail, mk, mv, w_out, g.reshape(1, -1), b.reshape(1, -1))


def _dup_kv_columns(w):
    d = w.shape[0]
    w4 = w.reshape(d, GQA_KV_HEADS, 1, HEAD_DIM)
    return jnp.broadcast_to(w4, (d, GQA_KV_HEADS, 2, HEAD_DIM)).reshape(d, 2 * GQA_KV_HEADS * HEAD_DIM)


def kernel(x, mem, ln_in_g, ln_in_b, w_mem_kv, w_in_a, w_in_b, q_norm_g, k_norm_g, w_in_c, w_out, ln_g, ln_b):
    B, S, D = x.shape
    assert B == 1 and D == D_MODEL and S % DIL_CHUNK == 0 and S % (FFT_N2 * SUBLANES) == 0

    mkv = _proj(mem[0], w_mem_kv.astype(BF16), name="proj_mem").astype(BF16)
    mk, mv = mkv[:, :MEM_WIDTH], mkv[:, MEM_WIDTH:]
    h = None
    fft_tabs = None
    qw = GQA_Q_HEADS * HEAD_DIM
    kvw = GQA_KV_HEADS * HEAD_DIM

    for i in range(DEPTH):
        kind, j = i % 3, i // 3
        tail_block = 0
        if kind == 0:
            if fft_tabs is None:
                fft_tabs = _fft_tables(S)
            fc, gk, hm = fft_tabs
            w_tail = w_in_a[j].astype(BF16)
            if i == 0:
                h, z = _ln_proj_a(x[0], ln_in_g, ln_in_b, fc)
            else:
                z = _proj_a(h, fc)
            branch = _fft_stage2(_fft_stage1(z, gk), hm)
        elif kind == 1:
            w = w_in_b[j]
            half = HEAD_DIM // 4
            w_qk = jnp.concatenate([w[:, :qw], _dup_kv_columns(w[:, qw:qw + kvw])], axis=1)
            w_qkv = jnp.concatenate([_pair_rotary_layout(w_qk, half), w[:, qw + kvw:qw + 2 * kvw]],
                                    axis=1).astype(BF16)
            cos_t, sin_t = (jnp.asarray(_pair_rotary_layout(t, half), F32) for t in _rope_tables_axial(S))
            gq = _pair_rotary_layout(jnp.tile(q_norm_g[j], 2), half).reshape(1, LANES)
            gkn = _pair_rotary_layout(jnp.tile(k_norm_g[j], 2), half).reshape(1, LANES)
            qk, vt = _proj_b(h, w_qkv, cos_t, sin_t, gq, gkn)
            w_tail = w[:, qw + 2 * kvw:].astype(BF16)
            branch = _gqa_attention(qk, vt)
        else:
            half = HEAD_DIM // 2
            n_qk = 2 * len(DIL_DILATIONS) * D_MODEL
            w_qk = _prep_qk_weights(w_in_c[j], n_qk, half)
            w_tail = w_in_c[j][:, n_qk:].astype(BF16)
            tail_block = (w_tail.shape[1] - TAIL) // TAIL
            cos_t, sin_t = (_pair_rotary_layout(t, half) for t in _rope_tables_1d(S))
            qkvs = [_proj_c(h, w_qk, w_tail, cos_t, sin_t, g, dil) for g, dil in enumerate(DIL_DILATIONS)]
            branch = _dilated_attention(qkvs)
        h = _epilogue(branch, h, w_tail, mk, mv, w_out[i].astype(BF16), ln_g[i], ln_b[i], tail_block)
    return h[None]
```

```python
import functools

import jax
import jax.numpy as jnp
import numpy as np
from jax import lax
from jax.experimental import pallas as pl
from jax.experimental.pallas import tpu as pltpu

F32 = jnp.float32
BF16 = jnp.bfloat16

D_MODEL = 1024
DEPTH = 4
N_MEM = 256
GRID_W = 64
HEAD_DIM = 64
ROPE_THETA = 10000.0
LN_EPS = 1e-5
RMS_EPS = 1e-6
MASK_VALUE = -1e30
FNET_GROUPS = 4
FNET_GROUP_W = D_MODEL // FNET_GROUPS
GQA_Q_HEADS = 16
GQA_KV_HEADS = 4
DIL_DILATIONS = (1, 4, 16)
DIL_SIDE = 64
MEM_WIDTH = 256
INNER = D_MODEL + MEM_WIDTH
TAIL = INNER + MEM_WIDTH
DEEPNORM_ALPHA = (2.0 * DEPTH) ** 0.25
ATTN_SCALE = HEAD_DIM ** -0.5
LOG2_E = float(np.log2(np.e))

BF16_SUBLANES = 16
VT_ROWS = HEAD_DIM + BF16_SUBLANES
LANES = 128
SUBLANES = 8
FFT_N2 = 256
DIL_CHUNK = 1024
VMEM_LIMIT = 48 << 20


def _cparams(*sem):
    return pltpu.CompilerParams(dimension_semantics=sem, vmem_limit_bytes=VMEM_LIMIT)


def _dot(a, b):
    return jnp.dot(a, b, preferred_element_type=F32)


def _dot_nt(a, b):
    return lax.dot_general(a, b, (((1,), (1,)), ((), ())), preferred_element_type=F32)


def _lane_lo(shape=(1, LANES)):
    return lax.broadcasted_iota(jnp.int32, shape, len(shape) - 1) % LANES < HEAD_DIM


def _layer_norm_rows(z, g, b):
    mu = jnp.mean(z, axis=-1, keepdims=True)
    zc = z - mu
    var = jnp.mean(zc * zc, axis=-1, keepdims=True)
    return zc * lax.rsqrt(var + LN_EPS) * g + b


def _proj_kernel(x_ref, w_ref, o_ref):
    o_ref[...] = _dot(x_ref[...].astype(BF16), w_ref[...])


def _proj(x, w, tr=512, name="proj"):
    S, K = x.shape
    N = w.shape[1]
    tr = min(tr, S)
    return pl.pallas_call(
        _proj_kernel,
        grid=(S // tr,),
        in_specs=[pl.BlockSpec((tr, K), lambda i: (i, 0)),
                  pl.BlockSpec((K, N), lambda i: (0, 0))],
        out_specs=pl.BlockSpec((tr, N), lambda i: (i, 0)),
        out_shape=jax.ShapeDtypeStruct((S, N), F32),
        compiler_params=_cparams("parallel"),
        name=name,
    )(x, w)


def _proj_a_kernel(x_ref, fc_ref, z_ref):
    xb = x_ref[...].astype(BF16)
    for g in range(FNET_GROUPS):
        cols = slice(g * FNET_GROUP_W, (g + 1) * FNET_GROUP_W)
        zg = _dot(xb[:, cols], fc_ref[...])
        z_ref[0, :, cols] = zg[:, :FNET_GROUP_W]
        z_ref[1, :, cols] = zg[:, FNET_GROUP_W:]


def _ln_proj_a_kernel(x_ref, g_ref, b_ref, fc_ref, h_ref, z_ref):
    h = _layer_norm_rows(x_ref[...], g_ref[...], b_ref[...])
    h_ref[...] = h
    hb = h.astype(BF16)
    for g in range(FNET_GROUPS):
        cols = slice(g * FNET_GROUP_W, (g + 1) * FNET_GROUP_W)
        zg = _dot(hb[:, cols], fc_ref[...])
        z_ref[0, :, cols] = zg[:, :FNET_GROUP_W]
        z_ref[1, :, cols] = zg[:, FNET_GROUP_W:]


def _ln_proj_a(x, g, b, fc, tr=512):
    S = x.shape[0]
    return pl.pallas_call(
        _ln_proj_a_kernel,
        grid=(S // tr,),
        in_specs=[pl.BlockSpec((tr, D_MODEL), lambda i: (i, 0)),
                  pl.BlockSpec((1, D_MODEL), lambda i: (0, 0)),
                  pl.BlockSpec((1, D_MODEL), lambda i: (0, 0)),
                  pl.BlockSpec((FNET_GROUP_W, 2 * FNET_GROUP_W), lambda i: (0, 0))],
        out_specs=[pl.BlockSpec((tr, D_MODEL), lambda i: (i, 0)),
                   pl.BlockSpec((2, tr, D_MODEL), lambda i: (0, i, 0))],
        out_shape=[jax.ShapeDtypeStruct((S, D_MODEL), F32),
                   jax.ShapeDtypeStruct((2, S, D_MODEL), F32)],
        compiler_params=_cparams("parallel"),
        name="ln_proj_a",
    )(x, g.reshape(1, -1), b.reshape(1, -1), fc)


def _proj_a(h, fc, tr=512):
    S = h.shape[0]
    return pl.pallas_call(
        _proj_a_kernel,
        grid=(S // tr,),
        in_specs=[pl.BlockSpec((tr, D_MODEL), lambda i: (i, 0)),
                  pl.BlockSpec((FNET_GROUP_W, 2 * FNET_GROUP_W), lambda i: (0, 0))],
        out_specs=pl.BlockSpec((2, tr, D_MODEL), lambda i: (0, i, 0)),
        out_shape=jax.ShapeDtypeStruct((2, S, D_MODEL), F32),
        compiler_params=_cparams("parallel"),
        name="proj_a",
    )(h, fc)


def _fft1_kernel(z_ref, g_ref, t_ref):
    two, n1, _, sub, tc = z_ref.shape
    x = z_ref[...].reshape(two * n1 * sub, tc).astype(BF16)
    t_ref[...] = _dot(g_ref[...], x).reshape(t_ref.shape)


def _fft_stage1(z, gk, tc=1024):
    _, S, D = z.shape
    n1 = S // FFT_N2
    nu = FFT_N2 // SUBLANES
    z5 = z.reshape(2, n1, nu, SUBLANES, D)
    rows = 2 * n1 * SUBLANES
    t5 = pl.pallas_call(
        _fft1_kernel,
        grid=(nu, D // tc),
        in_specs=[pl.BlockSpec((2, n1, 1, SUBLANES, tc), lambda u, c: (0, 0, u, 0, c)),
                  pl.BlockSpec((rows, rows), lambda u, c: (0, 0))],
        out_specs=pl.BlockSpec((n1, 2, 1, SUBLANES, tc), lambda u, c: (0, 0, u, 0, c)),
        out_shape=jax.ShapeDtypeStruct((n1, 2, nu, SUBLANES, D), F32),
        compiler_params=_cparams("parallel", "parallel"),
        name="fft_stage1",
    )(z5, gk)
    return t5.reshape(n1, 2, FFT_N2, D)


def _fft2_kernel(t_ref, h_ref, o_ref, slab_ref):
    nj, _, n2, d = t_ref.shape
    nslab = d // LANES
    for j in range(nj):
        tj = t_ref[j].reshape(2 * n2, d).astype(BF16)
        r = _dot(h_ref[j], tj)
        for s in range(nslab):
            slab_ref[s, pl.ds(j, n2, stride=nj), :] = r[:, s * LANES:(s + 1) * LANES]
    for s in range(nslab):
        o_ref[:, :, s * LANES:(s + 1) * LANES] = slab_ref[s].reshape(n2, nj, LANES)


def _fft_stage2(t, hmat, tc=512):
    n1, _, n2, D = t.shape
    nj = SUBLANES
    y3 = pl.pallas_call(
        _fft2_kernel,
        grid=(n1 // nj, D // tc),
        in_specs=[pl.BlockSpec((nj, 2, n2, tc), lambda a, c: (a, 0, 0, c)),
                  pl.BlockSpec((nj, n2, 2 * n2), lambda a, c: (a, 0, 0))],
        out_specs=pl.BlockSpec((n2, nj, tc), lambda a, c: (0, a, c)),
        out_shape=jax.ShapeDtypeStruct((n2, n1, D), F32),
        scratch_shapes=[pltpu.VMEM((tc // LANES, n2 * nj, LANES), F32)],
        compiler_params=_cparams("parallel", "parallel"),
        name="fft_stage2",
    )(t, hmat)
    return y3.reshape(n2 * n1, D)


def _fft_tables(S):
    n1, n2 = S // FFT_N2, FFT_N2
    c = np.arange(FNET_GROUP_W)
    ang = (2.0 * np.pi / FNET_GROUP_W) * ((c[:, None] * c[None, :]) % FNET_GROUP_W)
    scale = 1.0 / np.sqrt(float(S) * FNET_GROUP_W)
    fc = np.concatenate([np.cos(ang), -np.sin(ang)], axis=1) * scale
    k1 = np.arange(n1)
    th = (2.0 * np.pi / n1) * ((k1[:, None] * k1[None, :]) % n1)
    cs, sn = np.cos(th), np.sin(th)
    g = np.stack([np.stack([cs, sn], axis=1), np.stack([-sn, cs], axis=1)], axis=1)
    gk = np.kron(g.reshape(2 * n1, 2 * n1), np.eye(SUBLANES))
    k2 = np.arange(n2)
    kk = k1[:, None, None] + n1 * k2[None, :, None]
    ph = (2.0 * np.pi / S) * ((k2[None, None, :] * kk) % S)
    hm = np.concatenate([np.cos(ph), np.sin(ph)], axis=2)
    return tuple(jnp.asarray(t, F32).astype(BF16) for t in (fc, gk, hm))


def _rope_angles(pos, dim):
    inv_freq = ROPE_THETA ** (-(np.arange(0, dim, 2, dtype=np.float64) / dim))
    return pos.astype(np.float64)[:, None] * inv_freq[None, :]


def _rope_tables_axial(S):
    t = np.arange(S)
    ar = _rope_angles(t // GRID_W, HEAD_DIM // 2)
    ac = _rope_angles(t % GRID_W, HEAD_DIM // 2)
    cos = np.concatenate([np.cos(ar), np.cos(ar), np.cos(ac), np.cos(ac)], axis=1)
    sin = np.concatenate([-np.sin(ar), np.sin(ar), -np.sin(ac), np.sin(ac)], axis=1)
    return np.tile(cos, (1, 2)), np.tile(sin, (1, 2))


def _rope_tables_1d(S):
    a = _rope_angles(np.arange(S), HEAD_DIM)
    cos = np.concatenate([np.cos(a), np.cos(a)], axis=1)
    sin = np.concatenate([-np.sin(a), np.sin(a)], axis=1)
    return np.tile(cos, (1, 2)), np.tile(sin, (1, 2))


def _pair_rotary_layout(a, half):
    lead = a.shape[:-1]
    a6 = a.reshape(*lead, a.shape[-1] // LANES, 2, HEAD_DIM // (2 * half), 2, half)
    xp = jnp if isinstance(a, jax.Array) else np
    return xp.moveaxis(a6, -2, -4).reshape(*lead, a.shape[-1])


def _pair_rotary_lanes(x, half):
    nblk = LANES // half
    groups = HEAD_DIM // (2 * half)
    lane_blk = lax.broadcasted_iota(jnp.int32, (1, LANES), 1) // half
    moves = {}
    for j in range(nblk):
        fs, rem = divmod(j, nblk // 2)
        head, group = divmod(rem, groups)
        s = head * (HEAD_DIM // half) + 2 * group + fs
        moves.setdefault(((j - s) * half) % LANES, []).append(j)
    out = x
    for shift, dst in moves.items():
        if shift:
            mask = functools.reduce(jnp.logical_or, [lane_blk == j for j in dst])
            out = jnp.where(mask, pltpu.roll(x, shift, 1), out)
    return out


def _prep_qk_kernel(w_ref, o_ref, *, half):
    for t in range(o_ref.shape[1] // LANES):
        cols = slice(t * LANES, (t + 1) * LANES)
        o_ref[:, cols] = _pair_rotary_lanes(w_ref[:, cols], half).astype(BF16)


def _prep_qk_weights(w, ncols, half, tc=1024):
    d = w.shape[0]
    return pl.pallas_call(
        functools.partial(_prep_qk_kernel, half=half),
        grid=(ncols // tc,),
        in_specs=[pl.BlockSpec((d, tc), lambda c: (0, c))],
        out_specs=pl.BlockSpec((d, tc), lambda c: (0, c)),
        out_shape=jax.ShapeDtypeStruct((d, ncols), BF16),
        compiler_params=_cparams("parallel"),
        name="prep_qk_weights",
    )(w)


def _lane_qk():
    return lax.broadcasted_iota(jnp.int32, (1, LANES), 1) % HEAD_DIM < HEAD_DIM // 2


def _rotate_partner(x):
    return pltpu.roll(x, HEAD_DIM, 1)


def _proj_b_kernel(x_ref, w_ref, cos_ref, sin_ref, gq_ref, gk_ref, o_ref, vt_ref):
    xb = x_ref[...].astype(BF16)
    cos_t, sin_t = cos_ref[...], sin_ref[...]
    lo = _lane_qk()
    nb_w = 4 * LANES
    vt = _dot(xb, w_ref[:, 3 * nb_w:]).T.astype(BF16)
    ones = jnp.ones((VT_ROWS - HEAD_DIM, vt.shape[1]), BF16)
    for hd in range(GQA_KV_HEADS):
        vt_ref[hd * VT_ROWS:hd * VT_ROWS + HEAD_DIM, :] = vt[hd * HEAD_DIM:(hd + 1) * HEAD_DIM]
        vt_ref[hd * VT_ROWS + HEAD_DIM:(hd + 1) * VT_ROWS, :] = ones
    for nb in range(3):
        r = _dot(xb, w_ref[:, nb * nb_w:(nb + 1) * nb_w])
        gain = gq_ref[...] if nb < 2 else gk_ref[...]
        scale = ATTN_SCALE * LOG2_E if nb < 2 else 1.0
        for t in range(4):
            rt = r[:, t * LANES:(t + 1) * LANES]
            r2 = rt * rt
            tot = jnp.sum(r2, axis=1, keepdims=True)
            low = jnp.sum(jnp.where(lo, r2, 0.0), axis=1, keepdims=True)
            ss = jnp.where(lo, low, tot - low)
            xn = rt * lax.rsqrt(ss * (1.0 / HEAD_DIM) + RMS_EPS) * gain
            out = (xn * cos_t + _rotate_partner(xn) * sin_t) * scale
            c0 = nb * nb_w + t * LANES
            o_ref[:, c0:c0 + LANES] = out.astype(BF16)


def _proj_b(h, w_qkv, cos_t, sin_t, gq, gk, tr=512):
    S = h.shape[0]
    n_in = w_qkv.shape[1]
    n_out = n_in - GQA_KV_HEADS * HEAD_DIM
    vw = GQA_KV_HEADS * VT_ROWS
    return pl.pallas_call(
        _proj_b_kernel,
        grid=(S // tr,),
        in_specs=[pl.BlockSpec((tr, D_MODEL), lambda i: (i, 0)),
                  pl.BlockSpec((D_MODEL, n_in), lambda i: (0, 0)),
                  pl.BlockSpec((tr, LANES), lambda i: (i, 0)),
                  pl.BlockSpec((tr, LANES), lambda i: (i, 0)),
                  pl.BlockSpec((1, LANES), lambda i: (0, 0)),
                  pl.BlockSpec((1, LANES), lambda i: (0, 0))],
        out_specs=[pl.BlockSpec((tr, n_out), lambda i: (i, 0)),
                   pl.BlockSpec((vw, tr), lambda i: (0, i))],
        out_shape=[jax.ShapeDtypeStruct((S, n_out), BF16),
                   jax.ShapeDtypeStruct((vw, S), BF16)],
        compiler_params=_cparams("parallel"),
        name="proj_b",
    )(h, w_qkv, cos_t, sin_t, gq, gk)


def _gqa_kernel(q_ref, k_ref, vt_ref, o_ref, qs_ref, m_ref, acc_ref, *p_refs, tk):
    tq = q_ref.shape[0]
    S = k_ref.shape[0]
    nq = 4 * tq
    nkv = S // tk
    nbuf = len(p_refs)
    lo = _lane_qk()
    for t in range(2):
        qt = q_ref[:, t * LANES:(t + 1) * LANES].astype(F32)
        qs_ref[(2 * t) * tq:(2 * t + 1) * tq, :] = jnp.where(lo, qt, 0.0).astype(BF16)
        qs_ref[(2 * t + 1) * tq:(2 * t + 2) * tq, :] = jnp.where(lo, 0.0, qt).astype(BF16)

    def scores(j):
        k0 = pl.multiple_of(j * tk, tk)
        return _dot_nt(k_ref[pl.ds(k0, tk), :], qs_ref[...])

    def values(j, p):
        return _dot(vt_ref[:, pl.ds(pl.multiple_of(j * tk, tk), tk)], p)

    def colmax(s):
        return jnp.max(jnp.max(s.reshape(tk // SUBLANES, SUBLANES, nq), axis=0), axis=0, keepdims=True)

    m0 = jnp.broadcast_to(colmax(scores(0)), (tk, nq))

    def probs(u):
        p_refs[u % nbuf][...] = jnp.exp2(scores(u) - m0).astype(BF16)

    probs(0)
    pv = None
    for u in range(nkv):
        if u + 1 < nkv:
            probs(u + 1)
        d = values(u, p_refs[u % nbuf][...])
        pv = d if pv is None else pv + d
    safe = jnp.min(jnp.where(jnp.isfinite(pv), 1.0, 0.0)) > 0.5

    @pl.when(safe)
    def _():
        acc_ref[...] = pv

    @pl.when(jnp.logical_not(safe))
    def _():
        m_ref[...] = jnp.full(m_ref.shape, -jnp.inf, F32)
        acc_ref[...] = jnp.zeros(acc_ref.shape, F32)

        def exact_tile(u, c):
            s = scores(u)
            m_prev = m_ref[...]
            m_cur = jnp.maximum(m_prev, colmax(s))
            d = values(u, jnp.exp2(s - m_cur).astype(BF16))
            acc_ref[...] = jnp.exp2(m_prev - m_cur) * acc_ref[...] + d
            m_ref[...] = m_cur
            return c

        lax.fori_loop(0, nkv, exact_tile, 0)

    acc = acc_ref[...]
    ot = acc[:HEAD_DIM] / acc[HEAD_DIM:HEAD_DIM + 1]
    for t in range(2):
        pair = jnp.concatenate([ot[:, (2 * t) * tq:(2 * t + 1) * tq],
                                ot[:, (2 * t + 1) * tq:(2 * t + 2) * tq]], axis=0)
        o_ref[:, t * LANES:(t + 1) * LANES] = pair.T.astype(o_ref.dtype)


def _gqa_attention(qk, vt, tq=256, tk=256, nbuf=2):
    S = qk.shape[0]
    qw = 4 * HEAD_DIM
    k_blk0 = GQA_Q_HEADS * HEAD_DIM // LANES
    tk = min(tk, S)
    return pl.pallas_call(
        functools.partial(_gqa_kernel, tk=tk),
        grid=(GQA_KV_HEADS, S // tq),
        in_specs=[pl.BlockSpec((tq, qw), lambda h, i: (i, h)),
                  pl.BlockSpec((S, LANES), lambda h, i: (0, k_blk0 + h)),
                  pl.BlockSpec((VT_ROWS, S), lambda h, i: (h, 0))],
        out_specs=pl.BlockSpec((tq, qw), lambda h, i: (i, h)),
        out_shape=jax.ShapeDtypeStruct((S, D_MODEL), F32),
        scratch_shapes=[pltpu.VMEM((4 * tq, LANES), BF16),
                        pltpu.VMEM((1, 4 * tq), F32),
                        pltpu.VMEM((VT_ROWS, 4 * tq), F32)]
        + [pltpu.VMEM((tk, 4 * tq), BF16)] * nbuf,
        compiler_params=_cparams("parallel", "parallel"),
        name="gqa_attention",
    )(qk, qk, vt)


def _proj_c_kernel(x_ref, wq_ref, wk_ref, wv_ref, cos_ref, sin_ref, o_ref, xp_ref, slab_ref, *, dil):
    tm, d = x_ref.shape[0], o_ref.shape[-1]
    n = tm // dil
    nslab = d // LANES
    cb_w = 2 * LANES
    ncb = d // cb_w

    if dil == 1:
        xp_ref[...] = x_ref[...].astype(BF16)
    else:
        for s in range(nslab):
            slab_ref[s] = x_ref[:, s * LANES:(s + 1) * LANES]
        for c in range(dil):
            for s in range(nslab):
                xp_ref[c * n:(c + 1) * n, s * LANES:(s + 1) * LANES] = (
                    slab_ref[s, pl.ds(c, n, stride=dil), :].astype(BF16))

    w_refs = (wq_ref, wk_ref, wv_ref)
    cos_k, sin_k = cos_ref[...], sin_ref[...]
    cos_q, sin_q = cos_k * (ATTN_SCALE * LOG2_E), sin_k * (ATTN_SCALE * LOG2_E)

    def matmul(kind, cb):
        return _dot(xp_ref[...], w_refs[kind][:, cb * cb_w:(cb + 1) * cb_w])

    def emit(kind, c0, vals):
        o_ref[kind, 0, :, :, c0:c0 + vals.shape[1]] = vals.astype(BF16).reshape(dil, n, vals.shape[1])

    chunks = [(kind, cb) for kind in range(3) for cb in range(ncb)]
    r_next = matmul(*chunks[0])
    for idx, (kind, cb) in enumerate(chunks):
        r = r_next
        if idx + 1 < len(chunks):
            r_next = matmul(*chunks[idx + 1])
        if kind == 2:
            emit(kind, cb * cb_w, r)
            continue
        cos_t, sin_t = (cos_q, sin_q) if kind == 0 else (cos_k, sin_k)
        for s in range(cb_w // LANES):
            rt = r[:, s * LANES:(s + 1) * LANES]
            emit(kind, cb * cb_w + s * LANES, rt * cos_t + _rotate_partner(rt) * sin_t)


def _regroup_rows(tab, dil):
    S, w = tab.shape
    return tab.reshape(S // DIL_CHUNK, DIL_CHUNK // dil, dil, w).transpose(0, 2, 1, 3).reshape(S, w)


def _proj_c(h, w_qk, w_v, cos_np, sin_np, group, dil):
    S = h.shape[0]
    tm = DIL_CHUNK
    n = tm // dil
    ngroups = len(DIL_DILATIONS)
    cos_t = jnp.asarray(_regroup_rows(cos_np, dil), F32)
    sin_t = jnp.asarray(_regroup_rows(sin_np, dil), F32)
    return pl.pallas_call(
        functools.partial(_proj_c_kernel, dil=dil),
        grid=(S // tm,),
        in_specs=[pl.BlockSpec((tm, D_MODEL), lambda i: (i, 0)),
                  pl.BlockSpec((D_MODEL, D_MODEL), lambda i: (0, group)),
                  pl.BlockSpec((D_MODEL, D_MODEL), lambda i: (0, ngroups + group)),
                  pl.BlockSpec((D_MODEL, D_MODEL), lambda i: (0, group)),
                  pl.BlockSpec((tm, LANES), lambda i: (i, 0)),
                  pl.BlockSpec((tm, LANES), lambda i: (i, 0))],
        out_specs=pl.BlockSpec((3, 1, dil, n, D_MODEL), lambda i: (0, i, 0, 0, 0)),
        out_shape=jax.ShapeDtypeStruct((3, S // tm, dil, n, D_MODEL), BF16),
        scratch_shapes=[pltpu.VMEM((tm, D_MODEL), BF16),
                        pltpu.VMEM((D_MODEL // LANES, tm, LANES), F32)],
        compiler_params=_cparams("parallel"),
        name=f"proj_c_dil{dil}",
    )(h, w_qk, w_qk, w_v, cos_t, sin_t)


def _dil_kernel(*refs, nchunks):
    ng = len(DIL_DILATIONS)
    o_ref, osc_ref, lsc_ref = refs[7 * ng:]
    i = pl.program_id(0)
    lo = _lane_lo()
    lo_qk = _lane_qk()

    groups, blocks = [], []
    for g, dil in enumerate(DIL_DILATIONS):
        n = DIL_CHUNK // dil
        bq = min(n, 2 * DIL_SIDE)
        nk = bq + 2 * DIL_SIDE
        qi = lax.broadcasted_iota(jnp.int32, (2 * bq, nk), 0) % bq
        kj = lax.broadcasted_iota(jnp.int32, (2 * bq, nk), 1)
        band = jnp.where(jnp.abs(kj - DIL_SIDE - qi) <= DIL_SIDE, 0.0, MASK_VALUE).astype(F32)
        groups.append(dict(dil=dil, n=n, bq=bq, nk=nk, nsub=n // bq, kj=kj, band=band,
                           kpad=-(n + 2 * DIL_SIDE) % LANES, refs=refs[7 * g:7 * g + 7]))
        blocks += [(g, c, b) for c in range(dil) for b in range(n // bq)]

    class_kv = {}

    def keys_values(g, c):
        if (g, c) not in class_kv:
            G = groups[g]
            _, kc_ref, kp_ref, kn_ref, vc_ref, vp_ref, vn_ref = G["refs"]
            kparts = [kp_ref[0, 0, c], kc_ref[0, 0, c], kn_ref[0, 0, c]]
            if G["kpad"]:
                kparts.append(jnp.zeros((G["kpad"], LANES), BF16))
            kcat_t = jnp.concatenate(kparts, axis=0).astype(F32).T.astype(BF16)
            vcat = jnp.concatenate([vp_ref[0, 0, c], vc_ref[0, 0, c], vn_ref[0, 0, c]], axis=0)
            class_kv[(g, c)] = (kcat_t, vcat)
        return class_kv[(g, c)]

    def scores(blk):
        g, c, b = blk
        G = groups[g]
        bq, nk, kj = G["bq"], G["nk"], G["kj"]
        p0 = b * bq
        bias = G["band"]
        if b == 0:
            bias = jnp.where(kj >= jnp.where(i == 0, DIL_SIDE, 0), bias, MASK_VALUE)
        if b == G["nsub"] - 1:
            bias = jnp.where(kj < jnp.where(i == nchunks - 1, bq + DIL_SIDE, nk), bias, MASK_VALUE)
        qf = G["refs"][0][0, 0, c, p0:p0 + bq, :].astype(F32)
        qs = jnp.concatenate([jnp.where(lo_qk, qf, 0.0), jnp.where(lo_qk, 0.0, qf)], axis=0)
        return _dot(qs.astype(BF16), keys_values(g, c)[0][:, p0:p0 + nk]) + bias

    def finish(blk, s):
        g, c, b = blk
        G = groups[g]
        bq, nk, dil = G["bq"], G["nk"], G["dil"]
        p0 = b * bq
        m = jnp.max(s, axis=1, keepdims=True)
        p = jnp.exp2(s - m)
        l = jnp.sum(p, axis=1, keepdims=True)
        o2 = _dot(p.astype(BF16), keys_values(g, c)[1][p0:p0 + nk]) / l
        lse2 = m + jnp.log2(l)
        o_pair = jnp.where(lo, o2[:bq], o2[bq:])
        l_pair = jnp.where(lo, lse2[:bq], lse2[bq:])
        rows = pl.ds(p0, bq) if dil == 1 else pl.ds(p0 * dil + c, bq, stride=dil)
        osc_ref[g, rows, :] = o_pair
        lsc_ref[g, rows, :] = l_pair

    s_next = scores(blocks[0])
    for idx, blk in enumerate(blocks):
        s_cur = s_next
        if idx + 1 < len(blocks):
            s_next = scores(blocks[idx + 1])
        finish(blk, s_cur)

    ls = [lsc_ref[g] for g in range(ng)]
    mx = functools.reduce(jnp.maximum, ls)
    ws = [jnp.exp2(l - mx) for l in ls]
    num = sum(w * osc_ref[g] for g, w in enumerate(ws))
    o_ref[...] = (num / sum(ws)).astype(o_ref.dtype)


def _dilated_attention(qkvs):
    nchunks = qkvs[0].shape[1]
    S = nchunks * DIL_CHUNK
    in_specs, args = [], []
    for dil, a in zip(DIL_DILATIONS, qkvs):
        n = DIL_CHUNK // dil
        nblk = n // DIL_SIDE
        full = (1, 1, dil, n, LANES)
        halo = (1, 1, dil, DIL_SIDE, LANES)

        def cur(t):
            return lambda i, m: (t, i, 0, 0, m)

        def prev(t, nblk=nblk):
            return lambda i, m: (t, jnp.maximum(i - 1, 0), 0, nblk - 1, m)

        def nxt(t):
            return lambda i, m: (t, jnp.minimum(i + 1, nchunks - 1), 0, 0, m)

        in_specs += [pl.BlockSpec(full, cur(0)),
                     pl.BlockSpec(full, cur(1)), pl.BlockSpec(halo, prev(1)), pl.BlockSpec(halo, nxt(1)),
                     pl.BlockSpec(full, cur(2)), pl.BlockSpec(halo, prev(2)), pl.BlockSpec(halo, nxt(2))]
        args += [a] * 7
    ng = len(DIL_DILATIONS)
    return pl.pallas_call(
        functools.partial(_dil_kernel, nchunks=nchunks),
        grid=(nchunks, D_MODEL // LANES),
        in_specs=in_specs,
        out_specs=pl.BlockSpec((DIL_CHUNK, LANES), lambda i, m: (i, m)),
        out_shape=jax.ShapeDtypeStruct((S, D_MODEL), F32),
        scratch_shapes=[pltpu.VMEM((ng, DIL_CHUNK, LANES), F32),
                        pltpu.VMEM((ng, DIL_CHUNK, LANES), F32)],
        compiler_params=_cparams("parallel", "parallel"),
        name="dilated_attention",
    )(*args)


def _epilogue_kernel(br_ref, h_ref, wt_ref, mk_ref, mv_ref, wo_ref, g_ref, b_ref, o_ref, *, nsplit):
    lo = _lane_lo()
    tr = h_ref.shape[0]
    rs = tr // nsplit

    def pre_norm(rows):
        h = h_ref[rows, :]
        hb = h.astype(BF16)
        gate_b = _dot(hb, wt_ref[:, :D_MODEL])
        yb = (br_ref[rows, :].astype(F32) * (gate_b * jax.nn.sigmoid(gate_b))).astype(BF16)
        y = _dot(yb, wo_ref[:D_MODEL, :])
        tail_m = _dot(hb, wt_ref[:, D_MODEL:])
        yms = []
        for t in range(MEM_WIDTH // LANES):
            cols = slice(t * LANES, (t + 1) * LANES)
            qf = tail_m[:, MEM_WIDTH + t * LANES:MEM_WIDTH + (t + 1) * LANES] * ATTN_SCALE
            kt, vt = mk_ref[:, cols], mv_ref[:, cols]
            outs = []
            for qh in (jnp.where(lo, qf, 0.0), jnp.where(lo, 0.0, qf)):
                s = _dot_nt(qh.astype(BF16), kt)
                m = jnp.max(s, axis=1, keepdims=True)
                p = jnp.exp(s - m)
                l = jnp.sum(p, axis=1, keepdims=True)
                outs.append(_dot(p.astype(BF16), vt) / l)
            mem_out = jnp.where(lo, outs[0], outs[1])
            gm = tail_m[:, cols]
            yms.append((mem_out * (gm * jax.nn.sigmoid(gm))).astype(BF16))
        y = y + _dot(jnp.concatenate(yms, axis=1), wo_ref[D_MODEL:, :])
        return DEEPNORM_ALPHA * h + y

    z_next = pre_norm(pl.ds(0, rs))
    for r in range(nsplit):
        z = z_next
        if r + 1 < nsplit:
            z_next = pre_norm(pl.ds((r + 1) * rs, rs))
        o_ref[pl.ds(r * rs, rs), :] = _layer_norm_rows(z, g_ref[...], b_ref[...])


def _epilogue(branch, h, w_tail, mk, mv, w_out, g, b, tail_block=0, tr=1024, nsplit=2):
    S = h.shape[0]
    row = lambda i: (i, 0)
    fixed = lambda i: (0, 0)
    return pl.pallas_call(
        functools.partial(_epilogue_kernel, nsplit=nsplit),
        grid=(S // tr,),
        in_specs=[pl.BlockSpec((tr, D_MODEL), row),
                  pl.BlockSpec((tr, D_MODEL), row),
                  pl.BlockSpec((D_MODEL, TAIL), lambda i: (0, tail_block)),
                  pl.BlockSpec((N_MEM, MEM_WIDTH), fixed),
                  pl.BlockSpec((N_MEM, MEM_WIDTH), fixed),
                  pl.BlockSpec((INNER, D_MODEL), fixed),
                  pl.BlockSpec((1, D_MODEL), fixed),
                  pl.BlockSpec((1, D_MODEL), fixed)],
        out_specs=pl.BlockSpec((tr, D_MODEL), row),
        out_shape=jax.ShapeDtypeStruct((S, D_MODEL), F32),
        compiler_params=_cparams("parallel"),
        name="epilogue",
    )(branch, h, w_tail, mk, mv, w_out, g.reshape(1, -1), b.reshape(1, -1))


def _dup_kv_columns(w):
    d = w.shape[0]
    w4 = w.reshape(d, GQA_KV_HEADS, 1, HEAD_DIM)
    return jnp.broadcast_to(w4, (d, GQA_KV_HEADS, 2, HEAD_DIM)).reshape(d, 2 * GQA_KV_HEADS * HEAD_DIM)


def kernel(x, mem, ln_in_g, ln_in_b, w_mem_kv, w_in_a, w_in_b, q_norm_g, k_norm_g, w_in_c, w_out, ln_g, ln_b):
    B, S, D = x.shape
    assert B == 1 and D == D_MODEL and S % DIL_CHUNK == 0 and S % (FFT_N2 * SUBLANES) == 0

    mkv = _proj(mem[0], w_mem_kv.astype(BF16), name="proj_mem").astype(BF16)
    mk, mv = mkv[:, :MEM_WIDTH], mkv[:, MEM_WIDTH:]
    h = None
    fft_tabs = None
    qw = GQA_Q_HEADS * HEAD_DIM
    kvw = GQA_KV_HEADS * HEAD_DIM

    for i in range(DEPTH):
        kind, j = i % 3, i // 3
        tail_block = 0
        if kind == 0:
            if fft_tabs is None:
                fft_tabs = _fft_tables(S)
            fc, gk, hm = fft_tabs
            w_tail = w_in_a[j].astype(BF16)
            if i == 0:
                h, z = _ln_proj_a(x[0], ln_in_g, ln_in_b, fc)
            else:
                z = _proj_a(h, fc)
            branch = _fft_stage2(_fft_stage1(z, gk), hm)
        elif kind == 1:
            w = w_in_b[j]
            half = HEAD_DIM // 4
            w_qk = jnp.concatenate([w[:, :qw], _dup_kv_columns(w[:, qw:qw + kvw])], axis=1)
            w_qkv = jnp.concatenate([_pair_rotary_layout(w_qk, half), w[:, qw + kvw:qw + 2 * kvw]],
                                    axis=1).astype(BF16)
            cos_t, sin_t = (jnp.asarray(_pair_rotary_layout(t, half), F32) for t in _rope_tables_axial(S))
            gq = _pair_rotary_layout(jnp.tile(q_norm_g[j], 2), half).reshape(1, LANES)
            gkn = _pair_rotary_layout(jnp.tile(k_norm_g[j], 2), half).reshape(1, LANES)
            qk, vt = _proj_b(h, w_qkv, cos_t, sin_t, gq, gkn)
            w_tail = w[:, qw + 2 * kvw:].astype(BF16)
            branch = _gqa_attention(qk, vt)
        else:
            half = HEAD_DIM // 2
            n_qk = 2 * len(DIL_DILATIONS) * D_MODEL
            w_qk = _prep_qk_weights(w_in_c[j], n_qk, half)
            w_tail = w_in_c[j][:, n_qk:].astype(BF16)
            tail_block = (w_tail.shape[1] - TAIL) // TAIL
            cos_t, sin_t = (_pair_rotary_layout(t, half) for t in _rope_tables_1d(S))
            qkvs = [_proj_c(h, w_qk, w_tail, cos_t, sin_t, g, dil) for g, dil in enumerate(DIL_DILATIONS)]
            branch = _dilated_attention(qkvs)
        h = _epilogue(branch, h, w_tail, mk, mv, w_out[i].astype(BF16), ln_g[i], ln_b[i], tail_block)
    return h[None]
```
